```python
import math
import jax
import jax.numpy as jnp
from jax import lax
import numpy as np

D_MODEL = 1024
BATCH = 2
SEQ = 8192
DEPTH = 4
DEC_BATCH = 32
DEC_SEQ = 1
PAST_LEN = 8192
PAGE_SIZE = 128

D_MIX = D_MODEL
POOL_DIM = D_MIX // 4
POOL_WINDOWS = (2, 4, 8, 16)
POOL_GROUP = POOL_DIM // len(POOL_WINDOWS)
POOL_BUF = max(POOL_WINDOWS) - 1
NSA_DIM = D_MIX // 2
NSA_HEAD_DIM = 64
NSA_HEADS = NSA_DIM // NSA_HEAD_DIM
NSA_KV_GROUPS = 2
NSA_REP = NSA_HEADS // NSA_KV_GROUPS
KV_WIDTH = NSA_KV_GROUPS * NSA_HEAD_DIM
CMP_BLOCK = 32
CMP_STRIDE = 16
SEL_BLOCK = 64
SEL_TOPN = 16
WINDOW = 512
Q_BLOCK = 128
SEL_BONUS = 1.0e4
NEG_INF = -1.0e30
SSM_DIM = D_MIX - POOL_DIM - NSA_DIM
SSM_HEAD_DIM = 64
SSM_HEADS = SSM_DIM // SSM_HEAD_DIM
SSM_GROUPS = 2
SSM_STATE = 128
SSM_CONV = 4
SSM_CHUNK = 128
SSM_CONV_DIM = SSM_DIM + 2 * SSM_GROUPS * SSM_STATE
D_FF = 2816
PLE_DIM = 256
ROPE_THETA = 10000.0
EPS = 1e-6
IN_SPLITS = (POOL_DIM, NSA_DIM, 6 * KV_WIDTH, 3 * NSA_HEADS, SSM_DIM, SSM_CONV_DIM, SSM_HEADS)
D_IN = sum(IN_SPLITS)

kernel_name = 'hybrid_pool_nsa_ssd_macaron_step'


def rmsnorm(x, g):
    xf = x.astype(jnp.float32)
    y = xf * lax.rsqrt(jnp.mean(xf * xf, axis=-1, keepdims=True) + EPS)
    return (y * g.astype(jnp.float32)).astype(x.dtype)


def swiglu(x, wg, wu, wd):
    return (jax.nn.silu(x @ wg) * (x @ wu)) @ wd


def rope(x, pos):
    half = x.shape[-1] // 2
    inv = ROPE_THETA ** (-jnp.arange(half, dtype=jnp.float32) / half)
    ang = pos.astype(jnp.float32)[:, None] * inv[None, :]
    cos = jnp.cos(ang)[None, :, None, :]
    sin = jnp.sin(ang)[None, :, None, :]
    xf = x.astype(jnp.float32)
    x1, x2 = xf[..., :half], xf[..., half:]
    return jnp.concatenate([x1 * cos - x2 * sin, x2 * cos + x1 * sin], axis=-1).astype(x.dtype)


def pool_mixer(u, buf, pos, w, scale):
    B, T, _ = u.shape
    ext = jnp.concatenate([buf, u], axis=1)
    extf = ext.astype(jnp.float32)
    cs = jnp.concatenate([jnp.zeros_like(extf[:, :1]), jnp.cumsum(extf, axis=1)], axis=1)
    end = cs[:, POOL_BUF + 1:]
    outs = []
    for gi, win in enumerate(POOL_WINDOWS):
        sl = slice(gi * POOL_GROUP, (gi + 1) * POOL_GROUP)
        start = cs[:, POOL_BUF + 1 - win:POOL_BUF + 1 - win + T, sl]
        cnt = jnp.minimum(pos + 1, win).astype(jnp.float32)[None, :, None]
        outs.append((end[..., sl] - start) / cnt - extf[:, POOL_BUF:, sl])
    d = jnp.stack(outs, axis=2).astype(u.dtype)
    y = jnp.einsum('btgc,gce->btge', d, w).reshape(B, T, POOL_DIM) * scale
    return y, ext[:, -POOL_BUF:]


def nsa_compress(rows, pe, w):
    B, L, G, hd = rows.shape
    ratio = CMP_BLOCK // CMP_STRIDE
    nseg = L // CMP_STRIDE
    nc = nseg - ratio + 1
    seg = rows[:, :nseg * CMP_STRIDE].reshape(B, nseg, CMP_STRIDE, G, hd)
    wseg = w.reshape(ratio, CMP_STRIDE, hd, hd)
    out = jnp.einsum('jd,jde->e', pe, w)
    for r in range(ratio):
        out = out + jnp.einsum('bnjgd,jde->bnge', seg[:, r:r + nc], wseg[r])
    cend = jnp.arange(nc, dtype=jnp.int32) * CMP_STRIDE + CMP_BLOCK - 1
    return out, cend


def nsa_global_keys(rows, cmp_pe, cmp_w):
    kc, cend = nsa_compress(rows[:, 0], cmp_pe[0], cmp_w[0])
    vc, _ = nsa_compress(rows[:, 1], cmp_pe[1], cmp_w[1])
    pad = (-rows.shape[2]) % SEL_BLOCK
    sel = jnp.pad(rows[:, 2:4], ((0, 0), (0, 0), (0, pad), (0, 0), (0, 0)))
    return kc, vc, cend, sel[:, 0], sel[:, 1]


def nsa_attend(q, qpos, kc, vc, cend, ks, vs, kw, vw, wpos, gates):
    B, T, G, R, hd = q.shape
    f32 = jnp.float32
    scale = hd ** -0.5
    s = jnp.einsum('btgrd,bcgd->btgrc', q, kc, preferred_element_type=f32) * scale
    cmask = (cend[None, :] <= qpos[:, None])[None, :, None, None, :]
    p_cmp = jax.nn.softmax(jnp.where(cmask, s, NEG_INF), axis=-1) * cmask
    o_cmp = jnp.einsum('btgrc,bcgd->btgrd', p_cmp.astype(vc.dtype), vc)
    nc = kc.shape[1]
    ns = ks.shape[1] // SEL_BLOCK
    c_start = jnp.arange(nc, dtype=jnp.int32) * CMP_STRIDE
    s_start = jnp.arange(ns, dtype=jnp.int32) * SEL_BLOCK
    overlap = ((c_start[:, None] < s_start[None, :] + SEL_BLOCK)
               & (c_start[:, None] + CMP_BLOCK > s_start[None, :])).astype(f32)
    imp = jnp.einsum('btgc,cs->btgs', p_cmp.sum(axis=3), overlap)
    cur = (qpos // SEL_BLOCK)[:, None]
    blk = jnp.arange(ns, dtype=jnp.int32)[None, :]
    valid = s_start[None, :] <= qpos[:, None]
    forced = ((blk == 0) | (blk == cur) | (blk == cur - 1)).astype(f32)
    score = jnp.where(valid[None, :, None, :], imp + SEL_BONUS * forced[None, :, None, :], -1.0)
    _, idx = lax.top_k(score, min(SEL_TOPN, ns))
    bi = jnp.arange(B)[:, None, None, None]
    gi = jnp.arange(G)[None, None, :, None]
    kg = ks.reshape(B, ns, SEL_BLOCK, G, hd)[bi, idx, :, gi, :]
    vg = vs.reshape(B, ns, SEL_BLOCK, G, hd)[bi, idx, :, gi, :]
    kpos = idx[..., None] * SEL_BLOCK + jnp.arange(SEL_BLOCK, dtype=jnp.int32)
    smask = (kpos <= qpos[None, :, None, None, None])[:, :, :, None]
    s = jnp.einsum('btgrd,btgnkd->btgrnk', q, kg, preferred_element_type=f32) * scale
    p = jax.nn.softmax(jnp.where(smask, s, NEG_INF).reshape(B, T, G, R, -1), axis=-1).reshape(s.shape)
    o_slc = jnp.einsum('btgrnk,btgnkd->btgrd', p.astype(vg.dtype), vg)
    s = jnp.einsum('btgrd,blgd->btgrl', q, kw, preferred_element_type=f32) * scale
    wmask = ((wpos[None, :] <= qpos[:, None]) & (wpos[None, :] >= qpos[:, None] - WINDOW)
             & (wpos[None, :] >= 0))[None, :, None, None, :]
    p = jax.nn.softmax(jnp.where(wmask, s, NEG_INF), axis=-1)
    o_win = jnp.einsum('btgrl,blgd->btgrd', p.astype(vw.dtype), vw)
    return gates[..., 0:1] * o_cmp + gates[..., 1:2] * o_slc + gates[..., 2:3] * o_win


def nsa_prompt(q, rows, win_rows, gates, cmp_pe, cmp_w):
    B, T = q.shape[:2]
    kc, vc, cend, ks, vs = nsa_global_keys(rows, cmp_pe, cmp_w)
    wpad = jnp.pad(win_rows, ((0, 0), (0, 0), (WINDOW, 0), (0, 0), (0, 0)))

    def block(i):
        t0 = i * Q_BLOCK
        qb = lax.dynamic_slice_in_dim(q, t0, Q_BLOCK, axis=1)
        gb = lax.dynamic_slice_in_dim(gates, t0, Q_BLOCK, axis=1)
        wb = lax.dynamic_slice_in_dim(wpad, t0, WINDOW + Q_BLOCK, axis=2)
        qpos = t0 + jnp.arange(Q_BLOCK, dtype=jnp.int32)
        wpos = t0 - WINDOW + jnp.arange(WINDOW + Q_BLOCK, dtype=jnp.int32)
        return nsa_attend(qb, qpos, kc, vc, cend, ks, vs, wb[:, 0], wb[:, 1], wpos, gb)

    out = lax.map(block, jnp.arange(T // Q_BLOCK, dtype=jnp.int32))
    return jnp.moveaxis(out, 0, 1).reshape(B, T, NSA_DIM)


def causal_conv(xbc, buf, w, b):
    ext = jnp.concatenate([buf, xbc], axis=1)
    out = lax.conv_general_dilated(ext, w[:, None, :], window_strides=(1,), padding='VALID',
                                   dimension_numbers=('NWC', 'WIO', 'NWC'),
                                   feature_group_count=xbc.shape[-1])
    return jax.nn.silu(out + b), ext[:, -(SSM_CONV - 1):]


def ssd(x, dt, a, bm, cm, h0):
    f32 = jnp.float32
    Bsz, T = x.shape[:2]
    q = min(SSM_CHUNK, T)
    nck = -(-T // q)
    pad = nck * q - T
    rep = SSM_HEADS // SSM_GROUPS

    def chunks(z):
        z = jnp.pad(z.astype(f32), [(0, 0), (0, pad)] + [(0, 0)] * (z.ndim - 2))
        return z.reshape((Bsz, nck, q) + z.shape[2:])

    xc, dtc = chunks(x), chunks(dt)
    bc, cc = chunks(jnp.repeat(bm, rep, axis=2)), chunks(jnp.repeat(cm, rep, axis=2))
    acum = jnp.cumsum(dtc * a, axis=2)
    seg = acum[:, :, :, None, :] - acum[:, :, None, :, :]
    causal = jnp.tril(jnp.ones((q, q), dtype=bool))[None, None, :, :, None]
    lmat = jnp.exp(jnp.where(causal, seg, -jnp.inf))
    xdt = xc * dtc[..., None]
    scores = jnp.einsum('bcihn,bcjhn->bcijh', cc, bc) * lmat
    y_diag = jnp.einsum('bcijh,bcjhp->bcihp', scores, xdt)
    decay_end = jnp.exp(acum[:, :, -1:, :] - acum)
    states = jnp.einsum('bcjhn,bcjh,bcjhp->bchpn', bc, decay_end, xdt)
    chunk_decay = jnp.exp(acum[:, :, -1, :])

    def step(h, inp):
        dec, st = inp
        return dec[:, :, None, None] * h + st, h

    h_final, h_in = lax.scan(step, h0.astype(f32),
                             (jnp.moveaxis(chunk_decay, 1, 0), jnp.moveaxis(states, 1, 0)))
    h_in = jnp.moveaxis(h_in, 0, 1)
    y_off = jnp.einsum('bcihn,bchpn->bcihp', cc, h_in) * jnp.exp(acum)[..., None]
    y = (y_diag + y_off).reshape(Bsz, nck * q, SSM_HEADS, SSM_HEAD_DIM)[:, :T]
    return y, h_final


def mamba_mixer(z, xbc, dt_raw, conv_buf, h0, conv_w, conv_b, dt_bias, a_log, d_skip, norm_g):
    f32 = jnp.float32
    B, T, _ = z.shape
    xbc_act, new_conv = causal_conv(xbc, conv_buf, conv_w, conv_b)
    xs, bm, cm = jnp.split(xbc_act, [SSM_DIM, SSM_DIM + SSM_GROUPS * SSM_STATE], axis=-1)
    xs = xs.reshape(B, T, SSM_HEADS, SSM_HEAD_DIM)
    bm = bm.reshape(B, T, SSM_GROUPS, SSM_STATE)
    cm = cm.reshape(B, T, SSM_GROUPS, SSM_STATE)
    dt = jax.nn.softplus(dt_raw.astype(f32) + dt_bias.astype(f32))
    a = -jnp.exp(a_log.astype(f32))
    y, h = ssd(xs, dt, a, bm, cm, h0)
    y = y + d_skip.astype(f32)[:, None] * xs.astype(f32)
    y = (y.reshape(B, T, SSM_DIM) * jax.nn.silu(z.astype(f32))).reshape(B, T, SSM_GROUPS, SSM_DIM // SSM_GROUPS)
    y = y * lax.rsqrt(jnp.mean(y * y, axis=-1, keepdims=True) + EPS)
    y = y.reshape(B, T, SSM_DIM) * norm_g.astype(f32)
    return y.astype(z.dtype), new_conv, h.astype(h0.dtype)


def trunk_layer(h, pe, start, past, lw):
    B, T, _ = h.shape
    pos = start + jnp.arange(T, dtype=jnp.int32)
    h = h + 0.5 * swiglu(rmsnorm(h, lw['ffn1_norm']), lw['ffn1_w_gate'], lw['ffn1_w_up'], lw['ffn1_w_down'])
    n = rmsnorm(h, lw['mix_norm'])
    offs = [int(v) for v in np.cumsum(IN_SPLITS)[:-1]]
    u, q, kv, gl, z, xbc, dtr = jnp.split(n @ lw['w_in'], offs, axis=-1)
    if past is None:
        pool_buf = jnp.zeros((B, POOL_BUF, POOL_DIM), h.dtype)
        conv_buf = jnp.zeros((B, SSM_CONV - 1, SSM_CONV_DIM), h.dtype)
        h0 = jnp.zeros((B, SSM_HEADS, SSM_HEAD_DIM, SSM_STATE), jnp.float32)
    else:
        nsa_past, win_past, pool_buf, conv_buf, h0 = past
    y_pool, new_pool = pool_mixer(u, pool_buf, pos, lw['pool_w'], lw['pool_scale'])
    qh = rope(rmsnorm(q.reshape(B, T, NSA_HEADS, NSA_HEAD_DIM), lw['nsa_q_norm']), pos)
    qh = qh.reshape(B, T, NSA_KV_GROUPS, NSA_REP, NSA_HEAD_DIM)
    kv = kv.reshape(B, T, 6, NSA_KV_GROUPS, NSA_HEAD_DIM)
    kn = lw['nsa_k_norm']
    k_cmp = rope(rmsnorm(kv[:, :, 0], kn[0]), pos)
    k_slc = rope(rmsnorm(kv[:, :, 2], kn[1]), pos)
    k_win = rope(rmsnorm(kv[:, :, 4], kn[2]), pos)
    rows = jnp.stack([k_cmp, kv[:, :, 1], k_slc, kv[:, :, 3]], axis=1)
    win_new = jnp.stack([k_win, kv[:, :, 5]], axis=1)
    gates = jax.nn.sigmoid(gl).reshape(B, T, NSA_KV_GROUPS, NSA_REP, 3)
    if past is None:
        y_nsa = nsa_prompt(qh, rows, win_new, gates, lw['nsa_cmp_pe'], lw['nsa_cmp_w'])
        new_win = win_new[:, :, -min(WINDOW, T):]
    else:
        full = jnp.concatenate([nsa_past, rows], axis=2)
        kc, vc, cend, ks, vs = nsa_global_keys(full, lw['nsa_cmp_pe'], lw['nsa_cmp_w'])
        wfull = jnp.concatenate([win_past, win_new], axis=2)
        wbuf = win_past.shape[2]
        wpos = start - wbuf + jnp.arange(wbuf + T, dtype=jnp.int32)
        y_nsa = nsa_attend(qh, pos, kc, vc, cend, ks, vs, wfull[:, 0], wfull[:, 1], wpos, gates)
        y_nsa = y_nsa.reshape(B, T, NSA_DIM)
        new_win = wfull[:, :, -wbuf:]
    y_ssm, new_conv, new_h = mamba_mixer(z, xbc, dtr, conv_buf, h0, lw['ssm_conv_w'], lw['ssm_conv_b'],
                                         lw['ssm_dt_bias'], lw['ssm_a_log'], lw['ssm_d'], lw['ssm_norm'])
    h = h + jnp.concatenate([y_pool, y_nsa, y_ssm], axis=-1) @ lw['w_out']
    h = h + 0.5 * swiglu(rmsnorm(h, lw['ffn2_norm']), lw['ffn2_w_gate'], lw['ffn2_w_up'], lw['ffn2_w_down'])
    h = h + jax.nn.sigmoid(rmsnorm(h, lw['ple_norm']) @ lw['ple_w_gate']) * (pe @ lw['ple_w_proj'])
    return h, (rows, new_win, new_pool, new_conv, new_h)


def setup_inputs(seed: int = 0) -> dict:
    key = jax.random.key(seed)
    keys = jax.random.split(key, 48)
    ctr = [0]

    def nk():
        ctr[0] += 1
        return keys[ctr[0] - 1]

    def nrm(shape, scale=1.0):
        return scale * jax.random.normal(nk(), shape, jnp.float32)

    def gain(shape):
        return 1.0 + nrm(shape, 0.05)

    n_pages = PAST_LEN // PAGE_SIZE
    n_used = DEC_BATCH * n_pages
    n_pool = n_used + n_used // 4
    wbuf = min(WINDOW, PAST_LEN)
    hd, G = NSA_HEAD_DIM, NSA_KV_GROUPS
    x_prompt = nrm((BATCH, SEQ, D_MODEL))
    x_sample = nrm((DEC_BATCH, DEC_SEQ, D_MODEL))
    cache_nsa_kv = nrm((n_pool, DEPTH, 4, PAGE_SIZE, G, hd))
    cache_win_kv = nrm((DEC_BATCH, DEPTH, 2, wbuf, G, hd))
    state_pool = nrm((DEC_BATCH, DEPTH, POOL_BUF, POOL_DIM))
    state_conv = nrm((DEC_BATCH, DEPTH, SSM_CONV - 1, SSM_CONV_DIM))
    state_ssm = nrm((DEC_BATCH, DEPTH, SSM_HEADS, SSM_HEAD_DIM, SSM_STATE), 0.5)
    page_table = jax.random.permutation(nk(), n_pool)[:n_used].reshape(DEC_BATCH, n_pages).astype(jnp.int32)
    p_prompt = nrm((DEPTH, BATCH, SEQ, PLE_DIM))
    p_sample = nrm((DEPTH, DEC_BATCH, DEC_SEQ, PLE_DIM))
    ffn1_norm = gain((DEPTH, D_MODEL))
    ffn1_w_gate = nrm((DEPTH, D_MODEL, D_FF), D_MODEL ** -0.5)
    ffn1_w_up = nrm((DEPTH, D_MODEL, D_FF), D_MODEL ** -0.5)
    ffn1_w_down = nrm((DEPTH, D_FF, D_MODEL), D_FF ** -0.5)
    mix_norm = gain((DEPTH, D_MODEL))
    w_in = nrm((DEPTH, D_MODEL, D_IN), D_MODEL ** -0.5)
    w_out = nrm((DEPTH, D_MIX, D_MODEL), D_MIX ** -0.5)
    pool_w = nrm((DEPTH, len(POOL_WINDOWS), POOL_GROUP, POOL_GROUP), POOL_GROUP ** -0.5)
    pool_scale = 1.0 + nrm((DEPTH, POOL_DIM), 0.1)
    nsa_q_norm = gain((DEPTH, hd))
    nsa_k_norm = gain((DEPTH, 3, hd))
    nsa_cmp_pe = nrm((DEPTH, 2, CMP_BLOCK, hd), 0.5)
    nsa_cmp_w = nrm((DEPTH, 2, CMP_BLOCK, hd, hd), (CMP_BLOCK * hd) ** -0.5)
    ssm_conv_w = nrm((DEPTH, SSM_CONV, SSM_CONV_DIM), SSM_CONV ** -0.5)
    ssm_conv_b = nrm((DEPTH, SSM_CONV_DIM), 0.01)
    dt0 = jnp.exp(jax.random.uniform(nk(), (DEPTH, SSM_HEADS), jnp.float32, math.log(1e-3), math.log(1e-1)))
    ssm_dt_bias = dt0 + jnp.log(-jnp.expm1(-dt0))
    ssm_a_log = jnp.log(jax.random.uniform(nk(), (DEPTH, SSM_HEADS), jnp.float32, 1.0, 16.0))
    ssm_d = 1.0 + nrm((DEPTH, SSM_HEADS), 0.1)
    ssm_norm = gain((DEPTH, SSM_DIM))
    ffn2_norm = gain((DEPTH, D_MODEL))
    ffn2_w_gate = nrm((DEPTH, D_MODEL, D_FF), D_MODEL ** -0.5)
    ffn2_w_up = nrm((DEPTH, D_MODEL, D_FF), D_MODEL ** -0.5)
    ffn2_w_down = nrm((DEPTH, D_FF, D_MODEL), D_FF ** -0.5)
    ple_norm = gain((DEPTH, D_MODEL))
    ple_w_gate = nrm((DEPTH, D_MODEL, D_MODEL), D_MODEL ** -0.5)
    ple_w_proj = nrm((DEPTH, PLE_DIM, D_MODEL), PLE_DIM ** -0.5)
    return {
        'x_prompt': x_prompt, 'x_sample': x_sample,
        'cache_nsa_kv': cache_nsa_kv, 'cache_win_kv': cache_win_kv,
        'state_pool': state_pool, 'state_conv': state_conv, 'state_ssm': state_ssm,
        'page_table': page_table, 'p_prompt': p_prompt, 'p_sample': p_sample,
        'ffn1_norm': ffn1_norm, 'ffn1_w_gate': ffn1_w_gate, 'ffn1_w_up': ffn1_w_up, 'ffn1_w_down': ffn1_w_down,
        'mix_norm': mix_norm, 'w_in': w_in, 'w_out': w_out,
        'pool_w': pool_w, 'pool_scale': pool_scale,
        'nsa_q_norm': nsa_q_norm, 'nsa_k_norm': nsa_k_norm, 'nsa_cmp_pe': nsa_cmp_pe, 'nsa_cmp_w': nsa_cmp_w,
        'ssm_conv_w': ssm_conv_w, 'ssm_conv_b': ssm_conv_b, 'ssm_dt_bias': ssm_dt_bias,
        'ssm_a_log': ssm_a_log, 'ssm_d': ssm_d, 'ssm_norm': ssm_norm,
        'ffn2_norm': ffn2_norm, 'ffn2_w_gate': ffn2_w_gate, 'ffn2_w_up': ffn2_w_up, 'ffn2_w_down': ffn2_w_down,
        'ple_norm': ple_norm, 'ple_w_gate': ple_w_gate, 'ple_w_proj': ple_w_proj,
    }


def reference(x_prompt, x_sample, cache_nsa_kv, cache_win_kv, state_pool, state_conv, state_ssm,
              page_table, p_prompt, p_sample,
              ffn1_norm, ffn1_w_gate, ffn1_w_up, ffn1_w_down,
              mix_norm, w_in, w_out, pool_w, pool_scale,
              nsa_q_norm, nsa_k_norm, nsa_cmp_pe, nsa_cmp_w,
              ssm_conv_w, ssm_conv_b, ssm_dt_bias, ssm_a_log, ssm_d, ssm_norm,
              ffn2_norm, ffn2_w_gate, ffn2_w_up, ffn2_w_down,
              ple_norm, ple_w_gate, ple_w_proj):
    db = x_sample.shape[0]
    past_len = page_table.shape[1] * PAGE_SIZE
    h_p, h_s = x_prompt, x_sample
    st_p = [[], [], [], [], []]
    st_s = [[], [], [], [], []]
    for i in range(DEPTH):
        lw = {
            'ffn1_norm': ffn1_norm[i], 'ffn1_w_gate': ffn1_w_gate[i], 'ffn1_w_up': ffn1_w_up[i],
            'ffn1_w_down': ffn1_w_down[i], 'mix_norm': mix_norm[i], 'w_in': w_in[i], 'w_out': w_out[i],
            'pool_w': pool_w[i], 'pool_scale': pool_scale[i], 'nsa_q_norm': nsa_q_norm[i],
            'nsa_k_norm': nsa_k_norm[i], 'nsa_cmp_pe': nsa_cmp_pe[i], 'nsa_cmp_w': nsa_cmp_w[i],
            'ssm_conv_w': ssm_conv_w[i], 'ssm_conv_b': ssm_conv_b[i], 'ssm_dt_bias': ssm_dt_bias[i],
            'ssm_a_log': ssm_a_log[i], 'ssm_d': ssm_d[i], 'ssm_norm': ssm_norm[i],
            'ffn2_norm': ffn2_norm[i], 'ffn2_w_gate': ffn2_w_gate[i], 'ffn2_w_up': ffn2_w_up[i],
            'ffn2_w_down': ffn2_w_down[i], 'ple_norm': ple_norm[i], 'ple_w_gate': ple_w_gate[i],
            'ple_w_proj': ple_w_proj[i],
        }
        h_p, new_p = trunk_layer(h_p, p_prompt[i], 0, None, lw)
        nsa_past = cache_nsa_kv[page_table, i]
        nsa_past = jnp.moveaxis(nsa_past, 2, 1).reshape(db, 4, past_len, NSA_KV_GROUPS, NSA_HEAD_DIM)
        past = (nsa_past, cache_win_kv[:, i], state_pool[:, i], state_conv[:, i], state_ssm[:, i])
        h_s, new_s = trunk_layer(h_s, p_sample[i], past_len, past, lw)
        for j in range(5):
            st_p[j].append(new_p[j])
            st_s[j].append(new_s[j])
    new_nsa_kv_prompt = jnp.stack(st_p[0], axis=1)
    new_nsa_kv_sample = jnp.stack(st_s[0], axis=1)
    new_win_kv_prompt = jnp.stack(st_p[1], axis=1)
    new_win_kv_sample = jnp.stack(st_s[1], axis=1)
    new_pool_prompt = jnp.stack(st_p[2], axis=1)
    new_pool_sample = jnp.stack(st_s[2], axis=1)
    new_conv_prompt = jnp.stack(st_p[3], axis=1)
    new_conv_sample = jnp.stack(st_s[3], axis=1)
    new_ssm_prompt = jnp.stack(st_p[4], axis=1)
    new_ssm_sample = jnp.stack(st_s[4], axis=1)
    return (h_p, h_s, new_nsa_kv_prompt, new_nsa_kv_sample, new_win_kv_prompt, new_win_kv_sample,
            new_pool_prompt, new_pool_sample, new_conv_prompt, new_conv_sample,
            new_ssm_prompt, new_ssm_sample)
```

```python
import functools

import jax
import jax.numpy as jnp
import numpy as np
from jax import lax
from jax.experimental import pallas as pl
from jax.experimental.pallas import tpu as pltpu

F32 = jnp.float32
BF16 = jnp.bfloat16

POOL_WINDOWS = (2, 4, 8, 16)
POOL_BUF = 15
HEAD_DIM = 64
NSA_HEADS = 8
KV_GROUPS = 2
NSA_REP = NSA_HEADS // KV_GROUPS
CMP_BLOCK = 32
CMP_STRIDE = 16
SEL_BLOCK = 64
SEL_TOPN = 16
WINDOW = 512
SEL_BONUS = 1.0e4
NEG_INF = -1.0e30
SSM_HEADS = 4
SSM_GROUPS = 2
SSM_STATE = 128
SSM_CONV = 4
SSM_CHUNK = 128
ROPE_THETA = 10000.0
EPS = 1e-6
PAGE_SIZE = 128

LANES = 128
SUBLANES = 8
VMEM_LIMIT_BYTES = 56 * 1024 * 1024

GATE_LANES = 3 * NSA_HEADS
DT_LANE0 = 32

NT_DIMS = (((1,), (1,)), ((), ()))
TN_DIMS = (((0,), (0,)), ((), ()))


def _cparams(*sem):
    return pltpu.CompilerParams(dimension_semantics=sem, vmem_limit_bytes=VMEM_LIMIT_BYTES)


def _const_spec(shape):
    nd = len(shape)
    return pl.BlockSpec(shape, lambda *_: (0,) * nd)


def _rmsnorm(x, g):
    ms = jnp.mean(x * x, axis=-1, keepdims=True)
    return x * lax.rsqrt(ms + EPS) * g


def _silu(x):
    return x * jax.nn.sigmoid(x)


def _split3(x):
    hi = x.astype(BF16)
    r = x - hi.astype(F32)
    mid = r.astype(BF16)
    lo = (r - mid.astype(F32)).astype(BF16)
    return hi, mid, lo


def _ffn_kernel(x_ref, g_ref, wg_ref, wu_ref, wd_ref, o_ref, *, n_chunks):
    x = x_ref[...]
    xn = _rmsnorm(x, g_ref[...]).astype(BF16)
    fc = wg_ref.shape[1] // n_chunks
    tot = None
    for c in range(n_chunks):
        sl = slice(c * fc, (c + 1) * fc)
        g = jnp.dot(xn, wg_ref[:, sl], preferred_element_type=F32)
        u = jnp.dot(xn, wu_ref[:, sl], preferred_element_type=F32)
        a = (_silu(g) * u).astype(BF16)
        d = jnp.dot(a, wd_ref[sl, :], preferred_element_type=F32)
        tot = d if tot is None else tot + d
    o_ref[...] = x + 0.5 * tot


def _row_tile(m, pref):
    return pref if m % pref == 0 else m


def ffn_halfstep(h, g, wg, wu, wd):
    m, d = h.shape
    f = wg.shape[1]
    tm = _row_tile(m, 512)
    return pl.pallas_call(
        functools.partial(_ffn_kernel, n_chunks=2),
        grid=(m // tm,),
        in_specs=[pl.BlockSpec((tm, d), lambda i: (i, 0)),
                  _const_spec((1, d)), _const_spec((d, f)), _const_spec((d, f)), _const_spec((f, d))],
        out_specs=pl.BlockSpec((tm, d), lambda i: (i, 0)),
        out_shape=jax.ShapeDtypeStruct((m, d), F32),
        compiler_params=_cparams("parallel"),
    )(h, g, wg, wu, wd)


def _rope128(x, cos, sin_signed, lane):
    rot = jnp.where((lane & 32) == 0, pltpu.roll(x, 96, 1), pltpu.roll(x, 32, 1))
    return x * cos + rot * sin_signed


def _inproj_kernel(x_ref, g_ref, wu_ref, wq_ref, wkv_ref, wgd_ref, wz_ref, wx_ref,
                   qn_ref, kn_ref, dtb_ref, cos_ref, sin_ref,
                   u_ref, q_ref, kv_ref, kvb_ref, gd_ref, z_ref, xbc_ref):
    x = x_ref[...]
    tm = x.shape[0]
    xn = _rmsnorm(x, g_ref[...]).astype(BF16)
    cos = cos_ref[...]
    sin = sin_ref[...]
    lane = lax.broadcasted_iota(jnp.int32, (tm, LANES), 1)
    inv_hd = 1.0 / HEAD_DIM

    u_ref[...] = jnp.dot(xn, wu_ref[...], preferred_element_type=F32)
    z_ref[...] = jnp.dot(xn, wz_ref[...], preferred_element_type=F32)
    xbc_ref[...] = jnp.dot(xn, wx_ref[...], preferred_element_type=F32)

    y = jnp.dot(xn, wgd_ref[...], preferred_element_type=F32)
    yd = y + dtb_ref[...]
    softplus = jnp.maximum(yd, 0.0) + jnp.log1p(jnp.exp(-jnp.abs(yd)))
    gd_ref[...] = jnp.where(lane < DT_LANE0, jax.nn.sigmoid(y), softplus)

    q = jnp.dot(xn, wq_ref[...], preferred_element_type=F32)
    qn = qn_ref[...]
    for hh in range(NSA_HEADS):
        s = q[:, hh * LANES:(hh + 1) * LANES]
        ms = jnp.sum(s * s, axis=-1, keepdims=True) * inv_hd
        s = s * lax.rsqrt(ms + EPS) * qn
        s = _rope128(s, cos, sin, lane) * (HEAD_DIM ** -0.5)
        q_ref[:, hh * LANES:(hh + 1) * LANES] = s.astype(BF16)

    kv = jnp.dot(xn, wkv_ref[...], preferred_element_type=F32)
    low = lane < HEAD_DIM
    for j in range(6):
        s = kv[:, j * LANES:(j + 1) * LANES]
        if j % 2 == 0:
            sq = s * s
            s_all = jnp.sum(sq, axis=-1, keepdims=True)
            s_low = jnp.sum(jnp.where(low, sq, 0.0), axis=-1, keepdims=True)
            ms = jnp.where(low, s_low, s_all - s_low) * inv_hd
            s = s * lax.rsqrt(ms + EPS) * kn_ref[j // 2:j // 2 + 1, :]
            s = _rope128(s, cos, sin, lane)
        kv_ref[:, j * LANES:(j + 1) * LANES] = s
        kvb_ref[:, j * LANES:(j + 1) * LANES] = s.astype(BF16)


def in_projection(h, lw, cos, sin):
    m, d = h.shape
    tm = _row_tile(m, 512)
    row = lambda n: pl.BlockSpec((tm, n), lambda i: (i, 0))
    ws = [lw["w_u"], lw["w_q"], lw["w_kv"], lw["w_gd"], lw["w_z"], lw["w_xbc"]]
    outs = [(256, F32), (NSA_HEADS * LANES, BF16), (6 * LANES, F32), (6 * LANES, BF16),
            (LANES, F32), (256, F32), (768, F32)]
    return pl.pallas_call(
        _inproj_kernel,
        grid=(m // tm,),
        in_specs=[row(d), _const_spec((1, d))] + [_const_spec(w.shape) for w in ws]
                 + [_const_spec((1, LANES)), _const_spec((3, LANES)), _const_spec((1, LANES)),
                    row(LANES), row(LANES)],
        out_specs=[row(n) for n, _ in outs],
        out_shape=[jax.ShapeDtypeStruct((m, n), dt) for n, dt in outs],
        compiler_params=_cparams("parallel"),
    )(h, lw["mix_norm"], *ws, lw["q_norm"], lw["k_norm"], lw["dt_bias"], cos, sin)


def _outproj_kernel(h_ref, yp_ref, yn_ref, ys_ref, wp_ref, wn_ref, ws_ref, o_ref):
    acc = jnp.dot(yp_ref[...].astype(BF16), wp_ref[...], preferred_element_type=F32)
    acc = acc + jnp.dot(yn_ref[...].astype(BF16), wn_ref[...], preferred_element_type=F32)
    acc = acc + jnp.dot(ys_ref[...].astype(BF16), ws_ref[...], preferred_element_type=F32)
    o_ref[...] = h_ref[...] + acc


def out_projection(h, y_pool, y_nsa, y_ssm, lw):
    m, d = h.shape
    tm = _row_tile(m, 512)
    row = lambda n: pl.BlockSpec((tm, n), lambda i: (i, 0))
    ws = [lw["w_out_pool"], lw["w_out_nsa"], lw["w_out_ssm"]]
    return pl.pallas_call(
        _outproj_kernel,
        grid=(m // tm,),
        in_specs=[row(d), row(y_pool.shape[1]), row(y_nsa.shape[1]), row(y_ssm.shape[1])]
                 + [_const_spec(w.shape) for w in ws],
        out_specs=row(d),
        out_shape=jax.ShapeDtypeStruct((m, d), F32),
        compiler_params=_cparams("parallel"),
    )(h, y_pool, y_nsa, y_ssm, *ws)


def _ple_kernel(h_ref, pe_ref, g_ref, wg_ref, wp_ref, o_ref):
    h = h_ref[...]
    xn = _rmsnorm(h, g_ref[...]).astype(BF16)
    gate = jax.nn.sigmoid(jnp.dot(xn, wg_ref[...], preferred_element_type=F32))
    proj = jnp.dot(pe_ref[...].astype(BF16), wp_ref[...], preferred_element_type=F32)
    o_ref[...] = h + gate * proj


def ple_step(h, pe, lw):
    m, d = h.shape
    tm = _row_tile(m, 512)
    row = lambda n: pl.BlockSpec((tm, n), lambda i: (i, 0))
    return pl.pallas_call(
        _ple_kernel,
        grid=(m // tm,),
        in_specs=[row(d), row(pe.shape[1]), _const_spec((1, d)),
                  _const_spec(lw["ple_w_gate"].shape), _const_spec(lw["ple_w_proj"].shape)],
        out_specs=row(d),
        out_shape=jax.ShapeDtypeStruct((m, d), F32),
        compiler_params=_cparams("parallel"),
    )(h, pe, lw["ple_norm"], lw["ple_w_gate"], lw["ple_w_proj"])


POOL_HALO = 2 * SUBLANES


def _pool_kernel(u_ref, w_ref, sc_ref, y_ref, ext_ref, *, tm):
    c = pl.program_id(1)

    @pl.when(c == 0)
    def _():
        ext_ref[0:POOL_HALO, :] = jnp.zeros((POOL_HALO, ext_ref.shape[1]), F32)

    ext_ref[POOL_HALO:POOL_HALO + tm, :] = u_ref[0]
    lane = lax.broadcasted_iota(jnp.int32, (tm, LANES), 1)
    low = lane < HEAD_DIM
    pos1 = c * tm + lax.broadcasted_iota(jnp.int32, (tm, LANES), 0) + 1
    ds = []
    for slab, (w_lo, w_hi) in enumerate(((POOL_WINDOWS[0], POOL_WINDOWS[1]), (POOL_WINDOWS[2], POOL_WINDOWS[3]))):
        cols = slice(slab * LANES, (slab + 1) * LANES)
        x = ext_ref[POOL_HALO:POOL_HALO + tm, cols]
        run = x
        s_lo = None
        for k in range(1, w_hi):
            run = run + ext_ref[POOL_HALO - k:POOL_HALO - k + tm, cols]
            if k == w_lo - 1:
                s_lo = run
        cnt = jnp.where(low, jnp.minimum(pos1, w_lo), jnp.minimum(pos1, w_hi)).astype(F32)
        ds.append(jnp.where(low, s_lo, run) / cnt - x)
    d = jnp.concatenate(ds, axis=1).astype(BF16)
    y_ref[0] = jnp.dot(d, w_ref[...], preferred_element_type=F32) * sc_ref[...]
    ext_ref[0:POOL_HALO, :] = ext_ref[tm:tm + POOL_HALO, :]


def pool_prompt(u, lw):
    b, t, ch = u.shape
    tm = _row_tile(t, 512)
    return pl.pallas_call(
        functools.partial(_pool_kernel, tm=tm),
        grid=(b, t // tm),
        in_specs=[pl.BlockSpec((1, tm, ch), lambda i, j: (i, j, 0)),
                  _const_spec((ch, ch)), _const_spec((1, ch))],
        out_specs=pl.BlockSpec((1, tm, ch), lambda i, j: (i, j, 0)),
        out_shape=jax.ShapeDtypeStruct((b, t, ch), F32),
        scratch_shapes=[pltpu.VMEM((POOL_HALO + tm, ch), F32)],
        compiler_params=_cparams("parallel", "arbitrary"),
    )(u, lw["pool_w"], lw["pool_scale"])


def _compress_kernel(seg_ref, w_ref, pe_ref, o_ref):
    y = jnp.dot(seg_ref[0], w_ref[0], preferred_element_type=F32)
    pe = jnp.dot(pe_ref[0], w_ref[0], preferred_element_type=F32)
    nseg = y.shape[0]
    second = pltpu.roll(y[:, LANES:], nseg - 1, 0)
    row = lax.broadcasted_iota(jnp.int32, (nseg, LANES), 0)
    second = jnp.where(row < nseg - 1, second, 0.0)
    out = y[:, :LANES] + second + pe[0:1, :LANES] + pe[1:2, LANES:]
    o_ref[0, 0] = out.astype(o_ref.dtype)


def nsa_compress(seg, w, pe):
    _, b, nseg, width = seg.shape
    return pl.pallas_call(
        _compress_kernel,
        grid=(2, b),
        in_specs=[pl.BlockSpec((None, 1, nseg, width), lambda j, i: (j, i, 0, 0)),
                  pl.BlockSpec((1, width, 2 * LANES), lambda j, i: (j, 0, 0)),
                  pl.BlockSpec((1, SUBLANES, width), lambda j, i: (j, 0, 0))],
        out_specs=pl.BlockSpec((1, 1, nseg, LANES), lambda j, i: (j, i, 0, 0)),
        out_shape=jax.ShapeDtypeStruct((2, b, nseg, LANES), BF16),
        compiler_params=_cparams("parallel", "parallel"),
    )(seg, w, pe)


Q_TILE = 128
KEY_CHUNK = 256
SEL_PAD = 128


def _topk_mask_t(score, n_sel):
    blk = lax.broadcasted_iota(jnp.int32, score.shape, 0)

    def body(_, carry):
        sc, sel = carry
        m = jnp.max(sc, axis=0, keepdims=True)
        idx = jnp.min(jnp.where(sc == m, blk, score.shape[0]), axis=0, keepdims=True)
        hit = blk == idx
        return jnp.where(hit, -jnp.inf, sc), jnp.where(hit, 1.0, sel)

    _, sel = lax.fori_loop(0, n_sel, body, (score, jnp.zeros(score.shape, F32)))
    return sel


def _softmax_rows(s, mask):
    sm = jnp.where(mask, s, NEG_INF)
    m = jnp.max(sm, axis=-1, keepdims=True)
    e = jnp.exp(sm - m)
    return e / jnp.sum(e, axis=-1, keepdims=True)


def _nsa_kernel(q_ref, gd_ref, kc_ref, vc_ref, ks_ref, vs_ref, kw_ref, vw_ref, e_ref, ovt_ref, o_ref,
                *, n_blocks, n_sel, win_len):
    i = pl.program_id(1)
    t0 = i * Q_TILE
    rows = NSA_REP * Q_TILE
    ncp = kc_ref.shape[2]
    qpos = t0 + lax.broadcasted_iota(jnp.int32, (Q_TILE, 1), 0)
    qpos4 = jnp.concatenate([qpos] * NSA_REP, axis=0)
    gd = gd_ref[0]

    blk = lax.broadcasted_iota(jnp.int32, (SEL_PAD, Q_TILE), 0)
    qpos_t = t0 + lax.broadcasted_iota(jnp.int32, (SEL_PAD, Q_TILE), 1)
    cur = qpos_t >> 6
    forced = jnp.where(blk == 0, 1.0, jnp.where(blk == cur, 1.0, jnp.where(blk == cur - 1, 1.0, 0.0)))
    valid = blk * SEL_BLOCK <= qpos_t
    real = blk < n_blocks

    n_full = t0 // KEY_CHUNK
    for g in range(KV_GROUPS):
        qs = jnp.concatenate(
            [q_ref[0, :, (NSA_REP * g + r) * LANES:(NSA_REP * g + r + 1) * LANES] for r in range(NSA_REP)], axis=0)

        s = lax.dot_general(qs, kc_ref[0, 0], NT_DIMS, preferred_element_type=F32)
        cend = lax.broadcasted_iota(jnp.int32, (1, ncp), 1) * CMP_STRIDE + (CMP_BLOCK - 1)
        cmask = cend <= qpos4
        p = jnp.where(cmask, _softmax_rows(s, cmask), 0.0)
        o_cmp = jnp.dot(p.astype(BF16), vc_ref[0, 0], preferred_element_type=F32)

        psum = p[0:Q_TILE]
        for r in range(1, NSA_REP):
            psum = psum + p[r * Q_TILE:(r + 1) * Q_TILE]
        imp_t = None
        for part in _split3(psum):
            term = lax.dot_general(ovt_ref[...], part, NT_DIMS, preferred_element_type=F32)
            imp_t = term if imp_t is None else imp_t + term
        score = jnp.where(valid, imp_t + SEL_BONUS * forced, -1.0)
        score = jnp.where(real, score, -jnp.inf)
        sel_t = _topk_mask_t(score, n_sel)
        bias = jnp.where(sel_t.T > 0.5, 0.0, NEG_INF).astype(BF16)
        q_aug = jnp.concatenate([jnp.concatenate([bias] * NSA_REP, axis=0), qs], axis=1)

        def chunk(c, carry, causal):
            m, l, acc = carry
            k0 = pl.multiple_of(c * KEY_CHUNK, KEY_CHUNK)
            k_aug = jnp.concatenate([e_ref[pl.ds(k0, KEY_CHUNK), :], ks_ref[0, pl.ds(k0, KEY_CHUNK), :]], axis=1)
            s = lax.dot_general(q_aug, k_aug, NT_DIMS, preferred_element_type=F32)
            if causal:
                kpos = k0 + lax.broadcasted_iota(jnp.int32, (1, KEY_CHUNK), 1)
                s = jnp.where(kpos <= qpos4, s, NEG_INF)
            m_new = jnp.maximum(m, jnp.max(s, axis=-1, keepdims=True))
            alpha = jnp.exp(m - m_new)
            pc = jnp.exp(s - m_new)
            l = alpha * l + jnp.sum(pc, axis=-1, keepdims=True)
            acc = alpha * acc + jnp.dot(pc.astype(BF16), vs_ref[0, pl.ds(k0, KEY_CHUNK), :],
                                        preferred_element_type=F32)
            return m_new, l, acc

        init = (jnp.full((rows, 1), NEG_INF, F32), jnp.zeros((rows, 1), F32), jnp.zeros((rows, LANES), F32))
        carry = lax.fori_loop(0, n_full, lambda c, cr: chunk(c, cr, False), init)
        _, l, acc = chunk(n_full, carry, True)
        o_slc = acc / l

        w0 = pl.multiple_of(jnp.maximum(t0 + Q_TILE - win_len, 0), Q_TILE)
        s = lax.dot_general(qs, kw_ref[0, pl.ds(w0, win_len), :], NT_DIMS, preferred_element_type=F32)
        dpos = (w0 + lax.broadcasted_iota(jnp.int32, (1, win_len), 1)) - qpos4
        wmask = (dpos + WINDOW).astype(jnp.uint32) <= WINDOW
        pw = _softmax_rows(s, wmask)
        o_win = jnp.dot(pw.astype(BF16), vw_ref[0, pl.ds(w0, win_len), :], preferred_element_type=F32)

        for r in range(NSA_REP):
            hh = NSA_REP * g + r
            rs = slice(r * Q_TILE, (r + 1) * Q_TILE)
            mix = (gd[:, 3 * hh:3 * hh + 1] * o_cmp[rs] + gd[:, 3 * hh + 1:3 * hh + 2] * o_slc[rs]
                   + gd[:, 3 * hh + 2:3 * hh + 3] * o_win[rs])
            o_ref[0, :, hh * HEAD_DIM:(hh + 1) * HEAD_DIM] = mix[:, g * HEAD_DIM:(g + 1) * HEAD_DIM]


def nsa_prompt(q, gd, cmp, kvb, onehot, ovt):
    b, t, _ = q.shape
    ncp = cmp.shape[2]
    n_blocks = t // SEL_BLOCK
    assert n_blocks <= SEL_PAD and t % KEY_CHUNK == 0
    win_len = min(WINDOW + Q_TILE, t)
    kern = functools.partial(_nsa_kernel, n_blocks=n_blocks, n_sel=min(SEL_TOPN, n_blocks), win_len=win_len)
    slab = lambda j: pl.BlockSpec((1, t, LANES), lambda bi, i, j=j: (bi, 0, j))
    return pl.pallas_call(
        kern,
        grid=(b, t // Q_TILE),
        in_specs=[pl.BlockSpec((1, Q_TILE, NSA_HEADS * LANES), lambda bi, i: (bi, i, 0)),
                  pl.BlockSpec((1, Q_TILE, LANES), lambda bi, i: (bi, i, 0)),
                  pl.BlockSpec((1, 1, ncp, LANES), lambda bi, i: (0, bi, 0, 0)),
                  pl.BlockSpec((1, 1, ncp, LANES), lambda bi, i: (1, bi, 0, 0)),
                  slab(2), slab(3), slab(4), slab(5),
                  _const_spec(onehot.shape), _const_spec(ovt.shape)],
        out_specs=pl.BlockSpec((1, Q_TILE, NSA_HEADS * HEAD_DIM), lambda bi, i: (bi, i, 0)),
        out_shape=jax.ShapeDtypeStruct((b, t, NSA_HEADS * HEAD_DIM), F32),
        compiler_params=_cparams("parallel", "arbitrary"),
    )(q, gd, cmp, cmp, kvb, kvb, kvb, kvb, onehot, ovt)


def _ssd_kernel(xbc_ref, z_ref, gd_ref, cw_ref, cb_ref, a_ref, dsk_ref, ng_ref,
                y_ref, hfin_ref, ext_ref, h_ref, y_scr):
    c = pl.program_id(1)
    L = SSM_CHUNK
    pd = SSM_HEADS * HEAD_DIM
    gw = SSM_STATE

    @pl.when(c == 0)
    def _():
        ext_ref[0:SUBLANES, :] = jnp.zeros((SUBLANES, ext_ref.shape[1]), F32)
        h_ref[...] = jnp.zeros(h_ref.shape, F32)

    ext_ref[SUBLANES:SUBLANES + L, :] = xbc_ref[0]
    conv = cb_ref[...]
    for k in range(SSM_CONV):
        off = SUBLANES - (SSM_CONV - 1) + k
        conv = conv + ext_ref[off:off + L, :] * cw_ref[k:k + 1, :]
    ext_ref[0:SUBLANES, :] = ext_ref[L:L + SUBLANES, :]
    act = _silu(conv)
    xs = act[:, :pd]
    gd = gd_ref[0]

    ii = lax.broadcasted_iota(jnp.int32, (L, L), 0)
    jj = lax.broadcasted_iota(jnp.int32, (L, L), 1)
    causal = ii >= jj
    tri = jnp.where(causal, 1.0, 0.0).astype(BF16)
    acum = None
    for part in _split3(gd * a_ref[...]):
        term = jnp.dot(tri, part, preferred_element_type=F32)
        acum = term if acum is None else acum + term
    acum_t = acum.T

    for g in range(SSM_GROUPS):
        bm = act[:, pd + g * gw:pd + (g + 1) * gw].astype(BF16)
        cm = act[:, pd + SSM_GROUPS * gw + g * gw:pd + SSM_GROUPS * gw + (g + 1) * gw].astype(BF16)
        cb = lax.dot_general(cm, bm, NT_DIMS, preferred_element_type=F32)
        for hl in range(SSM_HEADS // SSM_GROUPS):
            hh = g * (SSM_HEADS // SSM_GROUPS) + hl
            col = acum[:, DT_LANE0 + hh:DT_LANE0 + hh + 1]
            row = acum_t[DT_LANE0 + hh:DT_LANE0 + hh + 1, :]
            last = acum[L - 1:L, DT_LANE0 + hh:DT_LANE0 + hh + 1]
            lmat = jnp.exp(jnp.where(causal, col - row, NEG_INF))
            x_h = xs[:, hh * HEAD_DIM:(hh + 1) * HEAD_DIM]
            xdt = x_h * gd[:, DT_LANE0 + hh:DT_LANE0 + hh + 1]
            y_diag = jnp.dot((cb * lmat).astype(BF16), xdt.astype(BF16), preferred_element_type=F32)
            h_in = h_ref[hh]
            y_off = lax.dot_general(cm, h_in.astype(BF16), NT_DIMS, preferred_element_type=F32) * jnp.exp(col)
            st = lax.dot_general((xdt * jnp.exp(last - col)).astype(BF16), bm, TN_DIMS,
                                 preferred_element_type=F32)
            h_ref[hh] = jnp.exp(last) * h_in + st
            y_scr[:, hh * HEAD_DIM:(hh + 1) * HEAD_DIM] = (
                y_diag + y_off + dsk_ref[:, hh * HEAD_DIM:(hh + 1) * HEAD_DIM] * x_h)

    y = y_scr[...] * _silu(z_ref[0])
    for g in range(SSM_GROUPS):
        cols = slice(g * gw, (g + 1) * gw)
        yg = y[:, cols]
        ms = jnp.mean(yg * yg, axis=-1, keepdims=True)
        y_ref[0, :, cols] = yg * lax.rsqrt(ms + EPS) * ng_ref[:, cols]

    @pl.when(c == pl.num_programs(1) - 1)
    def _():
        hfin_ref[0] = h_ref[...]


def ssd_prompt(xbc, z, gd, lw):
    b, t, cd = xbc.shape
    L = SSM_CHUNK
    pd = SSM_HEADS * HEAD_DIM
    tile = lambda n: pl.BlockSpec((1, L, n), lambda i, j: (i, j, 0))
    return pl.pallas_call(
        _ssd_kernel,
        grid=(b, t // L),
        in_specs=[tile(cd), tile(pd), tile(LANES),
                  _const_spec((SSM_CONV, cd)), _const_spec((1, cd)), _const_spec((1, LANES)),
                  _const_spec((1, pd)), _const_spec((1, pd))],
        out_specs=[tile(pd), pl.BlockSpec((1, SSM_HEADS, HEAD_DIM, SSM_STATE), lambda i, j: (i, 0, 0, 0))],
        out_shape=[jax.ShapeDtypeStruct((b, t, pd), F32),
                   jax.ShapeDtypeStruct((b, SSM_HEADS, HEAD_DIM, SSM_STATE), F32)],
        scratch_shapes=[pltpu.VMEM((SUBLANES + L, cd), F32),
                        pltpu.VMEM((SSM_HEADS, HEAD_DIM, SSM_STATE), F32),
                        pltpu.VMEM((L, pd), F32)],
        compiler_params=_cparams("parallel", "arbitrary"),
    )(xbc, z, gd, lw["conv_w"], lw["conv_b"], lw["a_row"], lw["d_skip"], lw["ssm_norm"])


def _jx_compress(rows, pe, w):
    b, length, g, hd = rows.shape
    ratio = CMP_BLOCK // CMP_STRIDE
    nseg = length // CMP_STRIDE
    nc = nseg - ratio + 1
    seg = rows[:, :nseg * CMP_STRIDE].reshape(b, nseg, CMP_STRIDE, g, hd)
    wseg = w.reshape(ratio, CMP_STRIDE, hd, hd)
    out = jnp.einsum('jd,jde->e', pe, w)
    for r in range(ratio):
        out = out + jnp.einsum('bnjgd,jde->bnge', seg[:, r:r + nc], wseg[r])
    cend = jnp.arange(nc, dtype=jnp.int32) * CMP_STRIDE + CMP_BLOCK - 1
    return out, cend


def _jx_attend(q, qpos, kc, vc, cend, ks, vs, kw, vw, wpos, gates):
    b, t, g, r, hd = q.shape
    scale = hd ** -0.5
    s = jnp.einsum('btgrd,bcgd->btgrc', q, kc, preferred_element_type=F32) * scale
    cmask = (cend[None, :] <= qpos[:, None])[None, :, None, None, :]
    p_cmp = jax.nn.softmax(jnp.where(cmask, s, NEG_INF), axis=-1) * cmask
    o_cmp = jnp.einsum('btgrc,bcgd->btgrd', p_cmp.astype(vc.dtype), vc)
    nc = kc.shape[1]
    ns = ks.shape[1] // SEL_BLOCK
    c_start = jnp.arange(nc, dtype=jnp.int32) * CMP_STRIDE
    s_start = jnp.arange(ns, dtype=jnp.int32) * SEL_BLOCK
    overlap = ((c_start[:, None] < s_start[None, :] + SEL_BLOCK)
               & (c_start[:, None] + CMP_BLOCK > s_start[None, :])).astype(F32)
    imp = jnp.einsum('btgc,cs->btgs', p_cmp.sum(axis=3), overlap, precision=lax.Precision.HIGHEST)
    cur = (qpos // SEL_BLOCK)[:, None]
    blk = jnp.arange(ns, dtype=jnp.int32)[None, :]
    valid = s_start[None, :] <= qpos[:, None]
    forced = ((blk == 0) | (blk == cur) | (blk == cur - 1)).astype(F32)
    score = jnp.where(valid[None, :, None, :], imp + SEL_BONUS * forced[None, :, None, :], -1.0)
    _, idx = lax.top_k(score, min(SEL_TOPN, ns))
    bi = jnp.arange(b)[:, None, None, None]
    gi = jnp.arange(g)[None, None, :, None]
    kg = ks.reshape(b, ns, SEL_BLOCK, g, hd)[bi, idx, :, gi, :]
    vg = vs.reshape(b, ns, SEL_BLOCK, g, hd)[bi, idx, :, gi, :]
    kpos = idx[..., None] * SEL_BLOCK + jnp.arange(SEL_BLOCK, dtype=jnp.int32)
    smask = (kpos <= qpos[None, :, None, None, None])[:, :, :, None]
    s = jnp.einsum('btgrd,btgnkd->btgrnk', q, kg, preferred_element_type=F32) * scale
    p = jax.nn.softmax(jnp.where(smask, s, NEG_INF).reshape(b, t, g, r, -1), axis=-1).reshape(s.shape)
    o_slc = jnp.einsum('btgrnk,btgnkd->btgrd', p.astype(vg.dtype), vg)
    s = jnp.einsum('btgrd,blgd->btgrl', q, kw, preferred_element_type=F32) * scale
    wmask = ((wpos[None, :] <= qpos[:, None]) & (wpos[None, :] >= qpos[:, None] - WINDOW)
             & (wpos[None, :] >= 0))[None, :, None, None, :]
    p = jax.nn.softmax(jnp.where(wmask, s, NEG_INF), axis=-1)
    o_win = jnp.einsum('btgrl,blgd->btgrd', p.astype(vw.dtype), vw)
    return gates[..., 0:1] * o_cmp + gates[..., 1:2] * o_slc + gates[..., 2:3] * o_win


def _sample_mixers(u, qp, kv, gd, z, xbc, nsa_past, win_past, pool_buf, conv_buf, h0, raw, past_len):
    b = u.shape[0]
    pos = past_len + jnp.arange(1, dtype=jnp.int32)
    ext = jnp.concatenate([pool_buf, u[:, None, :]], axis=1)
    outs = []
    for gi, win in enumerate(POOL_WINDOWS):
        sl = slice(gi * HEAD_DIM, (gi + 1) * HEAD_DIM)
        outs.append(jnp.sum(ext[:, POOL_BUF + 1 - win:, sl], axis=1) / float(win) - ext[:, POOL_BUF, sl])
    d = jnp.stack(outs, axis=1)
    y_pool = jnp.einsum('bgc,gce->bge', d, raw['pool_w']).reshape(b, -1) * raw['pool_scale']
    new_pool = ext[:, 1:]
    q = qp.astype(F32).reshape(b, NSA_HEADS, KV_GROUPS, HEAD_DIM)
    q = jnp.stack([q[:, hh, hh // NSA_REP] for hh in range(NSA_HEADS)], axis=1) * float(HEAD_DIM ** 0.5)
    q = q.reshape(b, 1, KV_GROUPS, NSA_REP, HEAD_DIM)
    kv6 = kv.reshape(b, 1, 6, KV_GROUPS, HEAD_DIM)
    rows = jnp.stack([kv6[:, :, 0], kv6[:, :, 1], kv6[:, :, 2], kv6[:, :, 3]], axis=1)
    win_new = jnp.stack([kv6[:, :, 4], kv6[:, :, 5]], axis=1)
    gates = gd[:, :GATE_LANES].reshape(b, 1, KV_GROUPS, NSA_REP, 3)
    full = jnp.concatenate([nsa_past, rows], axis=2)
    kc, cend = _jx_compress(full[:, 0], raw['nsa_cmp_pe'][0], raw['nsa_cmp_w'][0])
    vc, _ = _jx_compress(full[:, 1], raw['nsa_cmp_pe'][1], raw['nsa_cmp_w'][1])
    pad = (-full.shape[2]) % SEL_BLOCK
    sel = jnp.pad(full[:, 2:4], ((0, 0), (0, 0), (0, pad), (0, 0), (0, 0)))
    wfull = jnp.concatenate([win_past, win_new], axis=2)
    wbuf = win_past.shape[2]
    wpos = past_len - wbuf + jnp.arange(wbuf + 1, dtype=jnp.int32)
    y_nsa = _jx_attend(q, pos, kc, vc, cend, sel[:, 0], sel[:, 1], wfull[:, 0], wfull[:, 1], wpos, gates)
    y_nsa = y_nsa.reshape(b, NSA_HEADS * HEAD_DIM)
    new_win = wfull[:, :, -wbuf:]
    extc = jnp.concatenate([conv_buf, xbc[:, None, :]], axis=1)
    act = _silu(jnp.sum(extc * raw['ssm_conv_w'][None], axis=1) + raw['ssm_conv_b'])
    new_conv = extc[:, 1:]
    pd = SSM_HEADS * HEAD_DIM
    xs = act[:, :pd].reshape(b, SSM_HEADS, HEAD_DIM)
    bm = jnp.repeat(act[:, pd:pd + SSM_GROUPS * SSM_STATE].reshape(b, SSM_GROUPS, SSM_STATE),
                    SSM_HEADS // SSM_GROUPS, axis=1)
    cm = jnp.repeat(act[:, pd + SSM_GROUPS * SSM_STATE:].reshape(b, SSM_GROUPS, SSM_STATE),
                    SSM_HEADS // SSM_GROUPS, axis=1)
    dt = gd[:, DT_LANE0:DT_LANE0 + SSM_HEADS]
    a = -jnp.exp(raw['ssm_a_log'])
    dec = jnp.exp(dt * a)
    xdt = xs * dt[..., None]
    h_new = dec[:, :, None, None] * h0 + xdt[..., None] * bm[:, :, None, :]
    y = jnp.sum(h_new * cm[:, :, None, :], axis=-1) + raw['ssm_d'][None, :, None] * xs
    y = (y.reshape(b, pd) * _silu(z)).reshape(b, SSM_GROUPS, pd // SSM_GROUPS)
    y = y * lax.rsqrt(jnp.mean(y * y, axis=-1, keepdims=True) + EPS)
    y_ssm = y.reshape(b, pd) * raw['ssm_norm']
    return y_pool, y_nsa, y_ssm, rows, new_win, new_pool, new_conv, h_new


def _prep_layer(p, i):
    d = p['w_in'].shape[1]
    w_in = p['w_in'][i]
    o = np.cumsum([0, 256, 512, 768, GATE_LANES, 256, 768, SSM_HEADS])
    w_q = w_in[:, o[1]:o[2]].reshape(d, NSA_HEADS, HEAD_DIM)
    slot = jnp.zeros((d, NSA_HEADS, KV_GROUPS, HEAD_DIM), F32)
    for hh in range(NSA_HEADS):
        slot = slot.at[:, hh, hh // NSA_REP].set(w_q[:, hh])
    w_gd = jnp.zeros((d, LANES), F32)
    w_gd = w_gd.at[:, :GATE_LANES].set(w_in[:, o[3]:o[4]])
    w_gd = w_gd.at[:, DT_LANE0:DT_LANE0 + SSM_HEADS].set(w_in[:, o[6]:o[7]])
    lane_pad = lambda v: jnp.zeros((1, LANES), F32).at[0, DT_LANE0:DT_LANE0 + SSM_HEADS].set(v)
    eye_g = jnp.eye(KV_GROUPS, dtype=F32)
    ratio = CMP_BLOCK // CMP_STRIDE
    cmp_w, cmp_pe = [], []
    for j in range(2):
        w4 = p['nsa_cmp_w'][i, j].reshape(ratio, CMP_STRIDE, HEAD_DIM, HEAD_DIM)
        cmp_w.append(jnp.einsum('rjde,gh->jgdrhe', w4, eye_g).reshape(CMP_STRIDE * 2 * HEAD_DIM, ratio * LANES))
        pe4 = p['nsa_cmp_pe'][i, j].reshape(ratio, CMP_STRIDE, 1, HEAD_DIM)
        pe_rows = jnp.broadcast_to(pe4, (ratio, CMP_STRIDE, KV_GROUPS, HEAD_DIM)).reshape(ratio, -1)
        cmp_pe.append(jnp.zeros((SUBLANES, pe_rows.shape[1]), F32).at[:ratio].set(pe_rows))
    pool_w = jnp.zeros((256, 256), F32)
    for gi in range(len(POOL_WINDOWS)):
        sl = slice(gi * HEAD_DIM, (gi + 1) * HEAD_DIM)
        pool_w = pool_w.at[sl, sl].set(p['pool_w'][i, gi])
    w_out = p['w_out'][i]
    bf = lambda x: x.astype(BF16)
    row = lambda x: x.reshape(1, -1)
    return {
        'ffn1_norm': row(p['ffn1_norm'][i]), 'ffn1_w_gate': bf(p['ffn1_w_gate'][i]),
        'ffn1_w_up': bf(p['ffn1_w_up'][i]), 'ffn1_w_down': bf(p['ffn1_w_down'][i]),
        'ffn2_norm': row(p['ffn2_norm'][i]), 'ffn2_w_gate': bf(p['ffn2_w_gate'][i]),
        'ffn2_w_up': bf(p['ffn2_w_up'][i]), 'ffn2_w_down': bf(p['ffn2_w_down'][i]),
        'mix_norm': row(p['mix_norm'][i]),
        'w_u': bf(w_in[:, o[0]:o[1]]), 'w_q': bf(slot.reshape(d, NSA_HEADS * LANES)),
        'w_kv': bf(w_in[:, o[2]:o[3]]), 'w_gd': bf(w_gd), 'w_z': bf(w_in[:, o[4]:o[5]]),
        'w_xbc': bf(w_in[:, o[5]:o[6]]),
        'q_norm': row(jnp.tile(p['nsa_q_norm'][i], 2)), 'k_norm': jnp.tile(p['nsa_k_norm'][i], (1, 2)),
        'dt_bias': lane_pad(p['ssm_dt_bias'][i]),
        'w_out_pool': bf(w_out[:256]), 'w_out_nsa': bf(w_out[256:768]), 'w_out_ssm': bf(w_out[768:]),
        'pool_w': bf(pool_w), 'pool_scale': row(p['pool_scale'][i]),
        'cmp_w': bf(jnp.stack(cmp_w)), 'cmp_pe': bf(jnp.stack(cmp_pe)),
        'conv_w': p['ssm_conv_w'][i], 'conv_b': row(p['ssm_conv_b'][i]),
        'a_row': lane_pad(-jnp.exp(p['ssm_a_log'][i])),
        'd_skip': row(jnp.repeat(p['ssm_d'][i], HEAD_DIM)), 'ssm_norm': row(p['ssm_norm'][i]),
        'ple_norm': row(p['ple_norm'][i]), 'ple_w_gate': bf(p['ple_w_gate'][i]),
        'ple_w_proj': bf(p['ple_w_proj'][i]),
    }


def _rope_tables(pos):
    half = HEAD_DIM // 2
    inv = ROPE_THETA ** (-jnp.arange(half, dtype=F32) / half)
    ang = pos.astype(F32)[:, None] * inv[None, :]
    cos = jnp.cos(ang)
    sin = jnp.sin(ang)
    return jnp.tile(cos, (1, 4)), jnp.tile(jnp.concatenate([-sin, sin], axis=1), (1, 2))


def _selection_constants(t):
    nseg = t // CMP_STRIDE
    nc = nseg - CMP_BLOCK // CMP_STRIDE + 1
    ns = t // SEL_BLOCK
    c_start = np.arange(nseg) * CMP_STRIDE
    s_start = np.arange(SEL_PAD) * SEL_BLOCK
    ovt = ((c_start[None, :] < s_start[:, None] + SEL_BLOCK) & (c_start[None, :] + CMP_BLOCK > s_start[:, None])
           & (np.arange(nseg)[None, :] < nc) & (np.arange(SEL_PAD)[:, None] < ns))
    onehot = (np.arange(t)[:, None] // SEL_BLOCK) == np.arange(SEL_PAD)[None, :]
    return jnp.asarray(onehot, BF16), jnp.asarray(ovt, BF16)


def _token_layer_front(h, lw, cos, sin):
    h = ffn_halfstep(h, lw['ffn1_norm'], lw['ffn1_w_gate'], lw['ffn1_w_up'], lw['ffn1_w_down'])
    return (h,) + tuple(in_projection(h, lw, cos, sin))


def _token_layer_back(h, y_pool, y_nsa, y_ssm, pe, lw):
    h = out_projection(h, y_pool, y_nsa, y_ssm, lw)
    h = ffn_halfstep(h, lw['ffn2_norm'], lw['ffn2_w_gate'], lw['ffn2_w_up'], lw['ffn2_w_down'])
    return ple_step(h, pe, lw)


def kernel(x_prompt, x_sample, cache_nsa_kv, cache_win_kv, state_pool, state_conv, state_ssm, page_table,
           p_prompt, p_sample, ffn1_norm, ffn1_w_gate, ffn1_w_up, ffn1_w_down, mix_norm, w_in, w_out,
           pool_w, pool_scale, nsa_q_norm, nsa_k_norm, nsa_cmp_pe, nsa_cmp_w, ssm_conv_w, ssm_conv_b,
           ssm_dt_bias, ssm_a_log, ssm_d, ssm_norm, ffn2_norm, ffn2_w_gate, ffn2_w_up, ffn2_w_down,
           ple_norm, ple_w_gate, ple_w_proj):
    params = dict(ffn1_norm=ffn1_norm, ffn1_w_gate=ffn1_w_gate, ffn1_w_up=ffn1_w_up, ffn1_w_down=ffn1_w_down,
                  mix_norm=mix_norm, w_in=w_in, w_out=w_out, pool_w=pool_w, pool_scale=pool_scale,
                  nsa_q_norm=nsa_q_norm, nsa_k_norm=nsa_k_norm, nsa_cmp_pe=nsa_cmp_pe, nsa_cmp_w=nsa_cmp_w,
                  ssm_conv_w=ssm_conv_w, ssm_conv_b=ssm_conv_b, ssm_dt_bias=ssm_dt_bias, ssm_a_log=ssm_a_log,
                  ssm_d=ssm_d, ssm_norm=ssm_norm, ffn2_norm=ffn2_norm, ffn2_w_gate=ffn2_w_gate,
                  ffn2_w_up=ffn2_w_up, ffn2_w_down=ffn2_w_down, ple_norm=ple_norm, ple_w_gate=ple_w_gate,
                  ple_w_proj=ple_w_proj)
    depth = w_in.shape[0]
    b, t, d = x_prompt.shape
    db = x_sample.shape[0]
    past_len = page_table.shape[1] * PAGE_SIZE
    wkeep = min(WINDOW, t)

    cos_p, sin_p = _rope_tables(jnp.tile(jnp.arange(t, dtype=jnp.int32), b))
    cos_s, sin_s = _rope_tables(jnp.full((db,), past_len, jnp.int32))
    onehot, ovt = _selection_constants(t)

    h_p = x_prompt.reshape(b * t, d)
    h_s = x_sample.reshape(db, d)
    st_p = [[] for _ in range(5)]
    st_s = [[] for _ in range(5)]
    for i in range(depth):
        lw = _prep_layer(params, i)
        h_p, u, qp, kv, kvb, gd, z, xbc = _token_layer_front(h_p, lw, cos_p, sin_p)
        r3 = lambda x: x.reshape(b, t, x.shape[-1])
        y_pool = pool_prompt(r3(u), lw)
        seg = jnp.stack([kvb[:, 0:LANES], kvb[:, LANES:2 * LANES]]).reshape(2, b, t // CMP_STRIDE,
                                                                           CMP_STRIDE * LANES)
        cmp = nsa_compress(seg, lw['cmp_w'], lw['cmp_pe'])
        y_nsa = nsa_prompt(r3(qp), r3(gd), cmp, r3(kvb), onehot, ovt)
        y_ssm, h_fin = ssd_prompt(r3(xbc), r3(z), r3(gd), lw)
        h_p = _token_layer_back(h_p, y_pool.reshape(b * t, -1), y_nsa.reshape(b * t, -1),
                                y_ssm.reshape(b * t, -1), p_prompt[i].reshape(b * t, -1), lw)
        kv6 = kv.reshape(b, t, 6, KV_GROUPS, HEAD_DIM)
        st_p[0].append(jnp.moveaxis(kv6[:, :, 0:4], 2, 1))
        st_p[1].append(jnp.moveaxis(kv6[:, t - wkeep:, 4:6], 2, 1))
        st_p[2].append(r3(u)[:, t - POOL_BUF:])
        st_p[3].append(r3(xbc)[:, t - (SSM_CONV - 1):])
        st_p[4].append(h_fin)
        h_s, u, qp, kv, kvb, gd, z, xbc = _token_layer_front(h_s, lw, cos_s, sin_s)
        nsa_past = cache_nsa_kv[page_table, i]
        nsa_past = jnp.moveaxis(nsa_past, 2, 1).reshape(db, 4, past_len, KV_GROUPS, HEAD_DIM)
        raw = {k: params[k][i] for k in ('pool_w', 'pool_scale', 'nsa_cmp_pe', 'nsa_cmp_w', 'ssm_conv_w',
                                         'ssm_conv_b', 'ssm_a_log', 'ssm_d', 'ssm_norm')}
        y_pool, y_nsa, y_ssm, rows, new_win, new_pool, new_conv, new_h = _sample_mixers(
            u, qp, kv, gd, z, xbc, nsa_past, cache_win_kv[:, i], state_pool[:, i], state_conv[:, i],
            state_ssm[:, i], raw, past_len)
        h_s = _token_layer_back(h_s, y_pool, y_nsa, y_ssm, p_sample[i].reshape(db, -1), lw)
        for j, v in enumerate((rows, new_win, new_pool, new_conv, new_h)):
            st_s[j].append(v)
    outs = [h_p.reshape(b, t, d), h_s.reshape(db, 1, d)]
    for j in range(5):
        outs.append(jnp.stack(st_p[j], axis=1))
        outs.append(jnp.stack(st_s[j], axis=1))
    return tuple(outs)
```

```python
import functools

import jax
import jax.numpy as jnp
import numpy as np
from jax import lax
from jax.experimental import pallas as pl
from jax.experimental.pallas import tpu as pltpu

F32 = jnp.float32
BF16 = jnp.bfloat16

POOL_WINDOWS = (2, 4, 8, 16)
POOL_BUF = 15
HEAD_DIM = 64
NSA_HEADS = 8
KV_GROUPS = 2
NSA_REP = NSA_HEADS // KV_GROUPS
CMP_BLOCK = 32
CMP_STRIDE = 16
SEL_BLOCK = 64
SEL_TOPN = 16
WINDOW = 512
SEL_BONUS = 1.0e4
NEG_INF = -1.0e30
SSM_HEADS = 4
SSM_GROUPS = 2
SSM_STATE = 128
SSM_CONV = 4
SSM_CHUNK = 128
ROPE_THETA = 10000.0
EPS = 1e-6
PAGE_SIZE = 128

LANES = 128
SUBLANES = 8
VMEM_LIMIT_BYTES = 56 * 1024 * 1024

GATE_LANES = 3 * NSA_HEADS
DT_LANE0 = 32

NT_DIMS = (((1,), (1,)), ((), ()))
TN_DIMS = (((0,), (0,)), ((), ()))


def _cparams(*sem):
    return pltpu.CompilerParams(dimension_semantics=sem, vmem_limit_bytes=VMEM_LIMIT_BYTES)


def _const_spec(shape):
    nd = len(shape)
    return pl.BlockSpec(shape, lambda *_: (0,) * nd)


def _rmsnorm(x, g):
    ms = jnp.mean(x * x, axis=-1, keepdims=True)
    return x * lax.rsqrt(ms + EPS) * g


def _silu(x):
    return x * jax.nn.sigmoid(x)


def _split3(x):
    hi = x.astype(BF16)
    r = x - hi.astype(F32)
    mid = r.astype(BF16)
    lo = (r - mid.astype(F32)).astype(BF16)
    return hi, mid, lo


def _ffn_kernel(x_ref, g_ref, wg_ref, wu_ref, wd_ref, o_ref, *, n_chunks):
    x = x_ref[...]
    xn = _rmsnorm(x, g_ref[...]).astype(BF16)
    fc = wg_ref.shape[1] // n_chunks
    tot = None
    for c in range(n_chunks):
        sl = slice(c * fc, (c + 1) * fc)
        g = jnp.dot(xn, wg_ref[:, sl], preferred_element_type=F32)
        u = jnp.dot(xn, wu_ref[:, sl], preferred_element_type=F32)
        a = (_silu(g) * u).astype(BF16)
        d = jnp.dot(a, wd_ref[sl, :], preferred_element_type=F32)
        tot = d if tot is None else tot + d
    o_ref[...] = x + 0.5 * tot


def _row_tile(m, pref):
    return pref if m % pref == 0 else m


def ffn_halfstep(h, g, wg, wu, wd):
    m, d = h.shape
    f = wg.shape[1]
    tm = _row_tile(m, 512)
    return pl.pallas_call(
        functools.partial(_ffn_kernel, n_chunks=2),
        grid=(m // tm,),
        in_specs=[pl.BlockSpec((tm, d), lambda i: (i, 0)),
                  _const_spec((1, d)), _const_spec((d, f)), _const_spec((d, f)), _const_spec((f, d))],
        out_specs=pl.BlockSpec((tm, d), lambda i: (i, 0)),
        out_shape=jax.ShapeDtypeStruct((m, d), F32),
        compiler_params=_cparams("parallel"),
    )(h, g, wg, wu, wd)


def _rope128(x, cos, sin_signed, lane):
    rot = jnp.where((lane & 32) == 0, pltpu.roll(x, 96, 1), pltpu.roll(x, 32, 1))
    return x * cos + rot * sin_signed


def _inproj_kernel(x_ref, g_ref, wu_ref, wq_ref, wkv_ref, wgd_ref, wz_ref, wx_ref,
                   qn_ref, kn_ref, dtb_ref, cos_ref, sin_ref,
                   u_ref, q_ref, kv_ref, kvb_ref, gd_ref, z_ref, xbc_ref):
    x = x_ref[...]
    tm = x.shape[0]
    xn = _rmsnorm(x, g_ref[...]).astype(BF16)
    cos = cos_ref[...]
    sin = sin_ref[...]
    lane = lax.broadcasted_iota(jnp.int32, (tm, LANES), 1)
    inv_hd = 1.0 / HEAD_DIM

    u_ref[...] = jnp.dot(xn, wu_ref[...], preferred_element_type=F32)
    z_ref[...] = jnp.dot(xn, wz_ref[...], preferred_element_type=F32)
    xbc_ref[...] = jnp.dot(xn, wx_ref[...], preferred_element_type=F32)

    y = jnp.dot(xn, wgd_ref[...], preferred_element_type=F32)
    yd = y + dtb_ref[...]
    softplus = jnp.maximum(yd, 0.0) + jnp.log1p(jnp.exp(-jnp.abs(yd)))
    gd_ref[...] = jnp.where(lane < DT_LANE0, jax.nn.sigmoid(y), softplus)

    q = jnp.dot(xn, wq_ref[...], preferred_element_type=F32)
    qn = qn_ref[...]
    for hh in range(NSA_HEADS):
        s = q[:, hh * LANES:(hh + 1) * LANES]
        ms = jnp.sum(s * s, axis=-1, keepdims=True) * inv_hd
        s = s * lax.rsqrt(ms + EPS) * qn
        s = _rope128(s, cos, sin, lane) * (HEAD_DIM ** -0.5)
        q_ref[:, hh * LANES:(hh + 1) * LANES] = s.astype(BF16)

    kv = jnp.dot(xn, wkv_ref[...], preferred_element_type=F32)
    low = lane < HEAD_DIM
    for j in range(6):
        s = kv[:, j * LANES:(j + 1) * LANES]
        if j % 2 == 0:
            sq = s * s
            s_all = jnp.sum(sq, axis=-1, keepdims=True)
            s_low = jnp.sum(jnp.where(low, sq, 0.0), axis=-1, keepdims=True)
            ms = jnp.where(low, s_low, s_all - s_low) * inv_hd
            s = s * lax.rsqrt(ms + EPS) * kn_ref[j // 2:j // 2 + 1, :]
            s = _rope128(s, cos, sin, lane)
        kv_ref[:, j * LANES:(j + 1) * LANES] = s
        kvb_ref[:, j * LANES:(j + 1) * LANES] = s.astype(BF16)


def in_projection(h, lw, cos, sin):
    m, d = h.shape
    tm = _row_tile(m, 512)
    row = lambda n: pl.BlockSpec((tm, n), lambda i: (i, 0))
    ws = [lw["w_u"], lw["w_q"], lw["w_kv"], lw["w_gd"], lw["w_z"], lw["w_xbc"]]
    outs = [(256, F32), (NSA_HEADS * LANES, BF16), (6 * LANES, F32), (6 * LANES, BF16),
            (LANES, F32), (256, F32), (768, F32)]
    return pl.pallas_call(
        _inproj_kernel,
        grid=(m // tm,),
        in_specs=[row(d), _const_spec((1, d))] + [_const_spec(w.shape) for w in ws]
                 + [_const_spec((1, LANES)), _const_spec((3, LANES)), _const_spec((1, LANES)),
                    row(LANES), row(LANES)],
        out_specs=[row(n) for n, _ in outs],
        out_shape=[jax.ShapeDtypeStruct((m, n), dt) for n, dt in outs],
        compiler_params=_cparams("parallel"),
    )(h, lw["mix_norm"], *ws, lw["q_norm"], lw["k_norm"], lw["dt_bias"], cos, sin)


def _outproj_kernel(h_ref, yp_ref, yn_ref, ys_ref, wp_ref, wn_ref, ws_ref, o_ref):
    acc = jnp.dot(yp_ref[...].astype(BF16), wp_ref[...], preferred_element_type=F32)
    acc = acc + jnp.dot(yn_ref[...].astype(BF16), wn_ref[...], preferred_element_type=F32)
    acc = acc + jnp.dot(ys_ref[...].astype(BF16), ws_ref[...], preferred_element_type=F32)
    o_ref[...] = h_ref[...] + acc


def out_projection(h, y_pool, y_nsa, y_ssm, lw):
    m, d = h.shape
    tm = _row_tile(m, 512)
    row = lambda n: pl.BlockSpec((tm, n), lambda i: (i, 0))
    ws = [lw["w_out_pool"], lw["w_out_nsa"], lw["w_out_ssm"]]
    return pl.pallas_call(
        _outproj_kernel,
        grid=(m // tm,),
        in_specs=[row(d), row(y_pool.shape[1]), row(y_nsa.shape[1]), row(y_ssm.shape[1])]
                 + [_const_spec(w.shape) for w in ws],
        out_specs=row(d),
        out_shape=jax.ShapeDtypeStruct((m, d), F32),
        compiler_params=_cparams("parallel"),
    )(h, y_pool, y_nsa, y_ssm, *ws)


def _ple_kernel(h_ref, pe_ref, g_ref, wg_ref, wp_ref, o_ref):
    h = h_ref[...]
    xn = _rmsnorm(h, g_ref[...]).astype(BF16)
    gate = jax.nn.sigmoid(jnp.dot(xn, wg_ref[...], preferred_element_type=F32))
    proj = jnp.dot(pe_ref[...].astype(BF16), wp_ref[...], preferred_element_type=F32)
    o_ref[...] = h + gate * proj


def ple_step(h, pe, lw):
    m, d = h.shape
    tm = _row_tile(m, 512)
    row = lambda n: pl.BlockSpec((tm, n), lambda i: (i, 0))
    return pl.pallas_call(
        _ple_kernel,
        grid=(m // tm,),
        in_specs=[row(d), row(pe.shape[1]), _const_spec((1, d)),
                  _const_spec(lw["ple_w_gate"].shape), _const_spec(lw["ple_w_proj"].shape)],
        out_specs=row(d),
        out_shape=jax.ShapeDtypeStruct((m, d), F32),
        compiler_params=_cparams("parallel"),
    )(h, pe, lw["ple_norm"], lw["ple_w_gate"], lw["ple_w_proj"])


POOL_HALO = 2 * SUBLANES


def _pool_kernel(u_ref, w_ref, sc_ref, y_ref, ext_ref, *, tm):
    c = pl.program_id(1)

    @pl.when(c == 0)
    def _():
        ext_ref[0:POOL_HALO, :] = jnp.zeros((POOL_HALO, ext_ref.shape[1]), F32)

    ext_ref[POOL_HALO:POOL_HALO + tm, :] = u_ref[0]
    lane = lax.broadcasted_iota(jnp.int32, (tm, LANES), 1)
    low = lane < HEAD_DIM
    pos1 = c * tm + lax.broadcasted_iota(jnp.int32, (tm, LANES), 0) + 1
    ds = []
    for slab, (w_lo, w_hi) in enumerate(((POOL_WINDOWS[0], POOL_WINDOWS[1]), (POOL_WINDOWS[2], POOL_WINDOWS[3]))):
        cols = slice(slab * LANES, (slab + 1) * LANES)
        x = ext_ref[POOL_HALO:POOL_HALO + tm, cols]
        run = x
        s_lo = None
        for k in range(1, w_hi):
            run = run + ext_ref[POOL_HALO - k:POOL_HALO - k + tm, cols]
            if k == w_lo - 1:
                s_lo = run
        cnt = jnp.where(low, jnp.minimum(pos1, w_lo), jnp.minimum(pos1, w_hi)).astype(F32)
        ds.append(jnp.where(low, s_lo, run) / cnt - x)
    d = jnp.concatenate(ds, axis=1).astype(BF16)
    y_ref[0] = jnp.dot(d, w_ref[...], preferred_element_type=F32) * sc_ref[...]
    ext_ref[0:POOL_HALO, :] = ext_ref[tm:tm + POOL_HALO, :]


def pool_prompt(u, lw):
    b, t, ch = u.shape
    tm = _row_tile(t, 512)
    return pl.pallas_call(
        functools.partial(_pool_kernel, tm=tm),
        grid=(b, t // tm),
        in_specs=[pl.BlockSpec((1, tm, ch), lambda i, j: (i, j, 0)),
                  _const_spec((ch, ch)), _const_spec((1, ch))],
        out_specs=pl.BlockSpec((1, tm, ch), lambda i, j: (i, j, 0)),
        out_shape=jax.ShapeDtypeStruct((b, t, ch), F32),
        scratch_shapes=[pltpu.VMEM((POOL_HALO + tm, ch), F32)],
        compiler_params=_cparams("parallel", "arbitrary"),
    )(u, lw["pool_w"], lw["pool_scale"])


def _compress_kernel(seg_ref, w_ref, pe_ref, o_ref):
    y = jnp.dot(seg_ref[0], w_ref[0], preferred_element_type=F32)
    pe = jnp.dot(pe_ref[0], w_ref[0], preferred_element_type=F32)
    nseg = y.shape[0]
    second = pltpu.roll(y[:, LANES:], nseg - 1, 0)
    row = lax.broadcasted_iota(jnp.int32, (nseg, LANES), 0)
    second = jnp.where(row < nseg - 1, second, 0.0)
    out = y[:, :LANES] + second + pe[0:1, :LANES] + pe[1:2, LANES:]
    o_ref[0, 0] = out.astype(o_ref.dtype)


def nsa_compress(seg, w, pe):
    _, b, nseg, width = seg.shape
    return pl.pallas_call(
        _compress_kernel,
        grid=(2, b),
        in_specs=[pl.BlockSpec((None, 1, nseg, width), lambda j, i: (j, i, 0, 0)),
                  pl.BlockSpec((1, width, 2 * LANES), lambda j, i: (j, 0, 0)),
                  pl.BlockSpec((1, SUBLANES, width), lambda j, i: (j, 0, 0))],
        out_specs=pl.BlockSpec((1, 1, nseg, LANES), lambda j, i: (j, i, 0, 0)),
        out_shape=jax.ShapeDtypeStruct((2, b, nseg, LANES), BF16),
        compiler_params=_cparams("parallel", "parallel"),
    )(seg, w, pe)


Q_TILE = 128
KEY_CHUNK = 512
SEL_PAD = 128


def _topk_mask_t(score, n_sel):
    blk = lax.broadcasted_iota(jnp.int32, score.shape, 0)

    def body(_, carry):
        sc, sel = carry
        m = jnp.max(sc, axis=0, keepdims=True)
        idx = jnp.min(jnp.where(sc == m, blk, score.shape[0]), axis=0, keepdims=True)
        hit = blk == idx
        return jnp.where(hit, -jnp.inf, sc), jnp.where(hit, 1.0, sel)

    _, sel = lax.fori_loop(0, n_sel, body, (score, jnp.zeros(score.shape, F32)))
    return sel


def _softmax_rows(s, mask):
    sm = jnp.where(mask, s, NEG_INF)
    m = jnp.max(sm, axis=-1, keepdims=True)
    e = jnp.exp(sm - m)
    return e / jnp.sum(e, axis=-1, keepdims=True)


def _nsa_kernel(q_ref, gd_ref, kc_ref, vc_ref, ks_ref, vs_ref, kw_ref, vw_ref, e_ref, ovt_ref, o_ref,
                *, n_blocks, n_sel, win_len):
    i = pl.program_id(1)
    t0 = i * Q_TILE
    ncp = kc_ref.shape[2]
    qpos = t0 + lax.broadcasted_iota(jnp.int32, (Q_TILE, 1), 0)
    gd = gd_ref[0]

    blk = lax.broadcasted_iota(jnp.int32, (SEL_PAD, Q_TILE), 0)
    qpos_t = t0 + lax.broadcasted_iota(jnp.int32, (SEL_PAD, Q_TILE), 1)
    cur = qpos_t >> 6
    forced = jnp.where(blk == 0, 1.0, jnp.where(blk == cur, 1.0, jnp.where(blk == cur - 1, 1.0, 0.0)))
    valid = blk * SEL_BLOCK <= qpos_t
    real = blk < n_blocks

    n_full = t0 // KEY_CHUNK
    lane_row = lax.broadcasted_iota(jnp.int32, (1, LANES), 1)
    cend = lax.broadcasted_iota(jnp.int32, (1, ncp), 1) * CMP_STRIDE + (CMP_BLOCK - 1)
    for g in range(KV_GROUPS):
        own = (lane_row >= g * HEAD_DIM) & (lane_row < (g + 1) * HEAD_DIM)
        den_lane = (1 - g) * HEAD_DIM
        q_heads = [q_ref[0, :, (NSA_REP * g + r) * LANES:(NSA_REP * g + r + 1) * LANES] for r in range(NSA_REP)]

        rows = NSA_REP * Q_TILE
        q_all = jnp.concatenate(q_heads, axis=0)
        qpos4 = jnp.concatenate([qpos] * NSA_REP, axis=0)

        s = lax.dot_general(q_all, kc_ref[0, 0], NT_DIMS, preferred_element_type=F32)
        cmask = cend <= qpos4
        p = jnp.where(cmask, _softmax_rows(s, cmask), 0.0)
        acc = jnp.dot(p.astype(BF16), vc_ref[0, 0], preferred_element_type=F32)
        o_cmp = [acc[r * Q_TILE:(r + 1) * Q_TILE] for r in range(NSA_REP)]
        psum = p[0:Q_TILE]
        for r in range(1, NSA_REP):
            psum = psum + p[r * Q_TILE:(r + 1) * Q_TILE]

        imp_t = None
        for part in _split3(psum):
            term = lax.dot_general(ovt_ref[...], part, NT_DIMS, preferred_element_type=F32)
            imp_t = term if imp_t is None else imp_t + term
        score = jnp.where(valid, imp_t + SEL_BONUS * forced, -1.0)
        score = jnp.where(real, score, -jnp.inf)
        sel_t = _topk_mask_t(score, n_sel)
        bias = jnp.where(sel_t.T > 0.5, 0.0, NEG_INF).astype(BF16)
        q_aug = jnp.concatenate([jnp.concatenate([bias] * NSA_REP, axis=0), q_all], axis=1)

        def scores(c):
            k0 = pl.multiple_of(c * KEY_CHUNK, KEY_CHUNK)
            k_aug = jnp.concatenate([e_ref[pl.ds(k0, KEY_CHUNK), :], ks_ref[0, pl.ds(k0, KEY_CHUNK), :]], axis=1)
            return lax.dot_general(q_aug, k_aug, NT_DIMS, preferred_element_type=F32)

        def reduce_chunk(c, s, m, acc):
            k0 = pl.multiple_of(c * KEY_CHUNK, KEY_CHUNK)
            v_one = jnp.where(own, vs_ref[0, pl.ds(k0, KEY_CHUNK), :], 1.0).astype(BF16)
            m_new = jnp.maximum(m, jnp.max(s, axis=-1, keepdims=True))
            pc = jnp.exp(s - m_new).astype(BF16)
            return m_new, jnp.exp(m - m_new) * acc + jnp.dot(pc, v_one, preferred_element_type=F32)

        def step(c, carry):
            s, m, acc = carry
            s_next = scores(c + 1)
            m, acc = reduce_chunk(c, s, m, acc)
            return s_next, m, acc

        init = (scores(0), jnp.full((rows, 1), NEG_INF, F32), jnp.zeros((rows, LANES), F32))
        s, m, acc = lax.fori_loop(0, n_full, step, init)
        kpos = n_full * KEY_CHUNK + lax.broadcasted_iota(jnp.int32, (1, KEY_CHUNK), 1)
        _, acc = reduce_chunk(n_full, jnp.where(kpos <= qpos4, s, NEG_INF), m, acc)
        acc = acc / acc[:, den_lane:den_lane + 1]
        o_slc = [acc[r * Q_TILE:(r + 1) * Q_TILE] for r in range(NSA_REP)]

        w0 = pl.multiple_of(jnp.maximum(t0 + Q_TILE - win_len, 0), Q_TILE)
        kw = kw_ref[0, pl.ds(w0, win_len), :]
        vw_one = jnp.where(own, vw_ref[0, pl.ds(w0, win_len), :], 1.0).astype(BF16)
        dpos = (w0 + lax.broadcasted_iota(jnp.int32, (1, win_len), 1)) - qpos4
        wmask = (dpos + WINDOW).astype(jnp.uint32) <= WINDOW
        s = jnp.where(wmask, lax.dot_general(q_all, kw, NT_DIMS, preferred_element_type=F32), NEG_INF)
        pw = jnp.exp(s - jnp.max(s, axis=-1, keepdims=True)).astype(BF16)
        acc = jnp.dot(pw, vw_one, preferred_element_type=F32)
        acc = acc / acc[:, den_lane:den_lane + 1]
        o_win = [acc[r * Q_TILE:(r + 1) * Q_TILE] for r in range(NSA_REP)]

        for r in range(NSA_REP):
            hh = NSA_REP * g + r
            mix = (gd[:, 3 * hh:3 * hh + 1] * o_cmp[r] + gd[:, 3 * hh + 1:3 * hh + 2] * o_slc[r]
                   + gd[:, 3 * hh + 2:3 * hh + 3] * o_win[r])
            o_ref[0, :, hh * HEAD_DIM:(hh + 1) * HEAD_DIM] = mix[:, g * HEAD_DIM:(g + 1) * HEAD_DIM]


def nsa_prompt(q, gd, cmp, kvb, onehot, ovt):
    b, t, _ = q.shape
    ncp = cmp.shape[2]
    n_blocks = t // SEL_BLOCK
    assert n_blocks <= SEL_PAD and t % KEY_CHUNK == 0
    win_len = min(WINDOW + Q_TILE, t)
    kern = functools.partial(_nsa_kernel, n_blocks=n_blocks, n_sel=min(SEL_TOPN, n_blocks), win_len=win_len)
    slab = lambda j: pl.BlockSpec((1, t, LANES), lambda bi, i, j=j: (bi, 0, j))
    return pl.pallas_call(
        kern,
        grid=(b, t // Q_TILE),
        in_specs=[pl.BlockSpec((1, Q_TILE, NSA_HEADS * LANES), lambda bi, i: (bi, i, 0)),
                  pl.BlockSpec((1, Q_TILE, LANES), lambda bi, i: (bi, i, 0)),
                  pl.BlockSpec((1, 1, ncp, LANES), lambda bi, i: (0, bi, 0, 0)),
                  pl.BlockSpec((1, 1, ncp, LANES), lambda bi, i: (1, bi, 0, 0)),
                  slab(2), slab(3), slab(4), slab(5),
                  _const_spec(onehot.shape), _const_spec(ovt.shape)],
        out_specs=pl.BlockSpec((1, Q_TILE, NSA_HEADS * HEAD_DIM), lambda bi, i: (bi, i, 0)),
        out_shape=jax.ShapeDtypeStruct((b, t, NSA_HEADS * HEAD_DIM), F32),
        compiler_params=_cparams("parallel", "arbitrary"),
    )(q, gd, cmp, cmp, kvb, kvb, kvb, kvb, onehot, ovt)


def _ssd_kernel(xbc_ref, z_ref, gd_ref, cw_ref, cb_ref, a_ref, dsk_ref, ng_ref,
                y_ref, hfin_ref, ext_ref, h_ref, y_scr):
    c = pl.program_id(1)
    L = SSM_CHUNK
    pd = SSM_HEADS * HEAD_DIM
    gw = SSM_STATE

    @pl.when(c == 0)
    def _():
        ext_ref[0:SUBLANES, :] = jnp.zeros((SUBLANES, ext_ref.shape[1]), F32)
        h_ref[...] = jnp.zeros(h_ref.shape, F32)

    ext_ref[SUBLANES:SUBLANES + L, :] = xbc_ref[0]
    conv = cb_ref[...]
    for k in range(SSM_CONV):
        off = SUBLANES - (SSM_CONV - 1) + k
        conv = conv + ext_ref[off:off + L, :] * cw_ref[k:k + 1, :]
    ext_ref[0:SUBLANES, :] = ext_ref[L:L + SUBLANES, :]
    act = _silu(conv)
    xs = act[:, :pd]
    gd = gd_ref[0]

    ii = lax.broadcasted_iota(jnp.int32, (L, L), 0)
    jj = lax.broadcasted_iota(jnp.int32, (L, L), 1)
    causal = ii >= jj
    tri = jnp.where(causal, 1.0, 0.0).astype(BF16)
    acum = None
    for part in _split3(gd * a_ref[...]):
        term = jnp.dot(tri, part, preferred_element_type=F32)
        acum = term if acum is None else acum + term
    acum_t = acum.T

    for g in range(SSM_GROUPS):
        bm = act[:, pd + g * gw:pd + (g + 1) * gw].astype(BF16)
        cm = act[:, pd + SSM_GROUPS * gw + g * gw:pd + SSM_GROUPS * gw + (g + 1) * gw].astype(BF16)
        cb = lax.dot_general(cm, bm, NT_DIMS, preferred_element_type=F32)
        for hl in range(SSM_HEADS // SSM_GROUPS):
            hh = g * (SSM_HEADS // SSM_GROUPS) + hl
            col = acum[:, DT_LANE0 + hh:DT_LANE0 + hh + 1]
            row = acum_t[DT_LANE0 + hh:DT_LANE0 + hh + 1, :]
            last = acum[L - 1:L, DT_LANE0 + hh:DT_LANE0 + hh + 1]
            lmat = jnp.exp(jnp.where(causal, col - row, NEG_INF))
            x_h = xs[:, hh * HEAD_DIM:(hh + 1) * HEAD_DIM]
            xdt = x_h * gd[:, DT_LANE0 + hh:DT_LANE0 + hh + 1]
            y_diag = jnp.dot((cb * lmat).astype(BF16), xdt.astype(BF16), preferred_element_type=F32)
            h_in = h_ref[hh]
            y_off = lax.dot_general(cm, h_in.astype(BF16), NT_DIMS, preferred_element_type=F32) * jnp.exp(col)
            st = lax.dot_general((xdt * jnp.exp(last - col)).astype(BF16), bm, TN_DIMS,
                                 preferred_element_type=F32)
            h_ref[hh] = jnp.exp(last) * h_in + st
            y_scr[:, hh * HEAD_DIM:(hh + 1) * HEAD_DIM] = (
                y_diag + y_off + dsk_ref[:, hh * HEAD_DIM:(hh + 1) * HEAD_DIM] * x_h)

    y = y_scr[...] * _silu(z_ref[0])
    for g in range(SSM_GROUPS):
        cols = slice(g * gw, (g + 1) * gw)
        yg = y[:, cols]
        ms = jnp.mean(yg * yg, axis=-1, keepdims=True)
        y_ref[0, :, cols] = yg * lax.rsqrt(ms + EPS) * ng_ref[:, cols]

    @pl.when(c == pl.num_programs(1) - 1)
    def _():
        hfin_ref[0] = h_ref[...]


def ssd_prompt(xbc, z, gd, lw):
    b, t, cd = xbc.shape
    L = SSM_CHUNK
    pd = SSM_HEADS * HEAD_DIM
    tile = lambda n: pl.BlockSpec((1, L, n), lambda i, j: (i, j, 0))
    return pl.pallas_call(
        _ssd_kernel,
        grid=(b, t // L),
        in_specs=[tile(cd), tile(pd), tile(LANES),
                  _const_spec((SSM_CONV, cd)), _const_spec((1, cd)), _const_spec((1, LANES)),
                  _const_spec((1, pd)), _const_spec((1, pd))],
        out_specs=[tile(pd), pl.BlockSpec((1, SSM_HEADS, HEAD_DIM, SSM_STATE), lambda i, j: (i, 0, 0, 0))],
        out_shape=[jax.ShapeDtypeStruct((b, t, pd), F32),
                   jax.ShapeDtypeStruct((b, SSM_HEADS, HEAD_DIM, SSM_STATE), F32)],
        scratch_shapes=[pltpu.VMEM((SUBLANES + L, cd), F32),
                        pltpu.VMEM((SSM_HEADS, HEAD_DIM, SSM_STATE), F32),
                        pltpu.VMEM((L, pd), F32)],
        compiler_params=_cparams("parallel", "arbitrary"),
    )(xbc, z, gd, lw["conv_w"], lw["conv_b"], lw["a_row"], lw["d_skip"], lw["ssm_norm"])


PAGES_PER_STEP = 16
SEG_PER_PAGE = PAGE_SIZE // CMP_STRIDE
BLOCKS_PER_PAGE = PAGE_SIZE // SEL_BLOCK
N_PICK = SEL_TOPN - 1


def _past_compress_kernel(pt_ref, *refs):
    del pt_ref
    pages = refs[:PAGES_PER_STEP]
    nxt_ref, w_ref, pe_ref, o_ref = refs[PAGES_PER_STEP:]
    depth = w_ref.shape[0]
    nrow = PAGES_PER_STEP * SEG_PER_PAGE
    row8 = lax.broadcasted_iota(jnp.int32, (SUBLANES, CMP_STRIDE * LANES), 0)
    for l in range(depth):
        for kv in range(2):
            cols = []
            for jj in range(CMP_STRIDE):
                cols.append(jnp.concatenate(
                    [pg[l, kv, pl.ds(jj, SEG_PER_PAGE, stride=CMP_STRIDE), :] for pg in pages], axis=0))
            seg = jnp.concatenate(cols, axis=1).astype(BF16)
            nxt = jnp.concatenate([nxt_ref[l, kv, jj:jj + 1, :] for jj in range(CMP_STRIDE)], axis=1)
            tail = jnp.where(row8 == 0, nxt, pe_ref[l, kv].astype(F32)).astype(BF16)
            y = jnp.dot(jnp.concatenate([seg, tail], axis=0), w_ref[l, kv], preferred_element_type=F32)
            second = pltpu.roll(y[:, LANES:], nrow + SUBLANES - 1, 0)
            out = (y[:nrow, :LANES] + second[:nrow] + y[nrow + 1:nrow + 2, :LANES] + y[nrow + 2:nrow + 3, LANES:])
            o_ref[l, kv] = out.astype(o_ref.dtype)


def past_compress(cache5, page_table, w_all, pe_all):
    depth = cache5.shape[1]
    db, n_pages = page_table.shape
    assert n_pages % PAGES_PER_STEP == 0
    nrow = PAGES_PER_STEP * SEG_PER_PAGE
    page_spec = lambda k: pl.BlockSpec(
        (None, depth, 2, PAGE_SIZE, LANES), lambda b, j, pt, k=k: (pt[b, j * PAGES_PER_STEP + k], 0, 0, 0, 0))
    nxt_spec = pl.BlockSpec(
        (None, depth, 2, CMP_STRIDE, LANES),
        lambda b, j, pt: (pt[b, jnp.minimum((j + 1) * PAGES_PER_STEP, n_pages - 1)], 0, 0, 0, 0))
    const = lambda shape: pl.BlockSpec(shape, lambda b, j, pt: (0,) * len(shape))
    grid_spec = pltpu.PrefetchScalarGridSpec(
        num_scalar_prefetch=1,
        grid=(db, n_pages // PAGES_PER_STEP),
        in_specs=[page_spec(k) for k in range(PAGES_PER_STEP)] + [nxt_spec, const(w_all.shape), const(pe_all.shape)],
        out_specs=pl.BlockSpec((depth, 2, None, nrow, LANES), lambda b, j, pt: (0, 0, b, j, 0)),
    )
    return pl.pallas_call(
        _past_compress_kernel,
        grid_spec=grid_spec,
        out_shape=jax.ShapeDtypeStruct((depth, 2, db, n_pages * SEG_PER_PAGE, LANES), BF16),
        compiler_params=_cparams("parallel", "parallel"),
    )(page_table, *([cache5] * (PAGES_PER_STEP + 1)), w_all, pe_all)


def _sample_cmp_kernel(q_ref, kc_ref, vc_ref, ov_ref, ocmp_ref, idx_ref, *, past_len):
    q = q_ref[0]
    ncp = kc_ref.shape[3]
    s = lax.dot_general(q, kc_ref[0, 0, 0], NT_DIMS, preferred_element_type=F32)
    cend = lax.broadcasted_iota(jnp.int32, (1, ncp), 1) * CMP_STRIDE + (CMP_BLOCK - 1)
    cmask = cend <= past_len
    p = jnp.where(cmask, _softmax_rows(s, cmask), 0.0)
    ocmp_ref[0] = jnp.dot(p.astype(BF16), vc_ref[0, 0, 0], preferred_element_type=F32)
    hrow = lax.broadcasted_iota(jnp.int32, (NSA_HEADS, 1), 0)
    psum = jnp.concatenate(
        [jnp.sum(jnp.where((hrow // NSA_REP) == g, p, 0.0), axis=0, keepdims=True) for g in range(KV_GROUPS)]
        + [jnp.zeros((SUBLANES - KV_GROUPS, ncp), F32)], axis=0)
    imp = None
    for part in _split3(psum):
        term = jnp.dot(part, ov_ref[...], preferred_element_type=F32)
        imp = term if imp is None else imp + term
    n_past = past_len // SEL_BLOCK
    blk = lax.broadcasted_iota(jnp.int32, (SUBLANES, SEL_PAD), 1)
    forced = jnp.where(blk == 0, 1.0, jnp.where(blk == n_past - 1, 1.0, 0.0))
    score = jnp.where(blk < n_past, imp + SEL_BONUS * forced, -jnp.inf)
    col = lax.broadcasted_iota(jnp.int32, (SUBLANES, LANES), 1)
    picked = jnp.zeros((SUBLANES, LANES), jnp.int32)
    for k in range(N_PICK):
        m = jnp.max(score, axis=-1, keepdims=True)
        idx = jnp.min(jnp.where(score == m, blk, SEL_PAD), axis=-1, keepdims=True)
        score = jnp.where(blk == idx, -jnp.inf, score)
        picked = jnp.where(col == k, idx, picked)
    idx_ref[0] = picked


def sample_cmp_select(q8, cmp_past, layer, ov, past_len):
    db = q8.shape[0]
    ncp = cmp_past.shape[3]
    n_past = past_len // SEL_BLOCK
    assert n_past <= SEL_PAD and n_past + 1 > SEL_TOPN and past_len % SEL_BLOCK == 0
    return pl.pallas_call(
        functools.partial(_sample_cmp_kernel, past_len=past_len),
        grid=(db,),
        in_specs=[pl.BlockSpec((1, NSA_HEADS, LANES), lambda b: (b, 0, 0)),
                  pl.BlockSpec((1, 1, 1, ncp, LANES), lambda b: (layer, 0, b, 0, 0)),
                  pl.BlockSpec((1, 1, 1, ncp, LANES), lambda b: (layer, 1, b, 0, 0)),
                  _const_spec(ov.shape)],
        out_specs=[pl.BlockSpec((1, NSA_HEADS, LANES), lambda b: (b, 0, 0)),
                   pl.BlockSpec((1, SUBLANES, LANES), lambda b: (b, 0, 0))],
        out_shape=[jax.ShapeDtypeStruct((db, NSA_HEADS, LANES), F32),
                   jax.ShapeDtypeStruct((db, SUBLANES, LANES), jnp.int32)],
        compiler_params=_cparams("parallel"),
    )(q8, cmp_past, cmp_past, ov)


def _sample_attn_kernel(idx_ref, pt_ref, *refs):
    del idx_ref, pt_ref
    kblk = refs[:N_PICK]
    vblk = refs[N_PICK:2 * N_PICK]
    q_ref, kv_ref, gd_ref, ocmp_ref, win_ref, o_ref = refs[2 * N_PICK:]
    g = pl.program_id(1)
    q = q_ref[0]
    qf = q.astype(F32)
    kvn = kv_ref[0]
    lane = lax.broadcasted_iota(jnp.int32, (1, LANES), 1)

    def branch(own, key_blocks, val_blocks, j_new):
        ss = [lax.dot_general(q, kb.astype(BF16), NT_DIMS, preferred_element_type=F32) for kb in key_blocks]
        k_new = kvn[:, j_new * LANES:(j_new + 1) * LANES].astype(BF16).astype(F32)
        s_new = jnp.sum(qf * k_new, axis=-1, keepdims=True)
        m = s_new
        for s in ss:
            m = jnp.maximum(m, jnp.max(s, axis=-1, keepdims=True))
        v_new = jnp.where(own, kvn[:, (j_new + 1) * LANES:(j_new + 2) * LANES], 1.0).astype(BF16).astype(F32)
        acc = jnp.exp(s_new - m).astype(BF16).astype(F32) * v_new
        for s, vb in zip(ss, val_blocks):
            v_one = jnp.where(own, vb, 1.0).astype(BF16)
            acc = acc + jnp.dot(jnp.exp(s - m).astype(BF16), v_one, preferred_element_type=F32)
        return acc

    gd = gd_ref[0]
    ocmp = ocmp_ref[0]
    for gg in range(KV_GROUPS):
        @pl.when(g == gg)
        def _(gg=gg):
            own = (lane >= gg * HEAD_DIM) & (lane < (gg + 1) * HEAD_DIM)
            den = (1 - gg) * HEAD_DIM
            acc = branch(own, [r[...] for r in kblk], [r[...] for r in vblk], 2)
            o_slc = acc / acc[:, den:den + 1]
            acc = branch(own, [win_ref[0]], [win_ref[1]], 4)
            o_win = acc / acc[:, den:den + 1]
            for r in range(NSA_REP):
                hh = NSA_REP * gg + r
                mix = (gd[:, 3 * hh:3 * hh + 1] * ocmp[hh:hh + 1] + gd[:, 3 * hh + 1:3 * hh + 2] * o_slc[hh:hh + 1]
                       + gd[:, 3 * hh + 2:3 * hh + 3] * o_win[hh:hh + 1])
                o_ref[0, :, hh * HEAD_DIM:(hh + 1) * HEAD_DIM] = mix[:, gg * HEAD_DIM:(gg + 1) * HEAD_DIM]


def sample_attend(q8, kv, gd, ocmp, idx, page_table, cache5, win5, layer):
    db = q8.shape[0]
    blk_spec = lambda which, j: pl.BlockSpec(
        (None, None, None, SEL_BLOCK, LANES),
        lambda b, g, ix, pt, j=j: (pt[b, ix[b, g, j] // BLOCKS_PER_PAGE], layer, which,
                                   ix[b, g, j] % BLOCKS_PER_PAGE, 0))
    per_b = lambda shape: pl.BlockSpec((1,) + shape, lambda b, g, ix, pt: (b,) + (0,) * len(shape))
    wlen = win5.shape[3]
    grid_spec = pltpu.PrefetchScalarGridSpec(
        num_scalar_prefetch=2,
        grid=(db, KV_GROUPS),
        in_specs=[blk_spec(2, j) for j in range(N_PICK)] + [blk_spec(3, j) for j in range(N_PICK)]
                 + [per_b((NSA_HEADS, LANES)), per_b((1, 6 * LANES)), per_b((1, LANES)), per_b((NSA_HEADS, LANES)),
                    pl.BlockSpec((None, None, 2, wlen, LANES), lambda b, g, ix, pt: (b, layer, 0, 0, 0))],
        out_specs=per_b((1, NSA_HEADS * HEAD_DIM)),
    )
    return pl.pallas_call(
        _sample_attn_kernel,
        grid_spec=grid_spec,
        out_shape=jax.ShapeDtypeStruct((db, 1, NSA_HEADS * HEAD_DIM), F32),
        compiler_params=_cparams("parallel", "arbitrary"),
    )(idx, page_table, *([cache5] * (2 * N_PICK)), q8, kv, gd, ocmp, win5)


def _sample_state_kernel(u_ref, xbc_ref, z_ref, gd_ref, pool_ref, conv_ref, h_ref,
                         pw_ref, ps_ref, cw_ref, cb_ref, a_ref, dsk_ref, ng_ref,
                         ypool_ref, yssm_ref, npool_ref, nconv_ref, nh_ref):
    pd = SSM_HEADS * HEAD_DIM
    u = u_ref[0]
    ext = jnp.concatenate([pool_ref[...], u], axis=0)
    row = lax.broadcasted_iota(jnp.int32, ext.shape, 0)
    lane = lax.broadcasted_iota(jnp.int32, (1, ext.shape[1]), 1)
    d = jnp.zeros_like(u)
    for gi, win in enumerate(POOL_WINDOWS):
        s = jnp.sum(jnp.where(row >= POOL_BUF + 1 - win, ext, 0.0), axis=0, keepdims=True)
        d = jnp.where((lane >= gi * HEAD_DIM) & (lane < (gi + 1) * HEAD_DIM), s / float(win) - u, d)
    d8 = jnp.concatenate([d, jnp.zeros((SUBLANES - 1, d.shape[1]), F32)], axis=0).astype(BF16)
    ypool_ref[0] = jnp.dot(d8, pw_ref[...], preferred_element_type=F32)[0:1] * ps_ref[...]
    npool_ref[...] = ext[1:]

    xbc = xbc_ref[0]
    extc = jnp.concatenate([conv_ref[...], xbc], axis=0)
    act = _silu(jnp.sum(extc * cw_ref[...], axis=0, keepdims=True) + cb_ref[...])
    nconv_ref[...] = extc[1:]
    gd = gd_ref[0]
    dec_row = jnp.exp(gd * a_ref[...])
    eye = (lax.broadcasted_iota(jnp.int32, (HEAD_DIM, HEAD_DIM), 0)
           == lax.broadcasted_iota(jnp.int32, (HEAD_DIM, HEAD_DIM), 1))
    ys = []
    for hh in range(SSM_HEADS):
        g = hh // (SSM_HEADS // SSM_GROUPS)
        x_row = act[:, hh * HEAD_DIM:(hh + 1) * HEAD_DIM]
        dt = gd[:, DT_LANE0 + hh:DT_LANE0 + hh + 1]
        xdt_col = jnp.sum(jnp.where(eye, x_row * dt, 0.0), axis=1, keepdims=True)
        b_row = act[:, pd + g * SSM_STATE:pd + (g + 1) * SSM_STATE]
        c_row = act[:, pd + (SSM_GROUPS + g) * SSM_STATE:pd + (SSM_GROUPS + g + 1) * SSM_STATE]
        h_new = dec_row[:, DT_LANE0 + hh:DT_LANE0 + hh + 1] * h_ref[hh] + xdt_col * b_row
        nh_ref[hh] = h_new
        y_col = jnp.sum(h_new * c_row, axis=1, keepdims=True)
        y_row = jnp.sum(jnp.where(eye, y_col, 0.0), axis=0, keepdims=True)
        ys.append(y_row + dsk_ref[:, hh * HEAD_DIM:(hh + 1) * HEAD_DIM] * x_row)
    y = jnp.concatenate(ys, axis=1) * _silu(z_ref[0])
    outs = []
    for g in range(SSM_GROUPS):
        yg = y[:, g * SSM_STATE:(g + 1) * SSM_STATE]
        ms = jnp.mean(yg * yg, axis=-1, keepdims=True)
        outs.append(yg * lax.rsqrt(ms + EPS) * ng_ref[:, g * SSM_STATE:(g + 1) * SSM_STATE])
    yssm_ref[0] = jnp.concatenate(outs, axis=1)


def sample_state_mixers(u, xbc, z, gd, state_pool, state_conv, state_ssm, layer, lw):
    db = u.shape[0]
    per_b = lambda shape: pl.BlockSpec((1,) + shape, lambda b: (b,) + (0,) * len(shape))
    st = lambda shape: pl.BlockSpec((None, None) + shape, lambda b: (b, layer) + (0,) * len(shape))
    new = lambda shape: pl.BlockSpec((None,) + shape, lambda b: (b,) + (0,) * len(shape))
    ps, cs, hs = state_pool.shape[2:], state_conv.shape[2:], state_ssm.shape[2:]
    params = [lw['pool_w'], lw['pool_scale'], lw['conv_w'], lw['conv_b'], lw['a_row'], lw['d_skip'], lw['ssm_norm']]
    return pl.pallas_call(
        _sample_state_kernel,
        grid=(db,),
        in_specs=[per_b(u.shape[1:]), per_b(xbc.shape[1:]), per_b(z.shape[1:]), per_b(gd.shape[1:]),
                  st(ps), st(cs), st(hs)] + [_const_spec(p.shape) for p in params],
        out_specs=[per_b((1, ps[1])), per_b((1, z.shape[2])), new(ps), new(cs), new(hs)],
        out_shape=[jax.ShapeDtypeStruct((db, 1, ps[1]), F32), jax.ShapeDtypeStruct((db, 1, z.shape[2]), F32),
                   jax.ShapeDtypeStruct((db,) + ps, F32), jax.ShapeDtypeStruct((db,) + cs, F32),
                   jax.ShapeDtypeStruct((db,) + hs, F32)],
        compiler_params=_cparams("parallel"),
    )(u, xbc, z, gd, state_pool, state_conv, state_ssm, *params)


def _prep_layer(p, i):
    d = p['w_in'].shape[1]
    w_in = p['w_in'][i]
    o = np.cumsum([0, 256, 512, 768, GATE_LANES, 256, 768, SSM_HEADS])
    w_q = w_in[:, o[1]:o[2]].reshape(d, NSA_HEADS, HEAD_DIM)
    slot = jnp.zeros((d, NSA_HEADS, KV_GROUPS, HEAD_DIM), F32)
    for hh in range(NSA_HEADS):
        slot = slot.at[:, hh, hh // NSA_REP].set(w_q[:, hh])
    w_gd = jnp.zeros((d, LANES), F32)
    w_gd = w_gd.at[:, :GATE_LANES].set(w_in[:, o[3]:o[4]])
    w_gd = w_gd.at[:, DT_LANE0:DT_LANE0 + SSM_HEADS].set(w_in[:, o[6]:o[7]])
    lane_pad = lambda v: jnp.zeros((1, LANES), F32).at[0, DT_LANE0:DT_LANE0 + SSM_HEADS].set(v)
    eye_g = jnp.eye(KV_GROUPS, dtype=F32)
    ratio = CMP_BLOCK // CMP_STRIDE
    cmp_w, cmp_pe = [], []
    for j in range(2):
        w4 = p['nsa_cmp_w'][i, j].reshape(ratio, CMP_STRIDE, HEAD_DIM, HEAD_DIM)
        cmp_w.append(jnp.einsum('rjde,gh->jgdrhe', w4, eye_g).reshape(CMP_STRIDE * 2 * HEAD_DIM, ratio * LANES))
        pe4 = p['nsa_cmp_pe'][i, j].reshape(ratio, CMP_STRIDE, 1, HEAD_DIM)
        pe_rows = jnp.broadcast_to(pe4, (ratio, CMP_STRIDE, KV_GROUPS, HEAD_DIM)).reshape(ratio, -1)
        cmp_pe.append(jnp.zeros((SUBLANES, pe_rows.shape[1]), F32).at[:ratio].set(pe_rows))
    pool_w = jnp.zeros((256, 256), F32)
    for gi in range(len(POOL_WINDOWS)):
        sl = slice(gi * HEAD_DIM, (gi + 1) * HEAD_DIM)
        pool_w = pool_w.at[sl, sl].set(p['pool_w'][i, gi])
    w_out = p['w_out'][i]
    bf = lambda x: x.astype(BF16)
    row = lambda x: x.reshape(1, -1)
    return {
        'ffn1_norm': row(p['ffn1_norm'][i]), 'ffn1_w_gate': bf(p['ffn1_w_gate'][i]),
        'ffn1_w_up': bf(p['ffn1_w_up'][i]), 'ffn1_w_down': bf(p['ffn1_w_down'][i]),
        'ffn2_norm': row(p['ffn2_norm'][i]), 'ffn2_w_gate': bf(p['ffn2_w_gate'][i]),
        'ffn2_w_up': bf(p['ffn2_w_up'][i]), 'ffn2_w_down': bf(p['ffn2_w_down'][i]),
        'mix_norm': row(p['mix_norm'][i]),
        'w_u': bf(w_in[:, o[0]:o[1]]), 'w_q': bf(slot.reshape(d, NSA_HEADS * LANES)),
        'w_kv': bf(w_in[:, o[2]:o[3]]), 'w_gd': bf(w_gd), 'w_z': bf(w_in[:, o[4]:o[5]]),
        'w_xbc': bf(w_in[:, o[5]:o[6]]),
        'q_norm': row(jnp.tile(p['nsa_q_norm'][i], 2)), 'k_norm': jnp.tile(p['nsa_k_norm'][i], (1, 2)),
        'dt_bias': lane_pad(p['ssm_dt_bias'][i]),
        'w_out_pool': bf(w_out[:256]), 'w_out_nsa': bf(w_out[256:768]), 'w_out_ssm': bf(w_out[768:]),
        'pool_w': bf(pool_w), 'pool_scale': row(p['pool_scale'][i]),
        'cmp_w': bf(jnp.stack(cmp_w)), 'cmp_pe': bf(jnp.stack(cmp_pe)),
        'conv_w': p['ssm_conv_w'][i], 'conv_b': row(p['ssm_conv_b'][i]),
        'a_row': lane_pad(-jnp.exp(p['ssm_a_log'][i])),
        'd_skip': row(jnp.repeat(p['ssm_d'][i], HEAD_DIM)), 'ssm_norm': row(p['ssm_norm'][i]),
        'ple_norm': row(p['ple_norm'][i]), 'ple_w_gate': bf(p['ple_w_gate'][i]),
        'ple_w_proj': bf(p['ple_w_proj'][i]),
    }


def _rope_tables(pos):
    half = HEAD_DIM // 2
    inv = ROPE_THETA ** (-jnp.arange(half, dtype=F32) / half)
    ang = pos.astype(F32)[:, None] * inv[None, :]
    cos = jnp.cos(ang)
    sin = jnp.sin(ang)
    return jnp.tile(cos, (1, 4)), jnp.tile(jnp.concatenate([-sin, sin], axis=1), (1, 2))


def _selection_constants(t):
    nseg = t // CMP_STRIDE
    nc = nseg - CMP_BLOCK // CMP_STRIDE + 1
    ns = t // SEL_BLOCK
    c_start = np.arange(nseg) * CMP_STRIDE
    s_start = np.arange(SEL_PAD) * SEL_BLOCK
    ovt = ((c_start[None, :] < s_start[:, None] + SEL_BLOCK) & (c_start[None, :] + CMP_BLOCK > s_start[:, None])
           & (np.arange(nseg)[None, :] < nc) & (np.arange(SEL_PAD)[:, None] < ns))
    onehot = (np.arange(t)[:, None] // SEL_BLOCK) == np.arange(SEL_PAD)[None, :]
    return jnp.asarray(onehot, BF16), jnp.asarray(ovt, BF16)


def _token_layer_front(h, lw, cos, sin):
    h = ffn_halfstep(h, lw['ffn1_norm'], lw['ffn1_w_gate'], lw['ffn1_w_up'], lw['ffn1_w_down'])
    return (h,) + tuple(in_projection(h, lw, cos, sin))


def _token_layer_back(h, y_pool, y_nsa, y_ssm, pe, lw):
    h = out_projection(h, y_pool, y_nsa, y_ssm, lw)
    h = ffn_halfstep(h, lw['ffn2_norm'], lw['ffn2_w_gate'], lw['ffn2_w_up'], lw['ffn2_w_down'])
    return ple_step(h, pe, lw)


def kernel(x_prompt, x_sample, cache_nsa_kv, cache_win_kv, state_pool, state_conv, state_ssm, page_table,
           p_prompt, p_sample, ffn1_norm, ffn1_w_gate, ffn1_w_up, ffn1_w_down, mix_norm, w_in, w_out,
           pool_w, pool_scale, nsa_q_norm, nsa_k_norm, nsa_cmp_pe, nsa_cmp_w, ssm_conv_w, ssm_conv_b,
           ssm_dt_bias, ssm_a_log, ssm_d, ssm_norm, ffn2_norm, ffn2_w_gate, ffn2_w_up, ffn2_w_down,
           ple_norm, ple_w_gate, ple_w_proj):
    params = dict(ffn1_norm=ffn1_norm, ffn1_w_gate=ffn1_w_gate, ffn1_w_up=ffn1_w_up, ffn1_w_down=ffn1_w_down,
                  mix_norm=mix_norm, w_in=w_in, w_out=w_out, pool_w=pool_w, pool_scale=pool_scale,
                  nsa_q_norm=nsa_q_norm, nsa_k_norm=nsa_k_norm, nsa_cmp_pe=nsa_cmp_pe, nsa_cmp_w=nsa_cmp_w,
                  ssm_conv_w=ssm_conv_w, ssm_conv_b=ssm_conv_b, ssm_dt_bias=ssm_dt_bias, ssm_a_log=ssm_a_log,
                  ssm_d=ssm_d, ssm_norm=ssm_norm, ffn2_norm=ffn2_norm, ffn2_w_gate=ffn2_w_gate,
                  ffn2_w_up=ffn2_w_up, ffn2_w_down=ffn2_w_down, ple_norm=ple_norm, ple_w_gate=ple_w_gate,
                  ple_w_proj=ple_w_proj)
    depth = w_in.shape[0]
    b, t, d = x_prompt.shape
    db = x_sample.shape[0]
    past_len = page_table.shape[1] * PAGE_SIZE
    wkeep = min(WINDOW, t)

    cos_p, sin_p = _rope_tables(jnp.tile(jnp.arange(t, dtype=jnp.int32), b))
    cos_s, sin_s = _rope_tables(jnp.full((db,), past_len, jnp.int32))
    onehot, ovt = _selection_constants(t)

    lws = [_prep_layer(params, i) for i in range(depth)]
    cache5 = cache_nsa_kv.reshape(cache_nsa_kv.shape[0], depth, 4, PAGE_SIZE, LANES)
    win5 = cache_win_kv.reshape(db, depth, 2, cache_win_kv.shape[3], LANES)
    cmp_past = past_compress(cache5, page_table, jnp.stack([lw['cmp_w'] for lw in lws]),
                             jnp.stack([jnp.roll(lw['cmp_pe'], 1, axis=-2) for lw in lws]))
    ov_past = _selection_constants(past_len)[1].T
    s3 = lambda x: x.reshape(db, 1, x.shape[-1])

    h_p = x_prompt.reshape(b * t, d)
    h_s = x_sample.reshape(db, d)
    st_p = [[] for _ in range(5)]
    st_s = [[] for _ in range(5)]
    for i in range(depth):
        lw = lws[i]
        h_p, u, qp, kv, kvb, gd, z, xbc = _token_layer_front(h_p, lw, cos_p, sin_p)
        r3 = lambda x: x.reshape(b, t, x.shape[-1])
        y_pool = pool_prompt(r3(u), lw)
        seg = jnp.stack([kvb[:, 0:LANES], kvb[:, LANES:2 * LANES]]).reshape(2, b, t // CMP_STRIDE,
                                                                           CMP_STRIDE * LANES)
        cmp = nsa_compress(seg, lw['cmp_w'], lw['cmp_pe'])
        y_nsa = nsa_prompt(r3(qp), r3(gd), cmp, r3(kvb), onehot, ovt)
        y_ssm, h_fin = ssd_prompt(r3(xbc), r3(z), r3(gd), lw)
        h_p = _token_layer_back(h_p, y_pool.reshape(b * t, -1), y_nsa.reshape(b * t, -1),
                                y_ssm.reshape(b * t, -1), p_prompt[i].reshape(b * t, -1), lw)
        kv6 = kv.reshape(b, t, 6, KV_GROUPS, HEAD_DIM)
        st_p[0].append(jnp.moveaxis(kv6[:, :, 0:4], 2, 1))
        st_p[1].append(jnp.moveaxis(kv6[:, t - wkeep:, 4:6], 2, 1))
        st_p[2].append(r3(u)[:, t - POOL_BUF:])
        st_p[3].append(r3(xbc)[:, t - (SSM_CONV - 1):])
        st_p[4].append(h_fin)
        h_s, u, qp, kv, kvb, gd, z, xbc = _token_layer_front(h_s, lw, cos_s, sin_s)
        q8 = qp.reshape(db, NSA_HEADS, LANES)
        o_cmp, picked = sample_cmp_select(q8, cmp_past, i, ov_past, past_len)
        y_nsa = sample_attend(q8, s3(kv), s3(gd), o_cmp, picked[:, :KV_GROUPS, :SEL_TOPN], page_table,
                              cache5, win5, i)
        y_pool, y_ssm, new_pool, new_conv, new_h = sample_state_mixers(
            s3(u), s3(xbc), s3(z), s3(gd), state_pool, state_conv, state_ssm, i, lw)
        h_s = _token_layer_back(h_s, y_pool.reshape(db, -1), y_nsa.reshape(db, -1), y_ssm.reshape(db, -1),
                                p_sample[i].reshape(db, -1), lw)
        kv6 = kv.reshape(db, 6, 1, KV_GROUPS, HEAD_DIM)
        rows = kv6[:, 0:4]
        new_win = jnp.concatenate([cache_win_kv[:, i, :, 1:], kv6[:, 4:6]], axis=2)
        for j, v in enumerate((rows, new_win, new_pool, new_conv, new_h)):
            st_s[j].append(v)
    outs = [h_p.reshape(b, t, d), h_s.reshape(db, 1, d)]
    for j in range(5):
        outs.append(jnp.stack(st_p[j], axis=1))
        outs.append(jnp.stack(st_s[j], axis=1))
    return tuple(outs)
```

```python
import functools

import jax
import jax.numpy as jnp
import numpy as np
from jax import lax
from jax.experimental import pallas as pl
from jax.experimental.pallas import tpu as pltpu

F32 = jnp.float32
BF16 = jnp.bfloat16

POOL_WINDOWS = (2, 4, 8, 16)
POOL_BUF = 15
HEAD_DIM = 64
NSA_HEADS = 8
KV_GROUPS = 2
NSA_REP = NSA_HEADS // KV_GROUPS
CMP_BLOCK = 32
CMP_STRIDE = 16
SEL_BLOCK = 64
SEL_TOPN = 16
WINDOW = 512
SEL_BONUS = 1.0e4
NEG_INF = -1.0e30
SSM_HEADS = 4
SSM_GROUPS = 2
SSM_STATE = 128
SSM_CONV = 4
SSM_CHUNK = 128
ROPE_THETA = 10000.0
EPS = 1e-6
PAGE_SIZE = 128

LANES = 128
SUBLANES = 8
VMEM_LIMIT_BYTES = 56 * 1024 * 1024

GATE_LANES = 3 * NSA_HEADS
DT_LANE0 = 32

NT_DIMS = (((1,), (1,)), ((), ()))
TN_DIMS = (((0,), (0,)), ((), ()))


def _cparams(*sem):
    return pltpu.CompilerParams(dimension_semantics=sem, vmem_limit_bytes=VMEM_LIMIT_BYTES)


def _const_spec(shape):
    nd = len(shape)
    return pl.BlockSpec(shape, lambda *_: (0,) * nd)


def _rmsnorm(x, g):
    ms = jnp.mean(x * x, axis=-1, keepdims=True)
    return x * lax.rsqrt(ms + EPS) * g


def _silu(x):
    return x * jax.nn.sigmoid(x)


def _split3(x):
    hi = x.astype(BF16)
    r = x - hi.astype(F32)
    mid = r.astype(BF16)
    lo = (r - mid.astype(F32)).astype(BF16)
    return hi, mid, lo


def _ffn_kernel(x_ref, g_ref, wg_ref, wu_ref, wd_ref, o_ref, *, n_chunks):
    x = x_ref[...]
    xn = _rmsnorm(x, g_ref[...]).astype(BF16)
    fc = wg_ref.shape[1] // n_chunks
    tot = None
    for c in range(n_chunks):
        sl = slice(c * fc, (c + 1) * fc)
        g = jnp.dot(xn, wg_ref[:, sl], preferred_element_type=F32)
        u = jnp.dot(xn, wu_ref[:, sl], preferred_element_type=F32)
        a = (_silu(g) * u).astype(BF16)
        d = jnp.dot(a, wd_ref[sl, :], preferred_element_type=F32)
        tot = d if tot is None else tot + d
    o_ref[...] = x + 0.5 * tot


def _row_tile(m, pref):
    return pref if m % pref == 0 else m


def ffn_halfstep(h, g, wg, wu, wd):
    m, d = h.shape
    f = wg.shape[1]
    tm = _row_tile(m, 512)
    return pl.pallas_call(
        functools.partial(_ffn_kernel, n_chunks=2),
        grid=(m // tm,),
        in_specs=[pl.BlockSpec((tm, d), lambda i: (i, 0)),
                  _const_spec((1, d)), _const_spec((d, f)), _const_spec((d, f)), _const_spec((f, d))],
        out_specs=pl.BlockSpec((tm, d), lambda i: (i, 0)),
        out_shape=jax.ShapeDtypeStruct((m, d), F32),
        compiler_params=_cparams("parallel"),
    )(h, g, wg, wu, wd)


def _rope128(x, cos, sin_signed, lane):
    rot = jnp.where((lane & 32) == 0, pltpu.roll(x, 96, 1), pltpu.roll(x, 32, 1))
    return x * cos + rot * sin_signed


def _inproj_kernel(x_ref, g_ref, wu_ref, wq_ref, wkv_ref, wgd_ref, wz_ref, wx_ref,
                   qn_ref, kn_ref, dtb_ref, cos_ref, sin_ref,
                   u_ref, q_ref, kv_ref, kvb_ref, gd_ref, z_ref, xbc_ref):
    x = x_ref[...]
    tm = x.shape[0]
    xn = _rmsnorm(x, g_ref[...]).astype(BF16)
    cos = cos_ref[...]
    sin = sin_ref[...]
    lane = lax.broadcasted_iota(jnp.int32, (tm, LANES), 1)
    inv_hd = 1.0 / HEAD_DIM

    u_ref[...] = jnp.dot(xn, wu_ref[...], preferred_element_type=F32)
    z_ref[...] = jnp.dot(xn, wz_ref[...], preferred_element_type=F32)
    xbc_ref[...] = jnp.dot(xn, wx_ref[...], preferred_element_type=F32)

    y = jnp.dot(xn, wgd_ref[...], preferred_element_type=F32)
    yd = y + dtb_ref[...]
    softplus = jnp.maximum(yd, 0.0) + jnp.log1p(jnp.exp(-jnp.abs(yd)))
    gd_ref[...] = jnp.where(lane < DT_LANE0, jax.nn.sigmoid(y), softplus)

    q = jnp.dot(xn, wq_ref[...], preferred_element_type=F32)
    qn = qn_ref[...]
    for hh in range(NSA_HEADS):
        s = q[:, hh * LANES:(hh + 1) * LANES]
        ms = jnp.sum(s * s, axis=-1, keepdims=True) * inv_hd
        s = s * lax.rsqrt(ms + EPS) * qn
        s = _rope128(s, cos, sin, lane) * (HEAD_DIM ** -0.5)
        q_ref[:, hh * LANES:(hh + 1) * LANES] = s.astype(BF16)

    kv = jnp.dot(xn, wkv_ref[...], preferred_element_type=F32)
    low = lane < HEAD_DIM
    for j in range(6):
        s = kv[:, j * LANES:(j + 1) * LANES]
        if j % 2 == 0:
            sq = s * s
            s_all = jnp.sum(sq, axis=-1, keepdims=True)
            s_low = jnp.sum(jnp.where(low, sq, 0.0), axis=-1, keepdims=True)
            ms = jnp.where(low, s_low, s_all - s_low) * inv_hd
            s = s * lax.rsqrt(ms + EPS) * kn_ref[j // 2:j // 2 + 1, :]
            s = _rope128(s, cos, sin, lane)
        kv_ref[:, j * LANES:(j + 1) * LANES] = s
        kvb_ref[:, j * LANES:(j + 1) * LANES] = s.astype(BF16)


def in_projection(h, lw, cos, sin):
    m, d = h.shape
    tm = _row_tile(m, 512)
    row = lambda n: pl.BlockSpec((tm, n), lambda i: (i, 0))
    ws = [lw["w_u"], lw["w_q"], lw["w_kv"], lw["w_gd"], lw["w_z"], lw["w_xbc"]]
    outs = [(256, F32), (NSA_HEADS * LANES, BF16), (6 * LANES, F32), (6 * LANES, BF16),
            (LANES, F32), (256, F32), (768, F32)]
    return pl.pallas_call(
        _inproj_kernel,
        grid=(m // tm,),
        in_specs=[row(d), _const_spec((1, d))] + [_const_spec(w.shape) for w in ws]
                 + [_const_spec((1, LANES)), _const_spec((3, LANES)), _const_spec((1, LANES)),
                    row(LANES), row(LANES)],
        out_specs=[row(n) for n, _ in outs],
        out_shape=[jax.ShapeDtypeStruct((m, n), dt) for n, dt in outs],
        compiler_params=_cparams("parallel"),
    )(h, lw["mix_norm"], *ws, lw["q_norm"], lw["k_norm"], lw["dt_bias"], cos, sin)


def _outproj_kernel(h_ref, yp_ref, yn_ref, ys_ref, wp_ref, wn_ref, ws_ref, o_ref):
    acc = jnp.dot(yp_ref[...].astype(BF16), wp_ref[...], preferred_element_type=F32)
    acc = acc + jnp.dot(yn_ref[...].astype(BF16), wn_ref[...], preferred_element_type=F32)
    acc = acc + jnp.dot(ys_ref[...].astype(BF16), ws_ref[...], preferred_element_type=F32)
    o_ref[...] = h_ref[...] + acc


def out_projection(h, y_pool, y_nsa, y_ssm, lw):
    m, d = h.shape
    tm = _row_tile(m, 512)
    row = lambda n: pl.BlockSpec((tm, n), lambda i: (i, 0))
    ws = [lw["w_out_pool"], lw["w_out_nsa"], lw["w_out_ssm"]]
    return pl.pallas_call(
        _outproj_kernel,
        grid=(m // tm,),
        in_specs=[row(d), row(y_pool.shape[1]), row(y_nsa.shape[1]), row(y_ssm.shape[1])]
                 + [_const_spec(w.shape) for w in ws],
        out_specs=row(d),
        out_shape=jax.ShapeDtypeStruct((m, d), F32),
        compiler_params=_cparams("parallel"),
    )(h, y_pool, y_nsa, y_ssm, *ws)


def _ple_kernel(h_ref, pe_ref, g_ref, wg_ref, wp_ref, o_ref):
    h = h_ref[...]
    xn = _rmsnorm(h, g_ref[...]).astype(BF16)
    gate = jax.nn.sigmoid(jnp.dot(xn, wg_ref[...], preferred_element_type=F32))
    proj = jnp.dot(pe_ref[...].astype(BF16), wp_ref[...], preferred_element_type=F32)
    o_ref[...] = h + gate * proj


def ple_step(h, pe, lw):
    m, d = h.shape
    tm = _row_tile(m, 512)
    row = lambda n: pl.BlockSpec((tm, n), lambda i: (i, 0))
    return pl.pallas_call(
        _ple_kernel,
        grid=(m // tm,),
        in_specs=[row(d), row(pe.shape[1]), _const_spec((1, d)),
                  _const_spec(lw["ple_w_gate"].shape), _const_spec(lw["ple_w_proj"].shape)],
        out_specs=row(d),
        out_shape=jax.ShapeDtypeStruct((m, d), F32),
        compiler_params=_cparams("parallel"),
    )(h, pe, lw["ple_norm"], lw["ple_w_gate"], lw["ple_w_proj"])


POOL_HALO = 2 * SUBLANES


def _pool_kernel(u_ref, w_ref, sc_ref, y_ref, ext_ref, *, tm):
    c = pl.program_id(1)

    @pl.when(c == 0)
    def _():
        ext_ref[0:POOL_HALO, :] = jnp.zeros((POOL_HALO, ext_ref.shape[1]), F32)

    ext_ref[POOL_HALO:POOL_HALO + tm, :] = u_ref[0]
    lane = lax.broadcasted_iota(jnp.int32, (tm, LANES), 1)
    low = lane < HEAD_DIM
    pos1 = c * tm + lax.broadcasted_iota(jnp.int32, (tm, LANES), 0) + 1
    ds = []
    for slab, (w_lo, w_hi) in enumerate(((POOL_WINDOWS[0], POOL_WINDOWS[1]), (POOL_WINDOWS[2], POOL_WINDOWS[3]))):
        cols = slice(slab * LANES, (slab + 1) * LANES)
        x = ext_ref[POOL_HALO:POOL_HALO + tm, cols]
        run = x
        s_lo = None
        for k in range(1, w_hi):
            run = run + ext_ref[POOL_HALO - k:POOL_HALO - k + tm, cols]
            if k == w_lo - 1:
                s_lo = run
        cnt = jnp.where(low, jnp.minimum(pos1, w_lo), jnp.minimum(pos1, w_hi)).astype(F32)
        ds.append(jnp.where(low, s_lo, run) / cnt - x)
    d = jnp.concatenate(ds, axis=1).astype(BF16)
    y_ref[0] = jnp.dot(d, w_ref[...], preferred_element_type=F32) * sc_ref[...]
    ext_ref[0:POOL_HALO, :] = ext_ref[tm:tm + POOL_HALO, :]


def pool_prompt(u, lw):
    b, t, ch = u.shape
    tm = _row_tile(t, 512)
    return pl.pallas_call(
        functools.partial(_pool_kernel, tm=tm),
        grid=(b, t // tm),
        in_specs=[pl.BlockSpec((1, tm, ch), lambda i, j: (i, j, 0)),
                  _const_spec((ch, ch)), _const_spec((1, ch))],
        out_specs=pl.BlockSpec((1, tm, ch), lambda i, j: (i, j, 0)),
        out_shape=jax.ShapeDtypeStruct((b, t, ch), F32),
        scratch_shapes=[pltpu.VMEM((POOL_HALO + tm, ch), F32)],
        compiler_params=_cparams("parallel", "arbitrary"),
    )(u, lw["pool_w"], lw["pool_scale"])


def _compress_kernel(seg_ref, w_ref, pe_ref, o_ref):
    y = jnp.dot(seg_ref[0], w_ref[0], preferred_element_type=F32)
    pe = jnp.dot(pe_ref[0], w_ref[0], preferred_element_type=F32)
    nseg = y.shape[0]
    second = pltpu.roll(y[:, LANES:], nseg - 1, 0)
    row = lax.broadcasted_iota(jnp.int32, (nseg, LANES), 0)
    second = jnp.where(row < nseg - 1, second, 0.0)
    out = y[:, :LANES] + second + pe[0:1, :LANES] + pe[1:2, LANES:]
    o_ref[0, 0] = out.astype(o_ref.dtype)


def nsa_compress(seg, w, pe):
    _, b, nseg, width = seg.shape
    return pl.pallas_call(
        _compress_kernel,
        grid=(2, b),
        in_specs=[pl.BlockSpec((None, 1, nseg, width), lambda j, i: (j, i, 0, 0)),
                  pl.BlockSpec((1, width, 2 * LANES), lambda j, i: (j, 0, 0)),
                  pl.BlockSpec((1, SUBLANES, width), lambda j, i: (j, 0, 0))],
        out_specs=pl.BlockSpec((1, 1, nseg, LANES), lambda j, i: (j, i, 0, 0)),
        out_shape=jax.ShapeDtypeStruct((2, b, nseg, LANES), BF16),
        compiler_params=_cparams("parallel", "parallel"),
    )(seg, w, pe)


Q_TILE = 256
KEY_CHUNK = 512
SEL_PAD = 128


def _topk_mask_t(score, n_sel):
    blk = lax.broadcasted_iota(jnp.int32, score.shape, 0)

    def body(_, carry):
        sc, sel = carry
        m = jnp.max(sc, axis=0, keepdims=True)
        idx = jnp.min(jnp.where(sc == m, blk, score.shape[0]), axis=0, keepdims=True)
        hit = blk == idx
        return jnp.where(hit, -jnp.inf, sc), jnp.where(hit, 1.0, sel)

    _, sel = lax.fori_loop(0, n_sel, body, (score, jnp.zeros(score.shape, F32)))
    return sel


def _softmax_rows(s, mask):
    sm = jnp.where(mask, s, NEG_INF)
    m = jnp.max(sm, axis=-1, keepdims=True)
    e = jnp.exp(sm - m)
    return e / jnp.sum(e, axis=-1, keepdims=True)


def _nsa_kernel(q_ref, gd_ref, kc_ref, vc_ref, ks_ref, vs_ref, kw_ref, vw_ref, e_ref, ovt_ref, o_ref,
                sa_scr, sb_scr, m_scr, acc_scr, *, n_blocks, n_sel, win_len, last_chunk):
    i = pl.program_id(1)
    t0 = i * Q_TILE
    ncp = kc_ref.shape[2]
    qpos = t0 + lax.broadcasted_iota(jnp.int32, (Q_TILE, 1), 0)
    gd = gd_ref[0]

    blk = lax.broadcasted_iota(jnp.int32, (SEL_PAD, Q_TILE), 0)
    qpos_t = t0 + lax.broadcasted_iota(jnp.int32, (SEL_PAD, Q_TILE), 1)
    cur = qpos_t >> 6
    forced = jnp.where(blk == 0, 1.0, jnp.where(blk == cur, 1.0, jnp.where(blk == cur - 1, 1.0, 0.0)))
    valid = blk * SEL_BLOCK <= qpos_t
    real = blk < n_blocks

    n_full = t0 // KEY_CHUNK
    lane_row = lax.broadcasted_iota(jnp.int32, (1, LANES), 1)
    cend = lax.broadcasted_iota(jnp.int32, (1, ncp), 1) * CMP_STRIDE + (CMP_BLOCK - 1)
    for g in range(KV_GROUPS):
        own = (lane_row >= g * HEAD_DIM) & (lane_row < (g + 1) * HEAD_DIM)
        den_lane = (1 - g) * HEAD_DIM
        q_heads = [q_ref[0, :, (NSA_REP * g + r) * LANES:(NSA_REP * g + r + 1) * LANES] for r in range(NSA_REP)]

        rows = NSA_REP * Q_TILE
        q_all = jnp.concatenate(q_heads, axis=0)
        qpos4 = jnp.concatenate([qpos] * NSA_REP, axis=0)

        s = lax.dot_general(q_all, kc_ref[0, 0], NT_DIMS, preferred_element_type=F32)
        s = jnp.where(cend <= qpos4, s, NEG_INF)
        e = jnp.exp(s - jnp.max(s, axis=-1, keepdims=True))
        has_block = jnp.where(qpos4 >= CMP_BLOCK - 1, 1.0, 0.0)
        p = e * (has_block / jnp.sum(e, axis=-1, keepdims=True))
        acc = jnp.dot(p.astype(BF16), vc_ref[0, 0], preferred_element_type=F32)
        o_cmp = [acc[r * Q_TILE:(r + 1) * Q_TILE] for r in range(NSA_REP)]
        psum = p[0:Q_TILE]
        for r in range(1, NSA_REP):
            psum = psum + p[r * Q_TILE:(r + 1) * Q_TILE]

        imp_t = None
        for part in _split3(psum):
            term = lax.dot_general(ovt_ref[...], part, NT_DIMS, preferred_element_type=F32)
            imp_t = term if imp_t is None else imp_t + term
        score = jnp.where(valid, imp_t + SEL_BONUS * forced, -1.0)
        score = jnp.where(real, score, -jnp.inf)
        sel_t = _topk_mask_t(score, n_sel)
        bias = jnp.where(sel_t.T > 0.5, 0.0, NEG_INF).astype(BF16)
        q_aug = jnp.concatenate([jnp.concatenate([bias] * NSA_REP, axis=0), q_all], axis=1)

        def scores(c, s_ref):
            k0 = pl.multiple_of(jnp.minimum(c, last_chunk) * KEY_CHUNK, KEY_CHUNK)
            k_aug = jnp.concatenate([e_ref[pl.ds(k0, KEY_CHUNK), :], ks_ref[0, pl.ds(k0, KEY_CHUNK), :]], axis=1)
            s_ref[...] = lax.dot_general(q_aug, k_aug, NT_DIMS, preferred_element_type=F32)

        def reduce_chunk(c, s_ref, causal):
            k0 = pl.multiple_of(jnp.minimum(c, last_chunk) * KEY_CHUNK, KEY_CHUNK)
            v_one = jnp.where(own, vs_ref[0, pl.ds(k0, KEY_CHUNK), :], 1.0).astype(BF16)
            s = s_ref[...]
            if causal:
                kpos = c * KEY_CHUNK + lax.broadcasted_iota(jnp.int32, (1, KEY_CHUNK), 1)
                s = jnp.where(kpos <= qpos4, s, NEG_INF)
            m = m_scr[...]
            m_new = jnp.maximum(m, jnp.max(s, axis=-1, keepdims=True))
            pc = jnp.exp(s - m_new).astype(BF16)
            acc_scr[...] = jnp.exp(m - m_new) * acc_scr[...] + jnp.dot(pc, v_one, preferred_element_type=F32)
            m_scr[...] = m_new

        def pair(p, carry):
            scores(2 * p + 1, sb_scr)
            reduce_chunk(2 * p, sa_scr, False)
            scores(2 * p + 2, sa_scr)
            reduce_chunk(2 * p + 1, sb_scr, False)
            return carry

        m_scr[...] = jnp.full((rows, 1), NEG_INF, F32)
        acc_scr[...] = jnp.zeros((rows, LANES), F32)
        scores(0, sa_scr)
        n_pairs = n_full // 2
        lax.fori_loop(0, n_pairs, pair, 0)
        one_more = n_full > 2 * n_pairs

        @pl.when(one_more)
        def _():
            scores(2 * n_pairs + 1, sb_scr)

        reduce_chunk(2 * n_pairs, sa_scr, True)

        @pl.when(one_more)
        def _():
            reduce_chunk(2 * n_pairs + 1, sb_scr, True)

        acc = acc_scr[...]
        acc = acc / acc[:, den_lane:den_lane + 1]
        o_slc = [acc[r * Q_TILE:(r + 1) * Q_TILE] for r in range(NSA_REP)]

        w0 = pl.multiple_of(jnp.maximum(t0 + Q_TILE - win_len, 0), Q_TILE)
        kw = kw_ref[0, pl.ds(w0, win_len), :]
        vw_one = jnp.where(own, vw_ref[0, pl.ds(w0, win_len), :], 1.0).astype(BF16)
        dpos = (w0 + lax.broadcasted_iota(jnp.int32, (1, win_len), 1)) - qpos4
        wmask = (dpos + WINDOW).astype(jnp.uint32) <= WINDOW
        s = jnp.where(wmask, lax.dot_general(q_all, kw, NT_DIMS, preferred_element_type=F32), NEG_INF)
        pw = jnp.exp(s - jnp.max(s, axis=-1, keepdims=True)).astype(BF16)
        acc = jnp.dot(pw, vw_one, preferred_element_type=F32)
        acc = acc / acc[:, den_lane:den_lane + 1]
        o_win = [acc[r * Q_TILE:(r + 1) * Q_TILE] for r in range(NSA_REP)]

        for r in range(NSA_REP):
            hh = NSA_REP * g + r
            mix = (gd[:, 3 * hh:3 * hh + 1] * o_cmp[r] + gd[:, 3 * hh + 1:3 * hh + 2] * o_slc[r]
                   + gd[:, 3 * hh + 2:3 * hh + 3] * o_win[r])
            o_ref[0, :, hh * HEAD_DIM:(hh + 1) * HEAD_DIM] = mix[:, g * HEAD_DIM:(g + 1) * HEAD_DIM]


def nsa_prompt(q, gd, cmp, kvb, onehot, ovt):
    b, t, _ = q.shape
    ncp = cmp.shape[2]
    n_blocks = t // SEL_BLOCK
    assert n_blocks <= SEL_PAD and t % KEY_CHUNK == 0
    win_len = min(WINDOW + Q_TILE, t)
    kern = functools.partial(_nsa_kernel, n_blocks=n_blocks, n_sel=min(SEL_TOPN, n_blocks), win_len=win_len,
                             last_chunk=t // KEY_CHUNK - 1)
    rows = NSA_REP * Q_TILE
    slab = lambda j: pl.BlockSpec((1, t, LANES), lambda bi, i, j=j: (bi, 0, j))
    return pl.pallas_call(
        kern,
        grid=(b, t // Q_TILE),
        in_specs=[pl.BlockSpec((1, Q_TILE, NSA_HEADS * LANES), lambda bi, i: (bi, i, 0)),
                  pl.BlockSpec((1, Q_TILE, LANES), lambda bi, i: (bi, i, 0)),
                  pl.BlockSpec((1, 1, ncp, LANES), lambda bi, i: (0, bi, 0, 0)),
                  pl.BlockSpec((1, 1, ncp, LANES), lambda bi, i: (1, bi, 0, 0)),
                  slab(2), slab(3), slab(4), slab(5),
                  _const_spec(onehot.shape), _const_spec(ovt.shape)],
        out_specs=pl.BlockSpec((1, Q_TILE, NSA_HEADS * HEAD_DIM), lambda bi, i: (bi, i, 0)),
        out_shape=jax.ShapeDtypeStruct((b, t, NSA_HEADS * HEAD_DIM), F32),
        scratch_shapes=[pltpu.VMEM((rows, KEY_CHUNK), F32), pltpu.VMEM((rows, KEY_CHUNK), F32),
                        pltpu.VMEM((rows, 1), F32), pltpu.VMEM((rows, LANES), F32)],
        compiler_params=_cparams("parallel", "arbitrary"),
    )(q, gd, cmp, cmp, kvb, kvb, kvb, kvb, onehot, ovt)


def _ssd_kernel(xbc_ref, z_ref, gd_ref, cw_ref, cb_ref, a_ref, dsk_ref, ng_ref,
                y_ref, hfin_ref, ext_ref, h_ref, y_scr):
    c = pl.program_id(1)
    L = SSM_CHUNK
    pd = SSM_HEADS * HEAD_DIM
    gw = SSM_STATE

    @pl.when(c == 0)
    def _():
        ext_ref[0:SUBLANES, :] = jnp.zeros((SUBLANES, ext_ref.shape[1]), F32)
        h_ref[...] = jnp.zeros(h_ref.shape, F32)

    ext_ref[SUBLANES:SUBLANES + L, :] = xbc_ref[0]
    conv = cb_ref[...]
    for k in range(SSM_CONV):
        off = SUBLANES - (SSM_CONV - 1) + k
        conv = conv + ext_ref[off:off + L, :] * cw_ref[k:k + 1, :]
    ext_ref[0:SUBLANES, :] = ext_ref[L:L + SUBLANES, :]
    act = _silu(conv)
    xs = act[:, :pd]
    gd = gd_ref[0]

    ii = lax.broadcasted_iota(jnp.int32, (L, L), 0)
    jj = lax.broadcasted_iota(jnp.int32, (L, L), 1)
    causal = ii >= jj
    tri = jnp.where(causal, 1.0, 0.0).astype(BF16)
    acum = None
    for part in _split3(gd * a_ref[...]):
        term = jnp.dot(tri, part, preferred_element_type=F32)
        acum = term if acum is None else acum + term
    acum_t = acum.T

    for g in range(SSM_GROUPS):
        bm = act[:, pd + g * gw:pd + (g + 1) * gw].astype(BF16)
        cm = act[:, pd + SSM_GROUPS * gw + g * gw:pd + SSM_GROUPS * gw + (g + 1) * gw].astype(BF16)
        cb = lax.dot_general(cm, bm, NT_DIMS, preferred_element_type=F32)
        for hl in range(SSM_HEADS // SSM_GROUPS):
            hh = g * (SSM_HEADS // SSM_GROUPS) + hl
            col = acum[:, DT_LANE0 + hh:DT_LANE0 + hh + 1]
            row = acum_t[DT_LANE0 + hh:DT_LANE0 + hh + 1, :]
            last = acum[L - 1:L, DT_LANE0 + hh:DT_LANE0 + hh + 1]
            lmat = jnp.exp(jnp.where(causal, col - row, NEG_INF))
            x_h = xs[:, hh * HEAD_DIM:(hh + 1) * HEAD_DIM]
            xdt = x_h * gd[:, DT_LANE0 + hh:DT_LANE0 + hh + 1]
            y_diag = jnp.dot((cb * lmat).astype(BF16), xdt.astype(BF16), preferred_element_type=F32)
            h_in = h_ref[hh]
            y_off = lax.dot_general(cm, h_in.astype(BF16), NT_DIMS, preferred_element_type=F32) * jnp.exp(col)
            st = lax.dot_general((xdt * jnp.exp(last - col)).astype(BF16), bm, TN_DIMS,
                                 preferred_element_type=F32)
            h_ref[hh] = jnp.exp(last) * h_in + st
            y_scr[:, hh * HEAD_DIM:(hh + 1) * HEAD_DIM] = (
                y_diag + y_off + dsk_ref[:, hh * HEAD_DIM:(hh + 1) * HEAD_DIM] * x_h)

    y = y_scr[...] * _silu(z_ref[0])
    for g in range(SSM_GROUPS):
        cols = slice(g * gw, (g + 1) * gw)
        yg = y[:, cols]
        ms = jnp.mean(yg * yg, axis=-1, keepdims=True)
        y_ref[0, :, cols] = yg * lax.rsqrt(ms + EPS) * ng_ref[:, cols]

    @pl.when(c == pl.num_programs(1) - 1)
    def _():
        hfin_ref[0] = h_ref[...]


def ssd_prompt(xbc, z, gd, lw):
    b, t, cd = xbc.shape
    L = SSM_CHUNK
    pd = SSM_HEADS * HEAD_DIM
    tile = lambda n: pl.BlockSpec((1, L, n), lambda i, j: (i, j, 0))
    return pl.pallas_call(
        _ssd_kernel,
        grid=(b, t // L),
        in_specs=[tile(cd), tile(pd), tile(LANES),
                  _const_spec((SSM_CONV, cd)), _const_spec((1, cd)), _const_spec((1, LANES)),
                  _const_spec((1, pd)), _const_spec((1, pd))],
        out_specs=[tile(pd), pl.BlockSpec((1, SSM_HEADS, HEAD_DIM, SSM_STATE), lambda i, j: (i, 0, 0, 0))],
        out_shape=[jax.ShapeDtypeStruct((b, t, pd), F32),
                   jax.ShapeDtypeStruct((b, SSM_HEADS, HEAD_DIM, SSM_STATE), F32)],
        scratch_shapes=[pltpu.VMEM((SUBLANES + L, cd), F32),
                        pltpu.VMEM((SSM_HEADS, HEAD_DIM, SSM_STATE), F32),
                        pltpu.VMEM((L, pd), F32)],
        compiler_params=_cparams("parallel", "arbitrary"),
    )(xbc, z, gd, lw["conv_w"], lw["conv_b"], lw["a_row"], lw["d_skip"], lw["ssm_norm"])


SEG_PER_PAGE = PAGE_SIZE // CMP_STRIDE
BLOCKS_PER_PAGE = PAGE_SIZE // SEL_BLOCK
N_PICK = SEL_TOPN - 1


def _past_compress_kernel(pt_ref, *refs):
    del pt_ref
    n_pages = len(refs) - 4
    pages = refs[:n_pages]
    w_ref, pe_ref, o_ref, x_scr = refs[n_pages:]
    nrow = n_pages * SEG_PER_PAGE
    row = lax.broadcasted_iota(jnp.int32, (nrow, LANES), 0)
    for kv in range(2):
        for k, pg in enumerate(pages):
            x_scr[k * PAGE_SIZE:(k + 1) * PAGE_SIZE, :] = pg[kv].T
        seg = jnp.concatenate([x_scr[pl.ds(jj, nrow, stride=CMP_STRIDE), :].astype(BF16)
                               for jj in range(CMP_STRIDE)], axis=1)
        y = jnp.dot(jnp.concatenate([seg, pe_ref[kv]], axis=0), w_ref[kv], preferred_element_type=F32)
        second = pltpu.roll(y[:nrow, LANES:], nrow - 1, 0)
        second = jnp.where(row < nrow - 1, second, 0.0)
        out = y[:nrow, :LANES] + second + y[nrow:nrow + 1, :LANES] + y[nrow + 1:nrow + 2, LANES:]
        o_ref[kv] = out.astype(o_ref.dtype)


def past_compress(cache_t, page_table, w_all, pe_all):
    depth = cache_t.shape[1]
    db, n_pages = page_table.shape
    nrow = n_pages * SEG_PER_PAGE
    page_spec = lambda k: pl.BlockSpec(
        (None, None, 2, LANES, PAGE_SIZE), lambda l, b, pt, k=k: (pt[b, k], l, 0, 0, 0))
    grid_spec = pltpu.PrefetchScalarGridSpec(
        num_scalar_prefetch=1,
        grid=(depth, db),
        in_specs=[page_spec(k) for k in range(n_pages)]
                 + [pl.BlockSpec((None,) + w_all.shape[1:], lambda l, b, pt: (l, 0, 0, 0)),
                    pl.BlockSpec((None,) + pe_all.shape[1:], lambda l, b, pt: (l, 0, 0, 0))],
        out_specs=pl.BlockSpec((None, 2, None, nrow, LANES), lambda l, b, pt: (l, 0, b, 0, 0)),
        scratch_shapes=[pltpu.VMEM((n_pages * PAGE_SIZE, LANES), F32)],
    )
    return pl.pallas_call(
        _past_compress_kernel,
        grid_spec=grid_spec,
        out_shape=jax.ShapeDtypeStruct((depth, 2, db, nrow, LANES), BF16),
        compiler_params=_cparams("parallel", "parallel"),
    )(page_table, *([cache_t] * n_pages), w_all, pe_all)


def _sample_cmp_kernel(q_ref, kc_ref, vc_ref, ov_ref, ocmp_ref, idx_ref, *, past_len):
    q = q_ref[0]
    ncp = kc_ref.shape[3]
    s = lax.dot_general(q, kc_ref[0, 0, 0], NT_DIMS, preferred_element_type=F32)
    cend = lax.broadcasted_iota(jnp.int32, (1, ncp), 1) * CMP_STRIDE + (CMP_BLOCK - 1)
    cmask = cend <= past_len
    p = jnp.where(cmask, _softmax_rows(s, cmask), 0.0)
    ocmp_ref[0] = jnp.dot(p.astype(BF16), vc_ref[0, 0, 0], preferred_element_type=F32)
    hrow = lax.broadcasted_iota(jnp.int32, (NSA_HEADS, 1), 0)
    psum = jnp.concatenate(
        [jnp.sum(jnp.where((hrow // NSA_REP) == g, p, 0.0), axis=0, keepdims=True) for g in range(KV_GROUPS)]
        + [jnp.zeros((SUBLANES - KV_GROUPS, ncp), F32)], axis=0)
    imp = None
    for part in _split3(psum):
        term = jnp.dot(part, ov_ref[...], preferred_element_type=F32)
        imp = term if imp is None else imp + term
    n_past = past_len // SEL_BLOCK
    blk = lax.broadcasted_iota(jnp.int32, (SUBLANES, SEL_PAD), 1)
    forced = jnp.where(blk == 0, 1.0, jnp.where(blk == n_past - 1, 1.0, 0.0))
    score = jnp.where(blk < n_past, imp + SEL_BONUS * forced, -jnp.inf)
    col = lax.broadcasted_iota(jnp.int32, (SUBLANES, LANES), 1)
    picked = jnp.zeros((SUBLANES, LANES), jnp.int32)
    for k in range(N_PICK):
        m = jnp.max(score, axis=-1, keepdims=True)
        idx = jnp.min(jnp.where(score == m, blk, SEL_PAD), axis=-1, keepdims=True)
        score = jnp.where(blk == idx, -jnp.inf, score)
        picked = jnp.where(col == k, idx, picked)
    idx_ref[0] = picked


def sample_cmp_select(q8, cmp_past, layer, ov, past_len):
    db = q8.shape[0]
    ncp = cmp_past.shape[3]
    n_past = past_len // SEL_BLOCK
    assert n_past <= SEL_PAD and n_past + 1 > SEL_TOPN and past_len % SEL_BLOCK == 0
    return pl.pallas_call(
        functools.partial(_sample_cmp_kernel, past_len=past_len),
        grid=(db,),
        in_specs=[pl.BlockSpec((1, NSA_HEADS, LANES), lambda b: (b, 0, 0)),
                  pl.BlockSpec((1, 1, 1, ncp, LANES), lambda b: (layer, 0, b, 0, 0)),
                  pl.BlockSpec((1, 1, 1, ncp, LANES), lambda b: (layer, 1, b, 0, 0)),
                  _const_spec(ov.shape)],
        out_specs=[pl.BlockSpec((1, NSA_HEADS, LANES), lambda b: (b, 0, 0)),
                   pl.BlockSpec((1, SUBLANES, LANES), lambda b: (b, 0, 0))],
        out_shape=[jax.ShapeDtypeStruct((db, NSA_HEADS, LANES), F32),
                   jax.ShapeDtypeStruct((db, SUBLANES, LANES), jnp.int32)],
        compiler_params=_cparams("parallel"),
    )(q8, cmp_past, cmp_past, ov)


def _sample_attn_kernel(idx_ref, pt_ref, *refs):
    del pt_ref
    kpage = refs[:N_PICK]
    vpage = refs[N_PICK:2 * N_PICK]
    q_ref, kv_ref, gd_ref, ocmp_ref, win_ref, o_ref = refs[2 * N_PICK:]
    b = pl.program_id(0)
    g = pl.program_id(1)
    q = q_ref[0]
    qf = q.astype(F32)
    kvn = kv_ref[0]
    lane = lax.broadcasted_iota(jnp.int32, (1, LANES), 1)
    chan = lax.broadcasted_iota(jnp.int32, (LANES, 1), 0)
    tok_half = lax.broadcasted_iota(jnp.int32, (1, PAGE_SIZE), 1) // SEL_BLOCK

    def branch(gg, keys_t, vals_t, tok_masks, j_new):
        own = (lane >= gg * HEAD_DIM) & (lane < (gg + 1) * HEAD_DIM)
        own_t = (chan >= gg * HEAD_DIM) & (chan < (gg + 1) * HEAD_DIM)
        ss = []
        for kt, tm in zip(keys_t, tok_masks):
            s = jnp.dot(q, kt.astype(BF16), preferred_element_type=F32)
            ss.append(s if tm is None else jnp.where(tm, s, NEG_INF))
        k_new = kvn[:, j_new * LANES:(j_new + 1) * LANES].astype(BF16).astype(F32)
        s_new = jnp.sum(qf * k_new, axis=-1, keepdims=True)
        m = s_new
        for s in ss:
            m = jnp.maximum(m, jnp.max(s, axis=-1, keepdims=True))
        v_new = jnp.where(own, kvn[:, (j_new + 1) * LANES:(j_new + 2) * LANES], 1.0).astype(BF16).astype(F32)
        acc = jnp.exp(s_new - m).astype(BF16).astype(F32) * v_new
        for s, vt in zip(ss, vals_t):
            v_one_t = jnp.where(own_t, vt, 1.0).astype(BF16)
            acc = acc + lax.dot_general(jnp.exp(s - m).astype(BF16), v_one_t, NT_DIMS, preferred_element_type=F32)
        return acc

    gd = gd_ref[0]
    ocmp = ocmp_ref[0]
    for gg in range(KV_GROUPS):
        @pl.when(g == gg)
        def _(gg=gg):
            den = (1 - gg) * HEAD_DIM
            masks = [tok_half == idx_ref[b, gg, j] % BLOCKS_PER_PAGE for j in range(N_PICK)]
            acc = branch(gg, [r[...] for r in kpage], [r[...] for r in vpage], masks, 2)
            o_slc = acc / acc[:, den:den + 1]
            acc = branch(gg, [win_ref[0]], [win_ref[1]], [None], 4)
            o_win = acc / acc[:, den:den + 1]
            for r in range(NSA_REP):
                hh = NSA_REP * gg + r
                mix = (gd[:, 3 * hh:3 * hh + 1] * ocmp[hh:hh + 1] + gd[:, 3 * hh + 1:3 * hh + 2] * o_slc[hh:hh + 1]
                       + gd[:, 3 * hh + 2:3 * hh + 3] * o_win[hh:hh + 1])
                o_ref[0, :, hh * HEAD_DIM:(hh + 1) * HEAD_DIM] = mix[:, gg * HEAD_DIM:(gg + 1) * HEAD_DIM]


def sample_attend(q8, kv, gd, ocmp, idx, page_table, cache_t, win_t, layer):
    db = q8.shape[0]
    page_spec = lambda which, j: pl.BlockSpec(
        (None, None, None, LANES, PAGE_SIZE),
        lambda b, g, ix, pt, j=j: (pt[b, ix[b, g, j] // BLOCKS_PER_PAGE], layer, which, 0, 0))
    per_b = lambda shape: pl.BlockSpec((1,) + shape, lambda b, g, ix, pt: (b,) + (0,) * len(shape))
    wlen = win_t.shape[4]
    grid_spec = pltpu.PrefetchScalarGridSpec(
        num_scalar_prefetch=2,
        grid=(db, KV_GROUPS),
        in_specs=[page_spec(2, j) for j in range(N_PICK)] + [page_spec(3, j) for j in range(N_PICK)]
                 + [per_b((NSA_HEADS, LANES)), per_b((1, 6 * LANES)), per_b((1, LANES)), per_b((NSA_HEADS, LANES)),
                    pl.BlockSpec((None, None, 2, LANES, wlen), lambda b, g, ix, pt: (b, layer, 0, 0, 0))],
        out_specs=per_b((1, NSA_HEADS * HEAD_DIM)),
    )
    return pl.pallas_call(
        _sample_attn_kernel,
        grid_spec=grid_spec,
        out_shape=jax.ShapeDtypeStruct((db, 1, NSA_HEADS * HEAD_DIM), F32),
        compiler_params=_cparams("parallel", "arbitrary"),
    )(idx, page_table, *([cache_t] * (2 * N_PICK)), q8, kv, gd, ocmp, win_t)


def _sample_state_kernel(u_ref, xbc_ref, z_ref, gd_ref, pool_ref, conv_ref, h_ref,
                         pw_ref, ps_ref, cw_ref, cb_ref, a_ref, dsk_ref, ng_ref,
                         ypool_ref, yssm_ref, npool_ref, nconv_ref, nh_ref):
    pd = SSM_HEADS * HEAD_DIM
    u = u_ref[0]
    ext = jnp.concatenate([pool_ref[...], u], axis=0)
    row = lax.broadcasted_iota(jnp.int32, ext.shape, 0)
    lane = lax.broadcasted_iota(jnp.int32, (1, ext.shape[1]), 1)
    d = jnp.zeros_like(u)
    for gi, win in enumerate(POOL_WINDOWS):
        s = jnp.sum(jnp.where(row >= POOL_BUF + 1 - win, ext, 0.0), axis=0, keepdims=True)
        d = jnp.where((lane >= gi * HEAD_DIM) & (lane < (gi + 1) * HEAD_DIM), s / float(win) - u, d)
    d8 = jnp.concatenate([d, jnp.zeros((SUBLANES - 1, d.shape[1]), F32)], axis=0).astype(BF16)
    ypool_ref[0] = jnp.dot(d8, pw_ref[...], preferred_element_type=F32)[0:1] * ps_ref[...]
    npool_ref[...] = ext[1:]

    xbc = xbc_ref[0]
    extc = jnp.concatenate([conv_ref[...], xbc], axis=0)
    act = _silu(jnp.sum(extc * cw_ref[...], axis=0, keepdims=True) + cb_ref[...])
    nconv_ref[...] = extc[1:]
    gd = gd_ref[0]
    dec_row = jnp.exp(gd * a_ref[...])
    eye = (lax.broadcasted_iota(jnp.int32, (HEAD_DIM, HEAD_DIM), 0)
           == lax.broadcasted_iota(jnp.int32, (HEAD_DIM, HEAD_DIM), 1))
    ys = []
    for hh in range(SSM_HEADS):
        g = hh // (SSM_HEADS // SSM_GROUPS)
        x_row = act[:, hh * HEAD_DIM:(hh + 1) * HEAD_DIM]
        dt = gd[:, DT_LANE0 + hh:DT_LANE0 + hh + 1]
        xdt_col = jnp.sum(jnp.where(eye, x_row * dt, 0.0), axis=1, keepdims=True)
        b_row = act[:, pd + g * SSM_STATE:pd + (g + 1) * SSM_STATE]
        c_row = act[:, pd + (SSM_GROUPS + g) * SSM_STATE:pd + (SSM_GROUPS + g + 1) * SSM_STATE]
        h_new = dec_row[:, DT_LANE0 + hh:DT_LANE0 + hh + 1] * h_ref[hh] + xdt_col * b_row
        nh_ref[hh] = h_new
        y_col = jnp.sum(h_new * c_row, axis=1, keepdims=True)
        y_row = jnp.sum(jnp.where(eye, y_col, 0.0), axis=0, keepdims=True)
        ys.append(y_row + dsk_ref[:, hh * HEAD_DIM:(hh + 1) * HEAD_DIM] * x_row)
    y = jnp.concatenate(ys, axis=1) * _silu(z_ref[0])
    outs = []
    for g in range(SSM_GROUPS):
        yg = y[:, g * SSM_STATE:(g + 1) * SSM_STATE]
        ms = jnp.mean(yg * yg, axis=-1, keepdims=True)
        outs.append(yg * lax.rsqrt(ms + EPS) * ng_ref[:, g * SSM_STATE:(g + 1) * SSM_STATE])
    yssm_ref[0] = jnp.concatenate(outs, axis=1)


def sample_state_mixers(u, xbc, z, gd, state_pool, state_conv, state_ssm, layer, lw):
    db = u.shape[0]
    per_b = lambda shape: pl.BlockSpec((1,) + shape, lambda b: (b,) + (0,) * len(shape))
    st = lambda shape: pl.BlockSpec((None, None) + shape, lambda b: (b, layer) + (0,) * len(shape))
    new = lambda shape: pl.BlockSpec((None,) + shape, lambda b: (b,) + (0,) * len(shape))
    ps, cs, hs = state_pool.shape[2:], state_conv.shape[2:], state_ssm.shape[2:]
    params = [lw['pool_w'], lw['pool_scale'], lw['conv_w'], lw['conv_b'], lw['a_row'], lw['d_skip'], lw['ssm_norm']]
    return pl.pallas_call(
        _sample_state_kernel,
        grid=(db,),
        in_specs=[per_b(u.shape[1:]), per_b(xbc.shape[1:]), per_b(z.shape[1:]), per_b(gd.shape[1:]),
                  st(ps), st(cs), st(hs)] + [_const_spec(p.shape) for p in params],
        out_specs=[per_b((1, ps[1])), per_b((1, z.shape[2])), new(ps), new(cs), new(hs)],
        out_shape=[jax.ShapeDtypeStruct((db, 1, ps[1]), F32), jax.ShapeDtypeStruct((db, 1, z.shape[2]), F32),
                   jax.ShapeDtypeStruct((db,) + ps, F32), jax.ShapeDtypeStruct((db,) + cs, F32),
                   jax.ShapeDtypeStruct((db,) + hs, F32)],
        compiler_params=_cparams("parallel"),
    )(u, xbc, z, gd, state_pool, state_conv, state_ssm, *params)


def _prep_layer(p, i):
    d = p['w_in'].shape[1]
    w_in = p['w_in'][i]
    o = np.cumsum([0, 256, 512, 768, GATE_LANES, 256, 768, SSM_HEADS])
    w_q = w_in[:, o[1]:o[2]].reshape(d, NSA_HEADS, HEAD_DIM)
    slot = jnp.zeros((d, NSA_HEADS, KV_GROUPS, HEAD_DIM), F32)
    for hh in range(NSA_HEADS):
        slot = slot.at[:, hh, hh // NSA_REP].set(w_q[:, hh])
    w_gd = jnp.zeros((d, LANES), F32)
    w_gd = w_gd.at[:, :GATE_LANES].set(w_in[:, o[3]:o[4]])
    w_gd = w_gd.at[:, DT_LANE0:DT_LANE0 + SSM_HEADS].set(w_in[:, o[6]:o[7]])
    lane_pad = lambda v: jnp.zeros((1, LANES), F32).at[0, DT_LANE0:DT_LANE0 + SSM_HEADS].set(v)
    eye_g = jnp.eye(KV_GROUPS, dtype=F32)
    ratio = CMP_BLOCK // CMP_STRIDE
    cmp_w, cmp_pe = [], []
    for j in range(2):
        w4 = p['nsa_cmp_w'][i, j].reshape(ratio, CMP_STRIDE, HEAD_DIM, HEAD_DIM)
        cmp_w.append(jnp.einsum('rjde,gh->jgdrhe', w4, eye_g).reshape(CMP_STRIDE * 2 * HEAD_DIM, ratio * LANES))
        pe4 = p['nsa_cmp_pe'][i, j].reshape(ratio, CMP_STRIDE, 1, HEAD_DIM)
        pe_rows = jnp.broadcast_to(pe4, (ratio, CMP_STRIDE, KV_GROUPS, HEAD_DIM)).reshape(ratio, -1)
        cmp_pe.append(jnp.zeros((SUBLANES, pe_rows.shape[1]), F32).at[:ratio].set(pe_rows))
    pool_w = jnp.zeros((256, 256), F32)
    for gi in range(len(POOL_WINDOWS)):
        sl = slice(gi * HEAD_DIM, (gi + 1) * HEAD_DIM)
        pool_w = pool_w.at[sl, sl].set(p['pool_w'][i, gi])
    w_out = p['w_out'][i]
    bf = lambda x: x.astype(BF16)
    row = lambda x: x.reshape(1, -1)
    return {
        'ffn1_norm': row(p['ffn1_norm'][i]), 'ffn1_w_gate': bf(p['ffn1_w_gate'][i]),
        'ffn1_w_up': bf(p['ffn1_w_up'][i]), 'ffn1_w_down': bf(p['ffn1_w_down'][i]),
        'ffn2_norm': row(p['ffn2_norm'][i]), 'ffn2_w_gate': bf(p['ffn2_w_gate'][i]),
        'ffn2_w_up': bf(p['ffn2_w_up'][i]), 'ffn2_w_down': bf(p['ffn2_w_down'][i]),
        'mix_norm': row(p['mix_norm'][i]),
        'w_u': bf(w_in[:, o[0]:o[1]]), 'w_q': bf(slot.reshape(d, NSA_HEADS * LANES)),
        'w_kv': bf(w_in[:, o[2]:o[3]]), 'w_gd': bf(w_gd), 'w_z': bf(w_in[:, o[4]:o[5]]),
        'w_xbc': bf(w_in[:, o[5]:o[6]]),
        'q_norm': row(jnp.tile(p['nsa_q_norm'][i], 2)), 'k_norm': jnp.tile(p['nsa_k_norm'][i], (1, 2)),
        'dt_bias': lane_pad(p['ssm_dt_bias'][i]),
        'w_out_pool': bf(w_out[:256]), 'w_out_nsa': bf(w_out[256:768]), 'w_out_ssm': bf(w_out[768:]),
        'pool_w': bf(pool_w), 'pool_scale': row(p['pool_scale'][i]),
        'cmp_w': bf(jnp.stack(cmp_w)), 'cmp_pe': bf(jnp.stack(cmp_pe)),
        'conv_w': p['ssm_conv_w'][i], 'conv_b': row(p['ssm_conv_b'][i]),
        'a_row': lane_pad(-jnp.exp(p['ssm_a_log'][i])),
        'd_skip': row(jnp.repeat(p['ssm_d'][i], HEAD_DIM)), 'ssm_norm': row(p['ssm_norm'][i]),
        'ple_norm': row(p['ple_norm'][i]), 'ple_w_gate': bf(p['ple_w_gate'][i]),
        'ple_w_proj': bf(p['ple_w_proj'][i]),
    }


def _rope_tables(pos):
    half = HEAD_DIM // 2
    inv = ROPE_THETA ** (-jnp.arange(half, dtype=F32) / half)
    ang = pos.astype(F32)[:, None] * inv[None, :]
    cos = jnp.cos(ang)
    sin = jnp.sin(ang)
    return jnp.tile(cos, (1, 4)), jnp.tile(jnp.concatenate([-sin, sin], axis=1), (1, 2))


def _selection_constants(t):
    nseg = t // CMP_STRIDE
    nc = nseg - CMP_BLOCK // CMP_STRIDE + 1
    ns = t // SEL_BLOCK
    c_start = np.arange(nseg) * CMP_STRIDE
    s_start = np.arange(SEL_PAD) * SEL_BLOCK
    ovt = ((c_start[None, :] < s_start[:, None] + SEL_BLOCK) & (c_start[None, :] + CMP_BLOCK > s_start[:, None])
           & (np.arange(nseg)[None, :] < nc) & (np.arange(SEL_PAD)[:, None] < ns))
    onehot = (np.arange(t)[:, None] // SEL_BLOCK) == np.arange(SEL_PAD)[None, :]
    return jnp.asarray(onehot, BF16), jnp.asarray(ovt, BF16)


def _channel_major(cache):
    nd = cache.ndim
    t = jnp.transpose(cache, tuple(range(nd - 3)) + (nd - 2, nd - 1, nd - 3))
    return t.reshape(cache.shape[:nd - 3] + (cache.shape[-2] * cache.shape[-1], cache.shape[-3]))


def _token_layer_front(h, lw, cos, sin):
    h = ffn_halfstep(h, lw['ffn1_norm'], lw['ffn1_w_gate'], lw['ffn1_w_up'], lw['ffn1_w_down'])
    return (h,) + tuple(in_projection(h, lw, cos, sin))


def _token_layer_back(h, y_pool, y_nsa, y_ssm, pe, lw):
    h = out_projection(h, y_pool, y_nsa, y_ssm, lw)
    h = ffn_halfstep(h, lw['ffn2_norm'], lw['ffn2_w_gate'], lw['ffn2_w_up'], lw['ffn2_w_down'])
    return ple_step(h, pe, lw)


def kernel(x_prompt, x_sample, cache_nsa_kv, cache_win_kv, state_pool, state_conv, state_ssm, page_table,
           p_prompt, p_sample, ffn1_norm, ffn1_w_gate, ffn1_w_up, ffn1_w_down, mix_norm, w_in, w_out,
           pool_w, pool_scale, nsa_q_norm, nsa_k_norm, nsa_cmp_pe, nsa_cmp_w, ssm_conv_w, ssm_conv_b,
           ssm_dt_bias, ssm_a_log, ssm_d, ssm_norm, ffn2_norm, ffn2_w_gate, ffn2_w_up, ffn2_w_down,
           ple_norm, ple_w_gate, ple_w_proj):
    params = dict(ffn1_norm=ffn1_norm, ffn1_w_gate=ffn1_w_gate, ffn1_w_up=ffn1_w_up, ffn1_w_down=ffn1_w_down,
                  mix_norm=mix_norm, w_in=w_in, w_out=w_out, pool_w=pool_w, pool_scale=pool_scale,
                  nsa_q_norm=nsa_q_norm, nsa_k_norm=nsa_k_norm, nsa_cmp_pe=nsa_cmp_pe, nsa_cmp_w=nsa_cmp_w,
                  ssm_conv_w=ssm_conv_w, ssm_conv_b=ssm_conv_b, ssm_dt_bias=ssm_dt_bias, ssm_a_log=ssm_a_log,
                  ssm_d=ssm_d, ssm_norm=ssm_norm, ffn2_norm=ffn2_norm, ffn2_w_gate=ffn2_w_gate,
                  ffn2_w_up=ffn2_w_up, ffn2_w_down=ffn2_w_down, ple_norm=ple_norm, ple_w_gate=ple_w_gate,
                  ple_w_proj=ple_w_proj)
    depth = w_in.shape[0]
    b, t, d = x_prompt.shape
    db = x_sample.shape[0]
    past_len = page_table.shape[1] * PAGE_SIZE
    wkeep = min(WINDOW, t)

    cos_p, sin_p = _rope_tables(jnp.tile(jnp.arange(t, dtype=jnp.int32), b))
    cos_s, sin_s = _rope_tables(jnp.full((db,), past_len, jnp.int32))
    onehot, ovt = _selection_constants(t)

    lws = [_prep_layer(params, i) for i in range(depth)]
    cache_t, win_t = _channel_major(cache_nsa_kv), _channel_major(cache_win_kv)
    cmp_past = past_compress(cache_t, page_table, jnp.stack([lw['cmp_w'] for lw in lws]),
                             jnp.stack([lw['cmp_pe'] for lw in lws]))
    ov_past = _selection_constants(past_len)[1].T
    s3 = lambda x: x.reshape(db, 1, x.shape[-1])

    h_p = x_prompt.reshape(b * t, d)
    h_s = x_sample.reshape(db, d)
    st_p = [[] for _ in range(5)]
    st_s = [[] for _ in range(5)]
    for i in range(depth):
        lw = lws[i]
        h_p, u, qp, kv, kvb, gd, z, xbc = _token_layer_front(h_p, lw, cos_p, sin_p)
        r3 = lambda x: x.reshape(b, t, x.shape[-1])
        y_pool = pool_prompt(r3(u), lw)
        seg = jnp.stack([kvb[:, 0:LANES], kvb[:, LANES:2 * LANES]]).reshape(2, b, t // CMP_STRIDE,
                                                                           CMP_STRIDE * LANES)
        cmp = nsa_compress(seg, lw['cmp_w'], lw['cmp_pe'])
        y_nsa = nsa_prompt(r3(qp), r3(gd), cmp, r3(kvb), onehot, ovt)
        y_ssm, h_fin = ssd_prompt(r3(xbc), r3(z), r3(gd), lw)
        h_p = _token_layer_back(h_p, y_pool.reshape(b * t, -1), y_nsa.reshape(b * t, -1),
                                y_ssm.reshape(b * t, -1), p_prompt[i].reshape(b * t, -1), lw)
        kv6 = kv.reshape(b, t, 6, KV_GROUPS, HEAD_DIM)
        st_p[0].append(jnp.moveaxis(kv6[:, :, 0:4], 2, 1))
        st_p[1].append(jnp.moveaxis(kv6[:, t - wkeep:, 4:6], 2, 1))
        st_p[2].append(r3(u)[:, t - POOL_BUF:])
        st_p[3].append(r3(xbc)[:, t - (SSM_CONV - 1):])
        st_p[4].append(h_fin)
        h_s, u, qp, kv, kvb, gd, z, xbc = _token_layer_front(h_s, lw, cos_s, sin_s)
        q8 = qp.reshape(db, NSA_HEADS, LANES)
        o_cmp, picked = sample_cmp_select(q8, cmp_past, i, ov_past, past_len)
        y_nsa = sample_attend(q8, s3(kv), s3(gd), o_cmp, picked[:, :KV_GROUPS, :SEL_TOPN], page_table,
                              cache_t, win_t, i)
        y_pool, y_ssm, new_pool, new_conv, new_h = sample_state_mixers(
            s3(u), s3(xbc), s3(z), s3(gd), state_pool, state_conv, state_ssm, i, lw)
        h_s = _token_layer_back(h_s, y_pool.reshape(db, -1), y_nsa.reshape(db, -1), y_ssm.reshape(db, -1),
                                p_sample[i].reshape(db, -1), lw)
        kv6 = kv.reshape(db, 6, 1, KV_GROUPS, HEAD_DIM)
        rows = kv6[:, 0:4]
        new_win = jnp.concatenate([cache_win_kv[:, i, :, 1:], kv6[:, 4:6]], axis=2)
        for j, v in enumerate((rows, new_win, new_pool, new_conv, new_h)):
            st_s[j].append(v)
    outs = [h_p.reshape(b, t, d), h_s.reshape(db, 1, d)]
    for j in range(5):
        outs.append(jnp.stack(st_p[j], axis=1))
        outs.append(jnp.stack(st_s[j], axis=1))
    return tuple(outs)
```

```python
import functools

import jax
import jax.numpy as jnp
import numpy as np
from jax import lax
from jax.experimental import pallas as pl
from jax.experimental.pallas import tpu as pltpu

F32 = jnp.float32
BF16 = jnp.bfloat16

POOL_WINDOWS = (2, 4, 8, 16)
POOL_BUF = 15
HEAD_DIM = 64
NSA_HEADS = 8
KV_GROUPS = 2
NSA_REP = NSA_HEADS // KV_GROUPS
CMP_BLOCK = 32
CMP_STRIDE = 16
SEL_BLOCK = 64
SEL_TOPN = 16
WINDOW = 512
SEL_BONUS = 1.0e4
NEG_INF = -1.0e30
SSM_HEADS = 4
SSM_GROUPS = 2
SSM_STATE = 128
SSM_CONV = 4
SSM_CHUNK = 128
ROPE_THETA = 10000.0
EPS = 1e-6
PAGE_SIZE = 128

LANES = 128
SUBLANES = 8
VMEM_LIMIT_BYTES = 56 * 1024 * 1024

GATE_LANES = 3 * NSA_HEADS
DT_LANE0 = 32

NT_DIMS = (((1,), (1,)), ((), ()))
TN_DIMS = (((0,), (0,)), ((), ()))


def _cparams(*sem):
    return pltpu.CompilerParams(dimension_semantics=sem, vmem_limit_bytes=VMEM_LIMIT_BYTES)


def _const_spec(shape):
    nd = len(shape)
    return pl.BlockSpec(shape, lambda *_: (0,) * nd)


def _rmsnorm(x, g):
    ms = jnp.mean(x * x, axis=-1, keepdims=True)
    return x * lax.rsqrt(ms + EPS) * g


def _silu(x):
    return x * jax.nn.sigmoid(x)


def _split3(x):
    hi = x.astype(BF16)
    r = x - hi.astype(F32)
    mid = r.astype(BF16)
    lo = (r - mid.astype(F32)).astype(BF16)
    return hi, mid, lo


def _ffn_kernel(x_ref, g_ref, wg_ref, wu_ref, wd_ref, o_ref, *, n_chunks):
    x = x_ref[...]
    xn = _rmsnorm(x, g_ref[...]).astype(BF16)
    fc = wg_ref.shape[1] // n_chunks
    tot = None
    for c in range(n_chunks):
        sl = slice(c * fc, (c + 1) * fc)
        g = jnp.dot(xn, wg_ref[:, sl], preferred_element_type=F32)
        u = jnp.dot(xn, wu_ref[:, sl], preferred_element_type=F32)
        a = (_silu(g) * u).astype(BF16)
        d = jnp.dot(a, wd_ref[sl, :], preferred_element_type=F32)
        tot = d if tot is None else tot + d
    o_ref[...] = x + 0.5 * tot


def _row_tile(m, pref):
    return pref if m % pref == 0 else m


def _resident_spec(shape):
    nd = len(shape)
    return pl.BlockSpec(shape, lambda *_: (0,) * nd, pipeline_mode=pl.Buffered(1))


def ffn_halfstep(h, g, wg, wu, wd):
    m, d = h.shape
    f = wg.shape[1]
    tm = _row_tile(m, 1024)
    return pl.pallas_call(
        functools.partial(_ffn_kernel, n_chunks=2 if tm < 1024 else f // (2 * LANES)),
        grid=(m // tm,),
        in_specs=[pl.BlockSpec((tm, d), lambda i: (i, 0)),
                  _const_spec((1, d)), _resident_spec((d, f)), _resident_spec((d, f)), _resident_spec((f, d))],
        out_specs=pl.BlockSpec((tm, d), lambda i: (i, 0)),
        out_shape=jax.ShapeDtypeStruct((m, d), F32),
        compiler_params=_cparams("parallel"),
    )(h, g, wg, wu, wd)


def _rope128(x, cos, sin_signed, lane):
    rot = jnp.where((lane & 32) == 0, pltpu.roll(x, 96, 1), pltpu.roll(x, 32, 1))
    return x * cos + rot * sin_signed


def _inproj_kernel(x_ref, g_ref, wu_ref, wq_ref, wkv_ref, wgd_ref, wz_ref, wx_ref,
                   qn_ref, kn_ref, dtb_ref, cos_ref, sin_ref, *out_refs, prompt):
    if prompt:
        u_ref, q_ref, rows_ref, wins_ref, kvb_ref, seg_ref, gd_ref, z_ref, xbc_ref, seg_scr = out_refs
    else:
        u_ref, q_ref, kv_ref, gd_ref, z_ref, xbc_ref = out_refs
    x = x_ref[...]
    tm = x.shape[0]
    xn = _rmsnorm(x, g_ref[...]).astype(BF16)
    cos = cos_ref[...]
    sin = sin_ref[...]
    lane = lax.broadcasted_iota(jnp.int32, (tm, LANES), 1)
    inv_hd = 1.0 / HEAD_DIM

    u_ref[...] = jnp.dot(xn, wu_ref[...], preferred_element_type=F32)
    z_ref[...] = jnp.dot(xn, wz_ref[...], preferred_element_type=F32)
    xbc_ref[...] = jnp.dot(xn, wx_ref[...], preferred_element_type=F32)

    y = jnp.dot(xn, wgd_ref[...], preferred_element_type=F32)
    yd = y + dtb_ref[...]
    softplus = jnp.maximum(yd, 0.0) + jnp.log1p(jnp.exp(-jnp.abs(yd)))
    gd_ref[...] = jnp.where(lane < DT_LANE0, jax.nn.sigmoid(y), softplus)

    q = jnp.dot(xn, wq_ref[...], preferred_element_type=F32)
    qn = qn_ref[...]
    for hh in range(NSA_HEADS):
        s = q[:, hh * LANES:(hh + 1) * LANES]
        ms = jnp.sum(s * s, axis=-1, keepdims=True) * inv_hd
        s = s * lax.rsqrt(ms + EPS) * qn
        s = _rope128(s, cos, sin, lane) * (HEAD_DIM ** -0.5)
        q_ref[:, hh * LANES:(hh + 1) * LANES] = s.astype(BF16)

    kv = jnp.dot(xn, wkv_ref[...], preferred_element_type=F32)
    low = lane < HEAD_DIM
    for j in range(6):
        s = kv[:, j * LANES:(j + 1) * LANES]
        if j % 2 == 0:
            sq = s * s
            s_all = jnp.sum(sq, axis=-1, keepdims=True)
            s_low = jnp.sum(jnp.where(low, sq, 0.0), axis=-1, keepdims=True)
            ms = jnp.where(low, s_low, s_all - s_low) * inv_hd
            s = s * lax.rsqrt(ms + EPS) * kn_ref[j // 2:j // 2 + 1, :]
            s = _rope128(s, cos, sin, lane)
        if not prompt:
            kv_ref[:, j * LANES:(j + 1) * LANES] = s
            continue
        if j < 4:
            rows_ref[0, j] = s
        else:
            wins_ref[0, j - 4] = s
        if j >= 2:
            kvb_ref[:, (j - 2) * LANES:(j - 1) * LANES] = s.astype(BF16)
        else:
            seg_scr[j] = s
            seg_ref[j] = jnp.concatenate(
                [seg_scr[j, pl.ds(jj, tm // CMP_STRIDE, stride=CMP_STRIDE), :].astype(BF16)
                 for jj in range(CMP_STRIDE)], axis=1)


def in_projection(h, lw, cos, sin, seq_len=None):
    m, d = h.shape
    tm = _row_tile(m, 512)
    row = lambda n: pl.BlockSpec((tm, n), lambda i: (i, 0))
    ws = [lw["w_u"], lw["w_q"], lw["w_kv"], lw["w_gd"], lw["w_z"], lw["w_xbc"]]
    tail = [(LANES, F32), (256, F32), (768, F32)]
    out_specs = [row(256), row(NSA_HEADS * LANES)]
    out_shape = [jax.ShapeDtypeStruct((m, 256), F32), jax.ShapeDtypeStruct((m, NSA_HEADS * LANES), BF16)]
    scratch = []
    if seq_len is None:
        out_specs.append(row(6 * LANES))
        out_shape.append(jax.ShapeDtypeStruct((m, 6 * LANES), F32))
    else:
        assert seq_len % tm == 0 and tm % (2 * SUBLANES * CMP_STRIDE) == 0
        tps = seq_len // tm
        per_seq = lambda n: pl.BlockSpec((1, n, tm, LANES), lambda i: (i // tps, 0, i % tps, 0))
        out_specs += [per_seq(4), per_seq(2), row(4 * LANES),
                      pl.BlockSpec((2, tm // CMP_STRIDE, CMP_STRIDE * LANES), lambda i: (0, i, 0))]
        out_shape += [jax.ShapeDtypeStruct((m // seq_len, 4, seq_len, LANES), F32),
                      jax.ShapeDtypeStruct((m // seq_len, 2, seq_len, LANES), F32),
                      jax.ShapeDtypeStruct((m, 4 * LANES), BF16),
                      jax.ShapeDtypeStruct((2, m // CMP_STRIDE, CMP_STRIDE * LANES), BF16)]
        scratch = [pltpu.VMEM((2, tm, LANES), F32)]
    out_specs += [row(n) for n, _ in tail]
    out_shape += [jax.ShapeDtypeStruct((m, n), dt) for n, dt in tail]
    return pl.pallas_call(
        functools.partial(_inproj_kernel, prompt=seq_len is not None),
        grid=(m // tm,),
        in_specs=[row(d), _const_spec((1, d))] + [_const_spec(w.shape) for w in ws]
                 + [_const_spec((1, LANES)), _const_spec((3, LANES)), _const_spec((1, LANES)),
                    row(LANES), row(LANES)],
        out_specs=out_specs,
        out_shape=out_shape,
        scratch_shapes=scratch,
        compiler_params=_cparams("parallel"),
    )(h, lw["mix_norm"], *ws, lw["q_norm"], lw["k_norm"], lw["dt_bias"], cos, sin)


def _outproj_kernel(h_ref, yp_ref, yn_ref, ys_ref, wp_ref, wn_ref, ws_ref, o_ref):
    acc = jnp.dot(yp_ref[...].astype(BF16), wp_ref[...], preferred_element_type=F32)
    acc = acc + jnp.dot(yn_ref[...].astype(BF16), wn_ref[...], preferred_element_type=F32)
    acc = acc + jnp.dot(ys_ref[...].astype(BF16), ws_ref[...], preferred_element_type=F32)
    o_ref[...] = h_ref[...] + acc


def out_projection(h, y_pool, y_nsa, y_ssm, lw):
    m, d = h.shape
    tm = _row_tile(m, 512)
    row = lambda n: pl.BlockSpec((tm, n), lambda i: (i, 0))
    ws = [lw["w_out_pool"], lw["w_out_nsa"], lw["w_out_ssm"]]
    return pl.pallas_call(
        _outproj_kernel,
        grid=(m // tm,),
        in_specs=[row(d), row(y_pool.shape[1]), row(y_nsa.shape[1]), row(y_ssm.shape[1])]
                 + [_const_spec(w.shape) for w in ws],
        out_specs=row(d),
        out_shape=jax.ShapeDtypeStruct((m, d), F32),
        compiler_params=_cparams("parallel"),
    )(h, y_pool, y_nsa, y_ssm, *ws)


def _ple_kernel(h_ref, pe_ref, g_ref, wg_ref, wp_ref, o_ref):
    h = h_ref[...]
    xn = _rmsnorm(h, g_ref[...]).astype(BF16)
    gate = jax.nn.sigmoid(jnp.dot(xn, wg_ref[...], preferred_element_type=F32))
    proj = jnp.dot(pe_ref[...].astype(BF16), wp_ref[...], preferred_element_type=F32)
    o_ref[...] = h + gate * proj


def ple_step(h, pe, lw):
    m, d = h.shape
    tm = _row_tile(m, 512)
    row = lambda n: pl.BlockSpec((tm, n), lambda i: (i, 0))
    return pl.pallas_call(
        _ple_kernel,
        grid=(m // tm,),
        in_specs=[row(d), row(pe.shape[1]), _const_spec((1, d)),
                  _const_spec(lw["ple_w_gate"].shape), _const_spec(lw["ple_w_proj"].shape)],
        out_specs=row(d),
        out_shape=jax.ShapeDtypeStruct((m, d), F32),
        compiler_params=_cparams("parallel"),
    )(h, pe, lw["ple_norm"], lw["ple_w_gate"], lw["ple_w_proj"])


POOL_HALO = 2 * SUBLANES


def _pool_kernel(u_ref, w_ref, sc_ref, y_ref, ext_ref, *, tm):
    c = pl.program_id(1)

    @pl.when(c == 0)
    def _():
        ext_ref[0:POOL_HALO, :] = jnp.zeros((POOL_HALO, ext_ref.shape[1]), F32)

    ext_ref[POOL_HALO:POOL_HALO + tm, :] = u_ref[0]
    lane = lax.broadcasted_iota(jnp.int32, (tm, LANES), 1)
    low = lane < HEAD_DIM
    pos1 = c * tm + lax.broadcasted_iota(jnp.int32, (tm, LANES), 0) + 1
    ds = []
    for slab, (w_lo, w_hi) in enumerate(((POOL_WINDOWS[0], POOL_WINDOWS[1]), (POOL_WINDOWS[2], POOL_WINDOWS[3]))):
        cols = slice(slab * LANES, (slab + 1) * LANES)
        x = ext_ref[POOL_HALO:POOL_HALO + tm, cols]
        run = x
        s_lo = None
        for k in range(1, w_hi):
            run = run + ext_ref[POOL_HALO - k:POOL_HALO - k + tm, cols]
            if k == w_lo - 1:
                s_lo = run
        cnt = jnp.where(low, jnp.minimum(pos1, w_lo), jnp.minimum(pos1, w_hi)).astype(F32)
        ds.append(jnp.where(low, s_lo, run) / cnt - x)
    d = jnp.concatenate(ds, axis=1).astype(BF16)
    y_ref[0] = jnp.dot(d, w_ref[...], preferred_element_type=F32) * sc_ref[...]
    ext_ref[0:POOL_HALO, :] = ext_ref[tm:tm + POOL_HALO, :]


def pool_prompt(u, lw):
    b, t, ch = u.shape
    tm = _row_tile(t, 512)
    return pl.pallas_call(
        functools.partial(_pool_kernel, tm=tm),
        grid=(b, t // tm),
        in_specs=[pl.BlockSpec((1, tm, ch), lambda i, j: (i, j, 0)),
                  _const_spec((ch, ch)), _const_spec((1, ch))],
        out_specs=pl.BlockSpec((1, tm, ch), lambda i, j: (i, j, 0)),
        out_shape=jax.ShapeDtypeStruct((b, t, ch), F32),
        scratch_shapes=[pltpu.VMEM((POOL_HALO + tm, ch), F32)],
        compiler_params=_cparams("parallel", "arbitrary"),
    )(u, lw["pool_w"], lw["pool_scale"])


def _compress_kernel(seg_ref, w_ref, pe_ref, o_ref):
    y = jnp.dot(seg_ref[0], w_ref[0], preferred_element_type=F32)
    pe = jnp.dot(pe_ref[0], w_ref[0], preferred_element_type=F32)
    nseg = y.shape[0]
    second = pltpu.roll(y[:, LANES:], nseg - 1, 0)
    row = lax.broadcasted_iota(jnp.int32, (nseg, LANES), 0)
    second = jnp.where(row < nseg - 1, second, 0.0)
    out = y[:, :LANES] + second + pe[0:1, :LANES] + pe[1:2, LANES:]
    o_ref[0, 0] = out.astype(o_ref.dtype)


def nsa_compress(seg, w, pe):
    _, b, nseg, width = seg.shape
    return pl.pallas_call(
        _compress_kernel,
        grid=(2, b),
        in_specs=[pl.BlockSpec((None, 1, nseg, width), lambda j, i: (j, i, 0, 0)),
                  pl.BlockSpec((1, width, 2 * LANES), lambda j, i: (j, 0, 0)),
                  pl.BlockSpec((1, SUBLANES, width), lambda j, i: (j, 0, 0))],
        out_specs=pl.BlockSpec((1, 1, nseg, LANES), lambda j, i: (j, i, 0, 0)),
        out_shape=jax.ShapeDtypeStruct((2, b, nseg, LANES), BF16),
        compiler_params=_cparams("parallel", "parallel"),
    )(seg, w, pe)


Q_TILE = 256
KEY_CHUNK = 512
SEL_PAD = 128


def _topk_mask_t(score, n_sel):
    blk = lax.broadcasted_iota(jnp.int32, score.shape, 0)

    def body(_, carry):
        sc, sel = carry
        m = jnp.max(sc, axis=0, keepdims=True)
        idx = jnp.min(jnp.where(sc == m, blk, score.shape[0]), axis=0, keepdims=True)
        hit = blk == idx
        return jnp.where(hit, -jnp.inf, sc), jnp.where(hit, 1.0, sel)

    _, sel = lax.fori_loop(0, n_sel, body, (score, jnp.zeros(score.shape, F32)))
    return sel


def _softmax_rows(s, mask):
    sm = jnp.where(mask, s, NEG_INF)
    m = jnp.max(sm, axis=-1, keepdims=True)
    e = jnp.exp(sm - m)
    return e / jnp.sum(e, axis=-1, keepdims=True)


def _nsa_kernel(q_ref, gd_ref, kc_ref, vc_ref, ks_ref, vs_ref, kw_ref, vw_ref, e_ref, ovt_ref, o_ref,
                sa_scr, sb_scr, m_scr, acc_scr, *, n_blocks, n_sel, win_len, last_chunk):
    i = pl.program_id(1)
    t0 = i * Q_TILE
    ncp = kc_ref.shape[2]
    qpos = t0 + lax.broadcasted_iota(jnp.int32, (Q_TILE, 1), 0)
    gd = gd_ref[0]

    blk = lax.broadcasted_iota(jnp.int32, (SEL_PAD, Q_TILE), 0)
    qpos_t = t0 + lax.broadcasted_iota(jnp.int32, (SEL_PAD, Q_TILE), 1)
    cur = qpos_t >> 6
    forced = jnp.where(blk == 0, 1.0, jnp.where(blk == cur, 1.0, jnp.where(blk == cur - 1, 1.0, 0.0)))
    valid = blk * SEL_BLOCK <= qpos_t
    real = blk < n_blocks

    n_full = t0 // KEY_CHUNK
    lane_row = lax.broadcasted_iota(jnp.int32, (1, LANES), 1)
    cend = lax.broadcasted_iota(jnp.int32, (1, ncp), 1) * CMP_STRIDE + (CMP_BLOCK - 1)
    for g in range(KV_GROUPS):
        own = (lane_row >= g * HEAD_DIM) & (lane_row < (g + 1) * HEAD_DIM)
        den_lane = (1 - g) * HEAD_DIM
        q_heads = [q_ref[0, :, (NSA_REP * g + r) * LANES:(NSA_REP * g + r + 1) * LANES] for r in range(NSA_REP)]

        rows = NSA_REP * Q_TILE
        q_all = jnp.concatenate(q_heads, axis=0)
        qpos4 = jnp.concatenate([qpos] * NSA_REP, axis=0)

        s = lax.dot_general(q_all, kc_ref[0, 0], NT_DIMS, preferred_element_type=F32)
        s = jnp.where(cend <= qpos4, s, NEG_INF)
        e = jnp.exp(s - jnp.max(s, axis=-1, keepdims=True))
        has_block = jnp.where(qpos4 >= CMP_BLOCK - 1, 1.0, 0.0)
        p = e * (has_block / jnp.sum(e, axis=-1, keepdims=True))
        acc = jnp.dot(p.astype(BF16), vc_ref[0, 0], preferred_element_type=F32)
        o_cmp = [acc[r * Q_TILE:(r + 1) * Q_TILE] for r in range(NSA_REP)]
        psum = p[0:Q_TILE]
        for r in range(1, NSA_REP):
            psum = psum + p[r * Q_TILE:(r + 1) * Q_TILE]

        imp_t = None
        for part in _split3(psum):
            term = lax.dot_general(ovt_ref[...], part, NT_DIMS, preferred_element_type=F32)
            imp_t = term if imp_t is None else imp_t + term
        score = jnp.where(valid, imp_t + SEL_BONUS * forced, -1.0)
        score = jnp.where(real, score, -jnp.inf)
        sel_t = _topk_mask_t(score, n_sel)
        bias = jnp.where(sel_t.T > 0.5, 0.0, NEG_INF).astype(BF16)
        q_aug = jnp.concatenate([jnp.concatenate([bias] * NSA_REP, axis=0), q_all], axis=1)

        def scores(c, s_ref):
            k0 = pl.multiple_of(jnp.minimum(c, last_chunk) * KEY_CHUNK, KEY_CHUNK)
            k_aug = jnp.concatenate([e_ref[pl.ds(k0, KEY_CHUNK), :], ks_ref[0, pl.ds(k0, KEY_CHUNK), :]], axis=1)
            s_ref[...] = lax.dot_general(q_aug, k_aug, NT_DIMS, preferred_element_type=F32)

        def reduce_chunk(c, s_ref, causal):
            k0 = pl.multiple_of(jnp.minimum(c, last_chunk) * KEY_CHUNK, KEY_CHUNK)
            v_one = jnp.where(own, vs_ref[0, pl.ds(k0, KEY_CHUNK), :], 1.0).astype(BF16)
            s = s_ref[...]
            if causal:
                kpos = c * KEY_CHUNK + lax.broadcasted_iota(jnp.int32, (1, KEY_CHUNK), 1)
                s = jnp.where(kpos <= qpos4, s, NEG_INF)
            m = m_scr[...]
            m_new = jnp.maximum(m, jnp.max(s, axis=-1, keepdims=True))
            pc = jnp.exp(s - m_new).astype(BF16)
            acc_scr[...] = jnp.exp(m - m_new) * acc_scr[...] + jnp.dot(pc, v_one, preferred_element_type=F32)
            m_scr[...] = m_new

        def pair(p, carry):
            scores(2 * p + 1, sb_scr)
            reduce_chunk(2 * p, sa_scr, False)
            scores(2 * p + 2, sa_scr)
            reduce_chunk(2 * p + 1, sb_scr, False)
            return carry

        m_scr[...] = jnp.full((rows, 1), NEG_INF, F32)
        acc_scr[...] = jnp.zeros((rows, LANES), F32)
        scores(0, sa_scr)
        n_pairs = n_full // 2
        lax.fori_loop(0, n_pairs, pair, 0)
        one_more = n_full > 2 * n_pairs

        @pl.when(one_more)
        def _():
            scores(2 * n_pairs + 1, sb_scr)

        reduce_chunk(2 * n_pairs, sa_scr, True)

        @pl.when(one_more)
        def _():
            reduce_chunk(2 * n_pairs + 1, sb_scr, True)

        acc = acc_scr[...]
        acc = acc / acc[:, den_lane:den_lane + 1]
        o_slc = [acc[r * Q_TILE:(r + 1) * Q_TILE] for r in range(NSA_REP)]

        w0 = pl.multiple_of(jnp.maximum(t0 + Q_TILE - win_len, 0), Q_TILE)
        kw = kw_ref[0, pl.ds(w0, win_len), :]
        vw_one = jnp.where(own, vw_ref[0, pl.ds(w0, win_len), :], 1.0).astype(BF16)
        dpos = (w0 + lax.broadcasted_iota(jnp.int32, (1, win_len), 1)) - qpos4
        wmask = (dpos + WINDOW).astype(jnp.uint32) <= WINDOW
        s = jnp.where(wmask, lax.dot_general(q_all, kw, NT_DIMS, preferred_element_type=F32), NEG_INF)
        pw = jnp.exp(s - jnp.max(s, axis=-1, keepdims=True)).astype(BF16)
        acc = jnp.dot(pw, vw_one, preferred_element_type=F32)
        acc = acc / acc[:, den_lane:den_lane + 1]
        o_win = [acc[r * Q_TILE:(r + 1) * Q_TILE] for r in range(NSA_REP)]

        for r in range(NSA_REP):
            hh = NSA_REP * g + r
            mix = (gd[:, 3 * hh:3 * hh + 1] * o_cmp[r] + gd[:, 3 * hh + 1:3 * hh + 2] * o_slc[r]
                   + gd[:, 3 * hh + 2:3 * hh + 3] * o_win[r])
            o_ref[0, :, hh * HEAD_DIM:(hh + 1) * HEAD_DIM] = mix[:, g * HEAD_DIM:(g + 1) * HEAD_DIM]


def nsa_prompt(q, gd, cmp, kvb, onehot, ovt):
    b, t, _ = q.shape
    ncp = cmp.shape[2]
    n_blocks = t // SEL_BLOCK
    assert n_blocks <= SEL_PAD and t % KEY_CHUNK == 0
    win_len = min(WINDOW + Q_TILE, t)
    kern = functools.partial(_nsa_kernel, n_blocks=n_blocks, n_sel=min(SEL_TOPN, n_blocks), win_len=win_len,
                             last_chunk=t // KEY_CHUNK - 1)
    rows = NSA_REP * Q_TILE
    slab = lambda j: pl.BlockSpec((1, t, LANES), lambda bi, i, j=j: (bi, 0, j))
    return pl.pallas_call(
        kern,
        grid=(b, t // Q_TILE),
        in_specs=[pl.BlockSpec((1, Q_TILE, NSA_HEADS * LANES), lambda bi, i: (bi, i, 0)),
                  pl.BlockSpec((1, Q_TILE, LANES), lambda bi, i: (bi, i, 0)),
                  pl.BlockSpec((1, 1, ncp, LANES), lambda bi, i: (0, bi, 0, 0)),
                  pl.BlockSpec((1, 1, ncp, LANES), lambda bi, i: (1, bi, 0, 0)),
                  slab(0), slab(1), slab(2), slab(3),
                  _const_spec(onehot.shape), _const_spec(ovt.shape)],
        out_specs=pl.BlockSpec((1, Q_TILE, NSA_HEADS * HEAD_DIM), lambda bi, i: (bi, i, 0)),
        out_shape=jax.ShapeDtypeStruct((b, t, NSA_HEADS * HEAD_DIM), F32),
        scratch_shapes=[pltpu.VMEM((rows, KEY_CHUNK), F32), pltpu.VMEM((rows, KEY_CHUNK), F32),
                        pltpu.VMEM((rows, 1), F32), pltpu.VMEM((rows, LANES), F32)],
        compiler_params=_cparams("parallel", "arbitrary"),
    )(q, gd, cmp, cmp, kvb, kvb, kvb, kvb, onehot, ovt)


def _ssd_kernel(xbc_ref, z_ref, gd_ref, cw_ref, cb_ref, a_ref, dsk_ref, ng_ref,
                y_ref, hfin_ref, ext_ref, h_ref, y_scr):
    c = pl.program_id(1)
    L = SSM_CHUNK
    pd = SSM_HEADS * HEAD_DIM
    gw = SSM_STATE

    @pl.when(c == 0)
    def _():
        ext_ref[0:SUBLANES, :] = jnp.zeros((SUBLANES, ext_ref.shape[1]), F32)
        h_ref[...] = jnp.zeros(h_ref.shape, F32)

    ext_ref[SUBLANES:SUBLANES + L, :] = xbc_ref[0]
    conv = cb_ref[...]
    for k in range(SSM_CONV):
        off = SUBLANES - (SSM_CONV - 1) + k
        conv = conv + ext_ref[off:off + L, :] * cw_ref[k:k + 1, :]
    ext_ref[0:SUBLANES, :] = ext_ref[L:L + SUBLANES, :]
    act = _silu(conv)
    xs = act[:, :pd]
    gd = gd_ref[0]

    ii = lax.broadcasted_iota(jnp.int32, (L, L), 0)
    jj = lax.broadcasted_iota(jnp.int32, (L, L), 1)
    causal = ii >= jj
    tri = jnp.where(causal, 1.0, 0.0).astype(BF16)
    acum = None
    for part in _split3(gd * a_ref[...]):
        term = jnp.dot(tri, part, preferred_element_type=F32)
        acum = term if acum is None else acum + term
    acum_t = acum.T

    for g in range(SSM_GROUPS):
        bm = act[:, pd + g * gw:pd + (g + 1) * gw].astype(BF16)
        cm = act[:, pd + SSM_GROUPS * gw + g * gw:pd + SSM_GROUPS * gw + (g + 1) * gw].astype(BF16)
        cb = lax.dot_general(cm, bm, NT_DIMS, preferred_element_type=F32)
        for hl in range(SSM_HEADS // SSM_GROUPS):
            hh = g * (SSM_HEADS // SSM_GROUPS) + hl
            col = acum[:, DT_LANE0 + hh:DT_LANE0 + hh + 1]
            row = acum_t[DT_LANE0 + hh:DT_LANE0 + hh + 1, :]
            last = acum[L - 1:L, DT_LANE0 + hh:DT_LANE0 + hh + 1]
            lmat = jnp.exp(jnp.where(causal, col - row, NEG_INF))
            x_h = xs[:, hh * HEAD_DIM:(hh + 1) * HEAD_DIM]
            xdt = x_h * gd[:, DT_LANE0 + hh:DT_LANE0 + hh + 1]
            y_diag = jnp.dot((cb * lmat).astype(BF16), xdt.astype(BF16), preferred_element_type=F32)
            h_in = h_ref[hh]
            y_off = lax.dot_general(cm, h_in.astype(BF16), NT_DIMS, preferred_element_type=F32) * jnp.exp(col)
            st = lax.dot_general((xdt * jnp.exp(last - col)).astype(BF16), bm, TN_DIMS,
                                 preferred_element_type=F32)
            h_ref[hh] = jnp.exp(last) * h_in + st
            y_scr[:, hh * HEAD_DIM:(hh + 1) * HEAD_DIM] = (
                y_diag + y_off + dsk_ref[:, hh * HEAD_DIM:(hh + 1) * HEAD_DIM] * x_h)

    y = y_scr[...] * _silu(z_ref[0])
    for g in range(SSM_GROUPS):
        cols = slice(g * gw, (g + 1) * gw)
        yg = y[:, cols]
        ms = jnp.mean(yg * yg, axis=-1, keepdims=True)
        y_ref[0, :, cols] = yg * lax.rsqrt(ms + EPS) * ng_ref[:, cols]

    @pl.when(c == pl.num_programs(1) - 1)
    def _():
        hfin_ref[0] = h_ref[...]


def ssd_prompt(xbc, z, gd, lw):
    b, t, cd = xbc.shape
    L = SSM_CHUNK
    pd = SSM_HEADS * HEAD_DIM
    tile = lambda n: pl.BlockSpec((1, L, n), lambda i, j: (i, j, 0))
    return pl.pallas_call(
        _ssd_kernel,
        grid=(b, t // L),
        in_specs=[tile(cd), tile(pd), tile(LANES),
                  _const_spec((SSM_CONV, cd)), _const_spec((1, cd)), _const_spec((1, LANES)),
                  _const_spec((1, pd)), _const_spec((1, pd))],
        out_specs=[tile(pd), pl.BlockSpec((1, SSM_HEADS, HEAD_DIM, SSM_STATE), lambda i, j: (i, 0, 0, 0))],
        out_shape=[jax.ShapeDtypeStruct((b, t, pd), F32),
                   jax.ShapeDtypeStruct((b, SSM_HEADS, HEAD_DIM, SSM_STATE), F32)],
        scratch_shapes=[pltpu.VMEM((SUBLANES + L, cd), F32),
                        pltpu.VMEM((SSM_HEADS, HEAD_DIM, SSM_STATE), F32),
                        pltpu.VMEM((L, pd), F32)],
        compiler_params=_cparams("parallel", "arbitrary"),
    )(xbc, z, gd, lw["conv_w"], lw["conv_b"], lw["a_row"], lw["d_skip"], lw["ssm_norm"])


SEG_PER_PAGE = PAGE_SIZE // CMP_STRIDE
BLOCKS_PER_PAGE = PAGE_SIZE // SEL_BLOCK
N_PICK = SEL_TOPN - 1
SEQ_PER_STEP = SUBLANES // KV_GROUPS


def _past_compress_kernel(pt_ref, *refs):
    del pt_ref
    n_pages = len(refs) - 4
    pages = refs[:n_pages]
    w_ref, pe_ref, o_ref, x_scr = refs[n_pages:]
    nrow = n_pages * SEG_PER_PAGE
    row = lax.broadcasted_iota(jnp.int32, (nrow, LANES), 0)
    for kv in range(2):
        for k, pg in enumerate(pages):
            x_scr[k * PAGE_SIZE:(k + 1) * PAGE_SIZE, :] = pg[kv].T
        seg = jnp.concatenate([x_scr[pl.ds(jj, nrow, stride=CMP_STRIDE), :].astype(BF16)
                               for jj in range(CMP_STRIDE)], axis=1)
        y = jnp.dot(jnp.concatenate([seg, pe_ref[kv]], axis=0), w_ref[kv], preferred_element_type=F32)
        second = pltpu.roll(y[:nrow, LANES:], nrow - 1, 0)
        second = jnp.where(row < nrow - 1, second, 0.0)
        out = y[:nrow, :LANES] + second + y[nrow:nrow + 1, :LANES] + y[nrow + 1:nrow + 2, LANES:]
        o_ref[kv] = out.astype(o_ref.dtype)


def past_compress(cache_t, page_table, w_all, pe_all):
    depth = cache_t.shape[1]
    db, n_pages = page_table.shape
    nrow = n_pages * SEG_PER_PAGE
    page_spec = lambda k: pl.BlockSpec(
        (None, None, 2, LANES, PAGE_SIZE), lambda l, b, pt, k=k: (pt[b, k], l, 0, 0, 0))
    grid_spec = pltpu.PrefetchScalarGridSpec(
        num_scalar_prefetch=1,
        grid=(depth, db),
        in_specs=[page_spec(k) for k in range(n_pages)]
                 + [pl.BlockSpec((None,) + w_all.shape[1:], lambda l, b, pt: (l, 0, 0, 0)),
                    pl.BlockSpec((None,) + pe_all.shape[1:], lambda l, b, pt: (l, 0, 0, 0))],
        out_specs=pl.BlockSpec((None, 2, None, nrow, LANES), lambda l, b, pt: (l, 0, b, 0, 0)),
        scratch_shapes=[pltpu.VMEM((n_pages * PAGE_SIZE, LANES), F32)],
    )
    return pl.pallas_call(
        _past_compress_kernel,
        grid_spec=grid_spec,
        out_shape=jax.ShapeDtypeStruct((depth, 2, db, nrow, LANES), BF16),
        compiler_params=_cparams("parallel", "parallel"),
    )(page_table, *([cache_t] * n_pages), w_all, pe_all)


def _sample_cmp_kernel(q_ref, kc_ref, vc_ref, ov_ref, ocmp_ref, idx_ref, *, past_len):
    ncp = kc_ref.shape[3]
    cend = lax.broadcasted_iota(jnp.int32, (1, ncp), 1) * CMP_STRIDE + (CMP_BLOCK - 1)
    cmask = cend <= past_len
    hrow = lax.broadcasted_iota(jnp.int32, (NSA_HEADS, 1), 0)
    sums = []
    for sq in range(SEQ_PER_STEP):
        q = q_ref[sq]
        s = lax.dot_general(q, kc_ref[0, 0, sq], NT_DIMS, preferred_element_type=F32)
        p = jnp.where(cmask, _softmax_rows(s, cmask), 0.0)
        ocmp_ref[sq] = jnp.dot(p.astype(BF16), vc_ref[0, 0, sq], preferred_element_type=F32)
        sums += [jnp.sum(jnp.where((hrow // NSA_REP) == g, p, 0.0), axis=0, keepdims=True)
                 for g in range(KV_GROUPS)]
    psum = jnp.concatenate(sums, axis=0)
    imp = None
    for part in _split3(psum):
        term = jnp.dot(part, ov_ref[...], preferred_element_type=F32)
        imp = term if imp is None else imp + term
    n_past = past_len // SEL_BLOCK
    blk = lax.broadcasted_iota(jnp.int32, (SUBLANES, SEL_PAD), 1)
    forced = jnp.where(blk == 0, 1.0, jnp.where(blk == n_past - 1, 1.0, 0.0))
    score = jnp.where(blk < n_past, imp + SEL_BONUS * forced, -jnp.inf)
    col = lax.broadcasted_iota(jnp.int32, (SUBLANES, LANES), 1)
    picked = jnp.zeros((SUBLANES, LANES), jnp.int32)
    for k in range(N_PICK):
        m = jnp.max(score, axis=-1, keepdims=True)
        idx = jnp.min(jnp.where(score == m, blk, SEL_PAD), axis=-1, keepdims=True)
        score = jnp.where(blk == idx, -jnp.inf, score)
        picked = jnp.where(col == k, idx, picked)
    idx_ref[0] = picked


def sample_cmp_select(q8, cmp_past, layer, ov, past_len):
    db = q8.shape[0]
    ncp = cmp_past.shape[3]
    n_past = past_len // SEL_BLOCK
    assert n_past <= SEL_PAD and n_past + 1 > SEL_TOPN and past_len % SEL_BLOCK == 0
    assert db % SEQ_PER_STEP == 0
    o_cmp, picked = pl.pallas_call(
        functools.partial(_sample_cmp_kernel, past_len=past_len),
        grid=(db // SEQ_PER_STEP,),
        in_specs=[pl.BlockSpec((SEQ_PER_STEP, NSA_HEADS, LANES), lambda b: (b, 0, 0)),
                  pl.BlockSpec((1, 1, SEQ_PER_STEP, ncp, LANES), lambda b: (layer, 0, b, 0, 0)),
                  pl.BlockSpec((1, 1, SEQ_PER_STEP, ncp, LANES), lambda b: (layer, 1, b, 0, 0)),
                  _const_spec(ov.shape)],
        out_specs=[pl.BlockSpec((SEQ_PER_STEP, NSA_HEADS, LANES), lambda b: (b, 0, 0)),
                   pl.BlockSpec((1, SUBLANES, LANES), lambda b: (b, 0, 0))],
        out_shape=[jax.ShapeDtypeStruct((db, NSA_HEADS, LANES), F32),
                   jax.ShapeDtypeStruct((db // SEQ_PER_STEP, SUBLANES, LANES), jnp.int32)],
        compiler_params=_cparams("parallel"),
    )(q8, cmp_past, cmp_past, ov)
    return o_cmp, picked.reshape(db, KV_GROUPS, LANES)[:, :, :SEL_TOPN]


def _sample_attn_kernel(idx_ref, pt_ref, *refs):
    del pt_ref
    kpage = refs[:N_PICK]
    vpage = refs[N_PICK:2 * N_PICK]
    q_ref, kv_ref, gd_ref, ocmp_ref, win_ref, o_ref = refs[2 * N_PICK:]
    b = pl.program_id(0)
    g = pl.program_id(1)
    q = q_ref[0]
    qf = q.astype(F32)
    kvn = kv_ref[0]
    lane = lax.broadcasted_iota(jnp.int32, (1, LANES), 1)
    chan = lax.broadcasted_iota(jnp.int32, (LANES, 1), 0)
    tok_half = lax.broadcasted_iota(jnp.int32, (1, PAGE_SIZE), 1) // SEL_BLOCK

    def branch(gg, keys_t, vals_t, tok_masks, j_new):
        own = (lane >= gg * HEAD_DIM) & (lane < (gg + 1) * HEAD_DIM)
        own_t = (chan >= gg * HEAD_DIM) & (chan < (gg + 1) * HEAD_DIM)
        ss = []
        for kt, tm in zip(keys_t, tok_masks):
            s = jnp.dot(q, kt.astype(BF16), preferred_element_type=F32)
            ss.append(s if tm is None else jnp.where(tm, s, NEG_INF))
        k_new = kvn[:, j_new * LANES:(j_new + 1) * LANES].astype(BF16).astype(F32)
        s_new = jnp.sum(qf * k_new, axis=-1, keepdims=True)
        m = s_new
        for s in ss:
            m = jnp.maximum(m, jnp.max(s, axis=-1, keepdims=True))
        v_new = jnp.where(own, kvn[:, (j_new + 1) * LANES:(j_new + 2) * LANES], 1.0).astype(BF16).astype(F32)
        acc = jnp.exp(s_new - m).astype(BF16).astype(F32) * v_new
        for s, vt in zip(ss, vals_t):
            v_one_t = jnp.where(own_t, vt, 1.0).astype(BF16)
            acc = acc + lax.dot_general(jnp.exp(s - m).astype(BF16), v_one_t, NT_DIMS, preferred_element_type=F32)
        return acc

    gd = gd_ref[0]
    ocmp = ocmp_ref[0]
    for gg in range(KV_GROUPS):
        @pl.when(g == gg)
        def _(gg=gg):
            den = (1 - gg) * HEAD_DIM
            masks = [tok_half == idx_ref[b, gg, j] % BLOCKS_PER_PAGE for j in range(N_PICK)]
            acc = branch(gg, [r[...] for r in kpage], [r[...] for r in vpage], masks, 2)
            o_slc = acc / acc[:, den:den + 1]
            acc = branch(gg, [win_ref[0]], [win_ref[1]], [None], 4)
            o_win = acc / acc[:, den:den + 1]
            for r in range(NSA_REP):
                hh = NSA_REP * gg + r
                mix = (gd[:, 3 * hh:3 * hh + 1] * ocmp[hh:hh + 1] + gd[:, 3 * hh + 1:3 * hh + 2] * o_slc[hh:hh + 1]
                       + gd[:, 3 * hh + 2:3 * hh + 3] * o_win[hh:hh + 1])
                o_ref[0, :, hh * HEAD_DIM:(hh + 1) * HEAD_DIM] = mix[:, gg * HEAD_DIM:(gg + 1) * HEAD_DIM]


def sample_attend(q8, kv, gd, ocmp, idx, page_table, cache_t, win_t, layer):
    db = q8.shape[0]
    page_spec = lambda which, j: pl.BlockSpec(
        (None, None, None, LANES, PAGE_SIZE),
        lambda b, g, ix, pt, j=j: (pt[b, ix[b, g, j] // BLOCKS_PER_PAGE], layer, which, 0, 0))
    per_b = lambda shape: pl.BlockSpec((1,) + shape, lambda b, g, ix, pt: (b,) + (0,) * len(shape))
    wlen = win_t.shape[4]
    grid_spec = pltpu.PrefetchScalarGridSpec(
        num_scalar_prefetch=2,
        grid=(db, KV_GROUPS),
        in_specs=[page_spec(2, j) for j in range(N_PICK)] + [page_spec(3, j) for j in range(N_PICK)]
                 + [per_b((NSA_HEADS, LANES)), per_b((1, 6 * LANES)), per_b((1, LANES)), per_b((NSA_HEADS, LANES)),
                    pl.BlockSpec((None, None, 2, LANES, wlen), lambda b, g, ix, pt: (b, layer, 0, 0, 0))],
        out_specs=per_b((1, NSA_HEADS * HEAD_DIM)),
    )
    return pl.pallas_call(
        _sample_attn_kernel,
        grid_spec=grid_spec,
        out_shape=jax.ShapeDtypeStruct((db, 1, NSA_HEADS * HEAD_DIM), F32),
        compiler_params=_cparams("parallel", "arbitrary"),
    )(idx, page_table, *([cache_t] * (2 * N_PICK)), q8, kv, gd, ocmp, win_t)


def _sample_state_kernel(u_ref, xbc_ref, z_ref, gd_ref, pool_ref, conv_ref, h_ref,
                         pw_ref, ps_ref, cw_ref, cb_ref, a_ref, dsk_ref, ng_ref,
                         ypool_ref, yssm_ref, npool_ref, nconv_ref, nh_ref):
    pd = SSM_HEADS * HEAD_DIM
    u = u_ref[0]
    ext = jnp.concatenate([pool_ref[...], u], axis=0)
    row = lax.broadcasted_iota(jnp.int32, ext.shape, 0)
    lane = lax.broadcasted_iota(jnp.int32, (1, ext.shape[1]), 1)
    d = jnp.zeros_like(u)
    for gi, win in enumerate(POOL_WINDOWS):
        s = jnp.sum(jnp.where(row >= POOL_BUF + 1 - win, ext, 0.0), axis=0, keepdims=True)
        d = jnp.where((lane >= gi * HEAD_DIM) & (lane < (gi + 1) * HEAD_DIM), s / float(win) - u, d)
    d8 = jnp.concatenate([d, jnp.zeros((SUBLANES - 1, d.shape[1]), F32)], axis=0).astype(BF16)
    ypool_ref[0] = jnp.dot(d8, pw_ref[...], preferred_element_type=F32)[0:1] * ps_ref[...]
    npool_ref[...] = ext[1:]

    xbc = xbc_ref[0]
    extc = jnp.concatenate([conv_ref[...], xbc], axis=0)
    act = _silu(jnp.sum(extc * cw_ref[...], axis=0, keepdims=True) + cb_ref[...])
    nconv_ref[...] = extc[1:]
    gd = gd_ref[0]
    dec_row = jnp.exp(gd * a_ref[...])
    eye = (lax.broadcasted_iota(jnp.int32, (HEAD_DIM, HEAD_DIM), 0)
           == lax.broadcasted_iota(jnp.int32, (HEAD_DIM, HEAD_DIM), 1))
    ys = []
    for hh in range(SSM_HEADS):
        g = hh // (SSM_HEADS // SSM_GROUPS)
        x_row = act[:, hh * HEAD_DIM:(hh + 1) * HEAD_DIM]
        dt = gd[:, DT_LANE0 + hh:DT_LANE0 + hh + 1]
        xdt_col = jnp.sum(jnp.where(eye, x_row * dt, 0.0), axis=1, keepdims=True)
        b_row = act[:, pd + g * SSM_STATE:pd + (g + 1) * SSM_STATE]
        c_row = act[:, pd + (SSM_GROUPS + g) * SSM_STATE:pd + (SSM_GROUPS + g + 1) * SSM_STATE]
        h_new = dec_row[:, DT_LANE0 + hh:DT_LANE0 + hh + 1] * h_ref[hh] + xdt_col * b_row
        nh_ref[hh] = h_new
        y_col = jnp.sum(h_new * c_row, axis=1, keepdims=True)
        y_row = jnp.sum(jnp.where(eye, y_col, 0.0), axis=0, keepdims=True)
        ys.append(y_row + dsk_ref[:, hh * HEAD_DIM:(hh + 1) * HEAD_DIM] * x_row)
    y = jnp.concatenate(ys, axis=1) * _silu(z_ref[0])
    outs = []
    for g in range(SSM_GROUPS):
        yg = y[:, g * SSM_STATE:(g + 1) * SSM_STATE]
        ms = jnp.mean(yg * yg, axis=-1, keepdims=True)
        outs.append(yg * lax.rsqrt(ms + EPS) * ng_ref[:, g * SSM_STATE:(g + 1) * SSM_STATE])
    yssm_ref[0] = jnp.concatenate(outs, axis=1)


def sample_state_mixers(u, xbc, z, gd, state_pool, state_conv, state_ssm, layer, lw):
    db = u.shape[0]
    per_b = lambda shape: pl.BlockSpec((1,) + shape, lambda b: (b,) + (0,) * len(shape))
    st = lambda shape: pl.BlockSpec((None, None) + shape, lambda b: (b, layer) + (0,) * len(shape))
    new = lambda shape: pl.BlockSpec((None,) + shape, lambda b: (b,) + (0,) * len(shape))
    ps, cs, hs = state_pool.shape[2:], state_conv.shape[2:], state_ssm.shape[2:]
    params = [lw['pool_w'], lw['pool_scale'], lw['conv_w'], lw['conv_b'], lw['a_row'], lw['d_skip'], lw['ssm_norm']]
    return pl.pallas_call(
        _sample_state_kernel,
        grid=(db,),
        in_specs=[per_b(u.shape[1:]), per_b(xbc.shape[1:]), per_b(z.shape[1:]), per_b(gd.shape[1:]),
                  st(ps), st(cs), st(hs)] + [_const_spec(p.shape) for p in params],
        out_specs=[per_b((1, ps[1])), per_b((1, z.shape[2])), new(ps), new(cs), new(hs)],
        out_shape=[jax.ShapeDtypeStruct((db, 1, ps[1]), F32), jax.ShapeDtypeStruct((db, 1, z.shape[2]), F32),
                   jax.ShapeDtypeStruct((db,) + ps, F32), jax.ShapeDtypeStruct((db,) + cs, F32),
                   jax.ShapeDtypeStruct((db,) + hs, F32)],
        compiler_params=_cparams("parallel"),
    )(u, xbc, z, gd, state_pool, state_conv, state_ssm, *params)


def _prep_layer(p, i):
    d = p['w_in'].shape[1]
    w_in = p['w_in'][i]
    o = np.cumsum([0, 256, 512, 768, GATE_LANES, 256, 768, SSM_HEADS])
    w_q = w_in[:, o[1]:o[2]].reshape(d, NSA_HEADS, HEAD_DIM)
    slot = jnp.zeros((d, NSA_HEADS, KV_GROUPS, HEAD_DIM), F32)
    for hh in range(NSA_HEADS):
        slot = slot.at[:, hh, hh // NSA_REP].set(w_q[:, hh])
    w_gd = jnp.zeros((d, LANES), F32)
    w_gd = w_gd.at[:, :GATE_LANES].set(w_in[:, o[3]:o[4]])
    w_gd = w_gd.at[:, DT_LANE0:DT_LANE0 + SSM_HEADS].set(w_in[:, o[6]:o[7]])
    lane_pad = lambda v: jnp.zeros((1, LANES), F32).at[0, DT_LANE0:DT_LANE0 + SSM_HEADS].set(v)
    eye_g = jnp.eye(KV_GROUPS, dtype=F32)
    ratio = CMP_BLOCK // CMP_STRIDE
    cmp_w, cmp_pe = [], []
    for j in range(2):
        w4 = p['nsa_cmp_w'][i, j].reshape(ratio, CMP_STRIDE, HEAD_DIM, HEAD_DIM)
        cmp_w.append(jnp.einsum('rjde,gh->jgdrhe', w4, eye_g).reshape(CMP_STRIDE * 2 * HEAD_DIM, ratio * LANES))
        pe4 = p['nsa_cmp_pe'][i, j].reshape(ratio, CMP_STRIDE, 1, HEAD_DIM)
        pe_rows = jnp.broadcast_to(pe4, (ratio, CMP_STRIDE, KV_GROUPS, HEAD_DIM)).reshape(ratio, -1)
        cmp_pe.append(jnp.zeros((SUBLANES, pe_rows.shape[1]), F32).at[:ratio].set(pe_rows))
    pool_w = jnp.zeros((256, 256), F32)
    for gi in range(len(POOL_WINDOWS)):
        sl = slice(gi * HEAD_DIM, (gi + 1) * HEAD_DIM)
        pool_w = pool_w.at[sl, sl].set(p['pool_w'][i, gi])
    w_out = p['w_out'][i]
    bf = lambda x: x.astype(BF16)
    row = lambda x: x.reshape(1, -1)
    return {
        'ffn1_norm': row(p['ffn1_norm'][i]), 'ffn1_w_gate': bf(p['ffn1_w_gate'][i]),
        'ffn1_w_up': bf(p['ffn1_w_up'][i]), 'ffn1_w_down': bf(p['ffn1_w_down'][i]),
        'ffn2_norm': row(p['ffn2_norm'][i]), 'ffn2_w_gate': bf(p['ffn2_w_gate'][i]),
        'ffn2_w_up': bf(p['ffn2_w_up'][i]), 'ffn2_w_down': bf(p['ffn2_w_down'][i]),
        'mix_norm': row(p['mix_norm'][i]),
        'w_u': bf(w_in[:, o[0]:o[1]]), 'w_q': bf(slot.reshape(d, NSA_HEADS * LANES)),
        'w_kv': bf(w_in[:, o[2]:o[3]]), 'w_gd': bf(w_gd), 'w_z': bf(w_in[:, o[4]:o[5]]),
        'w_xbc': bf(w_in[:, o[5]:o[6]]),
        'q_norm': row(jnp.tile(p['nsa_q_norm'][i], 2)), 'k_norm': jnp.tile(p['nsa_k_norm'][i], (1, 2)),
        'dt_bias': lane_pad(p['ssm_dt_bias'][i]),
        'w_out_pool': bf(w_out[:256]), 'w_out_nsa': bf(w_out[256:768]), 'w_out_ssm': bf(w_out[768:]),
        'pool_w': bf(pool_w), 'pool_scale': row(p['pool_scale'][i]),
        'cmp_w': bf(jnp.stack(cmp_w)), 'cmp_pe': bf(jnp.stack(cmp_pe)),
        'conv_w': p['ssm_conv_w'][i], 'conv_b': row(p['ssm_conv_b'][i]),
        'a_row': lane_pad(-jnp.exp(p['ssm_a_log'][i])),
        'd_skip': row(jnp.repeat(p['ssm_d'][i], HEAD_DIM)), 'ssm_norm': row(p['ssm_norm'][i]),
        'ple_norm': row(p['ple_norm'][i]), 'ple_w_gate': bf(p['ple_w_gate'][i]),
        'ple_w_proj': bf(p['ple_w_proj'][i]),
    }


def _rope_tables(pos):
    half = HEAD_DIM // 2
    inv = ROPE_THETA ** (-jnp.arange(half, dtype=F32) / half)
    ang = pos.astype(F32)[:, None] * inv[None, :]
    cos = jnp.cos(ang)
    sin = jnp.sin(ang)
    return jnp.tile(cos, (1, 4)), jnp.tile(jnp.concatenate([-sin, sin], axis=1), (1, 2))


def _selection_constants(t):
    nseg = t // CMP_STRIDE
    nc = nseg - CMP_BLOCK // CMP_STRIDE + 1
    ns = t // SEL_BLOCK
    c_start = np.arange(nseg) * CMP_STRIDE
    s_start = np.arange(SEL_PAD) * SEL_BLOCK
    ovt = ((c_start[None, :] < s_start[:, None] + SEL_BLOCK) & (c_start[None, :] + CMP_BLOCK > s_start[:, None])
           & (np.arange(nseg)[None, :] < nc) & (np.arange(SEL_PAD)[:, None] < ns))
    onehot = (np.arange(t)[:, None] // SEL_BLOCK) == np.arange(SEL_PAD)[None, :]
    return jnp.asarray(onehot, BF16), jnp.asarray(ovt, BF16)


def _channel_major(cache):
    nd = cache.ndim
    t = jnp.transpose(cache, tuple(range(nd - 3)) + (nd - 2, nd - 1, nd - 3))
    return t.reshape(cache.shape[:nd - 3] + (cache.shape[-2] * cache.shape[-1], cache.shape[-3]))


def _token_layer_front(h, lw, cos, sin, seq_len=None):
    h = ffn_halfstep(h, lw['ffn1_norm'], lw['ffn1_w_gate'], lw['ffn1_w_up'], lw['ffn1_w_down'])
    return (h,) + tuple(in_projection(h, lw, cos, sin, seq_len))


def _token_layer_back(h, y_pool, y_nsa, y_ssm, pe, lw):
    h = out_projection(h, y_pool, y_nsa, y_ssm, lw)
    h = ffn_halfstep(h, lw['ffn2_norm'], lw['ffn2_w_gate'], lw['ffn2_w_up'], lw['ffn2_w_down'])
    return ple_step(h, pe, lw)


def kernel(x_prompt, x_sample, cache_nsa_kv, cache_win_kv, state_pool, state_conv, state_ssm, page_table,
           p_prompt, p_sample, ffn1_norm, ffn1_w_gate, ffn1_w_up, ffn1_w_down, mix_norm, w_in, w_out,
           pool_w, pool_scale, nsa_q_norm, nsa_k_norm, nsa_cmp_pe, nsa_cmp_w, ssm_conv_w, ssm_conv_b,
           ssm_dt_bias, ssm_a_log, ssm_d, ssm_norm, ffn2_norm, ffn2_w_gate, ffn2_w_up, ffn2_w_down,
           ple_norm, ple_w_gate, ple_w_proj):
    params = dict(ffn1_norm=ffn1_norm, ffn1_w_gate=ffn1_w_gate, ffn1_w_up=ffn1_w_up, ffn1_w_down=ffn1_w_down,
                  mix_norm=mix_norm, w_in=w_in, w_out=w_out, pool_w=pool_w, pool_scale=pool_scale,
                  nsa_q_norm=nsa_q_norm, nsa_k_norm=nsa_k_norm, nsa_cmp_pe=nsa_cmp_pe, nsa_cmp_w=nsa_cmp_w,
                  ssm_conv_w=ssm_conv_w, ssm_conv_b=ssm_conv_b, ssm_dt_bias=ssm_dt_bias, ssm_a_log=ssm_a_log,
                  ssm_d=ssm_d, ssm_norm=ssm_norm, ffn2_norm=ffn2_norm, ffn2_w_gate=ffn2_w_gate,
                  ffn2_w_up=ffn2_w_up, ffn2_w_down=ffn2_w_down, ple_norm=ple_norm, ple_w_gate=ple_w_gate,
                  ple_w_proj=ple_w_proj)
    depth = w_in.shape[0]
    b, t, d = x_prompt.shape
    db = x_sample.shape[0]
    past_len = page_table.shape[1] * PAGE_SIZE
    wkeep = min(WINDOW, t)

    cos_p, sin_p = _rope_tables(jnp.tile(jnp.arange(t, dtype=jnp.int32), b))
    cos_s, sin_s = _rope_tables(jnp.full((db,), past_len, jnp.int32))
    onehot, ovt = _selection_constants(t)

    lws = [_prep_layer(params, i) for i in range(depth)]
    cache_t, win_t = _channel_major(cache_nsa_kv), _channel_major(cache_win_kv)
    cmp_past = past_compress(cache_t, page_table, jnp.stack([lw['cmp_w'] for lw in lws]),
                             jnp.stack([lw['cmp_pe'] for lw in lws]))
    ov_past = _selection_constants(past_len)[1].T
    s3 = lambda x: x.reshape(db, 1, x.shape[-1])

    h_p = x_prompt.reshape(b * t, d)
    h_s = x_sample.reshape(db, d)
    st_p = [[] for _ in range(5)]
    st_s = [[] for _ in range(5)]
    for i in range(depth):
        lw = lws[i]
        h_p, u, qp, rows, wins, kvb, seg, gd, z, xbc = _token_layer_front(h_p, lw, cos_p, sin_p, t)
        r3 = lambda x: x.reshape(b, t, x.shape[-1])
        y_pool = pool_prompt(r3(u), lw)
        cmp = nsa_compress(seg.reshape(2, b, t // CMP_STRIDE, CMP_STRIDE * LANES), lw['cmp_w'], lw['cmp_pe'])
        y_nsa = nsa_prompt(r3(qp), r3(gd), cmp, r3(kvb), onehot, ovt)
        y_ssm, h_fin = ssd_prompt(r3(xbc), r3(z), r3(gd), lw)
        h_p = _token_layer_back(h_p, y_pool.reshape(b * t, -1), y_nsa.reshape(b * t, -1),
                                y_ssm.reshape(b * t, -1), p_prompt[i].reshape(b * t, -1), lw)
        st_p[0].append(rows.reshape(b, 4, t, KV_GROUPS, HEAD_DIM))
        st_p[1].append(wins[:, :, t - wkeep:].reshape(b, 2, wkeep, KV_GROUPS, HEAD_DIM))
        st_p[2].append(r3(u)[:, t - POOL_BUF:])
        st_p[3].append(r3(xbc)[:, t - (SSM_CONV - 1):])
        st_p[4].append(h_fin)
        h_s, u, qp, kv, gd, z, xbc = _token_layer_front(h_s, lw, cos_s, sin_s)
        q8 = qp.reshape(db, NSA_HEADS, LANES)
        o_cmp, picked = sample_cmp_select(q8, cmp_past, i, ov_past, past_len)
        y_nsa = sample_attend(q8, s3(kv), s3(gd), o_cmp, picked, page_table, cache_t, win_t, i)
        y_pool, y_ssm, new_pool, new_conv, new_h = sample_state_mixers(
            s3(u), s3(xbc), s3(z), s3(gd), state_pool, state_conv, state_ssm, i, lw)
        h_s = _token_layer_back(h_s, y_pool.reshape(db, -1), y_nsa.reshape(db, -1), y_ssm.reshape(db, -1),
                                p_sample[i].reshape(db, -1), lw)
        kv6 = kv.reshape(db, 6, 1, KV_GROUPS, HEAD_DIM)
        rows = kv6[:, 0:4]
        new_win = jnp.concatenate([cache_win_kv[:, i, :, 1:], kv6[:, 4:6]], axis=2)
        for j, v in enumerate((rows, new_win, new_pool, new_conv, new_h)):
            st_s[j].append(v)
    outs = [h_p.reshape(b, t, d), h_s.reshape(db, 1, d)]
    for j in range(5):
        outs.append(jnp.stack(st_p[j], axis=1))
        outs.append(jnp.stack(st_s[j], axis=1))
    return tuple(outs)
```

```python
import functools

import jax
import jax.numpy as jnp
import numpy as np
from jax import lax
from jax.experimental import pallas as pl
from jax.experimental.pallas import tpu as pltpu

F32 = jnp.float32
BF16 = jnp.bfloat16

POOL_WINDOWS = (2, 4, 8, 16)
POOL_BUF = 15
HEAD_DIM = 64
NSA_HEADS = 8
KV_GROUPS = 2
NSA_REP = NSA_HEADS // KV_GROUPS
CMP_BLOCK = 32
CMP_STRIDE = 16
SEL_BLOCK = 64
SEL_TOPN = 16
WINDOW = 512
SEL_BONUS = 1.0e4
NEG_INF = -1.0e30
SSM_HEADS = 4
SSM_GROUPS = 2
SSM_STATE = 128
SSM_CONV = 4
SSM_CHUNK = 128
ROPE_THETA = 10000.0
EPS = 1e-6
PAGE_SIZE = 128

LANES = 128
SUBLANES = 8
VMEM_LIMIT_BYTES = 56 * 1024 * 1024

GATE_LANES = 3 * NSA_HEADS
DT_LANE0 = 32

NT_DIMS = (((1,), (1,)), ((), ()))
TN_DIMS = (((0,), (0,)), ((), ()))


def _cparams(*sem):
    return pltpu.CompilerParams(dimension_semantics=sem, vmem_limit_bytes=VMEM_LIMIT_BYTES)


def _const_spec(shape):
    nd = len(shape)
    return pl.BlockSpec(shape, lambda *_: (0,) * nd)


def _rmsnorm(x, g):
    ms = jnp.mean(x * x, axis=-1, keepdims=True)
    return x * lax.rsqrt(ms + EPS) * g


def _silu(x):
    return x * jax.nn.sigmoid(x)


def _split3(x):
    hi = x.astype(BF16)
    r = x - hi.astype(F32)
    mid = r.astype(BF16)
    lo = (r - mid.astype(F32)).astype(BF16)
    return hi, mid, lo


def _ffn_kernel(x_ref, g_ref, wg_ref, wu_ref, wd_ref, o_ref, *, n_chunks):
    x = x_ref[...]
    xn = _rmsnorm(x, g_ref[...]).astype(BF16)
    fc = wg_ref.shape[1] // n_chunks
    tot = None
    for c in range(n_chunks):
        sl = slice(c * fc, (c + 1) * fc)
        g = jnp.dot(xn, wg_ref[:, sl], preferred_element_type=F32)
        u = jnp.dot(xn, wu_ref[:, sl], preferred_element_type=F32)
        a = (_silu(g) * u).astype(BF16)
        d = jnp.dot(a, wd_ref[sl, :], preferred_element_type=F32)
        tot = d if tot is None else tot + d
    o_ref[...] = x + 0.5 * tot


def _row_tile(m, pref):
    return pref if m % pref == 0 else m


def _resident_spec(shape):
    nd = len(shape)
    return pl.BlockSpec(shape, lambda *_: (0,) * nd, pipeline_mode=pl.Buffered(1))


def ffn_halfstep(h, g, wg, wu, wd):
    m, d = h.shape
    f = wg.shape[1]
    tm = _row_tile(m, 1024)
    return pl.pallas_call(
        functools.partial(_ffn_kernel, n_chunks=2 if tm < 1024 else f // (2 * LANES)),
        grid=(m // tm,),
        in_specs=[pl.BlockSpec((tm, d), lambda i: (i, 0)),
                  _const_spec((1, d)), _resident_spec((d, f)), _resident_spec((d, f)), _resident_spec((f, d))],
        out_specs=pl.BlockSpec((tm, d), lambda i: (i, 0)),
        out_shape=jax.ShapeDtypeStruct((m, d), F32),
        compiler_params=_cparams("parallel"),
    )(h, g, wg, wu, wd)


def _rope128(x, cos, sin_signed, lane):
    rot = jnp.where((lane & 32) == 0, pltpu.roll(x, 96, 1), pltpu.roll(x, 32, 1))
    return x * cos + rot * sin_signed


def _inproj_kernel(x_ref, g_ref, wu_ref, wq_ref, wkv_ref, wgd_ref, wz_ref, wx_ref,
                   qn_ref, kn_ref, dtb_ref, cos_ref, sin_ref, *out_refs, prompt):
    if prompt:
        u_ref, q_ref, rows_ref, wins_ref, kvb_ref, seg_ref, gd_ref, z_ref, xbc_ref, seg_scr = out_refs
    else:
        u_ref, q_ref, kv_ref, gd_ref, z_ref, xbc_ref = out_refs
    x = x_ref[...]
    tm = x.shape[0]
    xn = _rmsnorm(x, g_ref[...]).astype(BF16)
    cos = cos_ref[...]
    sin = sin_ref[...]
    lane = lax.broadcasted_iota(jnp.int32, (tm, LANES), 1)
    inv_hd = 1.0 / HEAD_DIM

    u_ref[...] = jnp.dot(xn, wu_ref[...], preferred_element_type=F32)
    z_ref[...] = jnp.dot(xn, wz_ref[...], preferred_element_type=F32)
    xbc_ref[...] = jnp.dot(xn, wx_ref[...], preferred_element_type=F32)

    y = jnp.dot(xn, wgd_ref[...], preferred_element_type=F32)
    yd = y + dtb_ref[...]
    softplus = jnp.maximum(yd, 0.0) + jnp.log1p(jnp.exp(-jnp.abs(yd)))
    gd_ref[...] = jnp.where(lane < DT_LANE0, jax.nn.sigmoid(y), softplus)

    q = jnp.dot(xn, wq_ref[...], preferred_element_type=F32)
    qn = qn_ref[...]
    for hh in range(NSA_HEADS):
        s = q[:, hh * LANES:(hh + 1) * LANES]
        ms = jnp.sum(s * s, axis=-1, keepdims=True) * inv_hd
        s = s * lax.rsqrt(ms + EPS) * qn
        s = _rope128(s, cos, sin, lane) * (HEAD_DIM ** -0.5)
        q_ref[:, hh * LANES:(hh + 1) * LANES] = s.astype(BF16)

    kv = jnp.dot(xn, wkv_ref[...], preferred_element_type=F32)
    low = lane < HEAD_DIM
    for j in range(6):
        s = kv[:, j * LANES:(j + 1) * LANES]
        if j % 2 == 0:
            sq = s * s
            s_all = jnp.sum(sq, axis=-1, keepdims=True)
            s_low = jnp.sum(jnp.where(low, sq, 0.0), axis=-1, keepdims=True)
            ms = jnp.where(low, s_low, s_all - s_low) * inv_hd
            s = s * lax.rsqrt(ms + EPS) * kn_ref[j // 2:j // 2 + 1, :]
            s = _rope128(s, cos, sin, lane)
        if not prompt:
            kv_ref[:, j * LANES:(j + 1) * LANES] = s
            continue
        if j < 4:
            rows_ref[0, j] = s
        else:
            wins_ref[0, j - 4] = s
        if j >= 2:
            kvb_ref[:, (j - 2) * LANES:(j - 1) * LANES] = s.astype(BF16)
        else:
            seg_scr[j] = s
            seg_ref[j] = jnp.concatenate(
                [seg_scr[j, pl.ds(jj, tm // CMP_STRIDE, stride=CMP_STRIDE), :].astype(BF16)
                 for jj in range(CMP_STRIDE)], axis=1)


def in_projection(h, lw, cos, sin, seq_len=None):
    m, d = h.shape
    tm = _row_tile(m, 512)
    row = lambda n: pl.BlockSpec((tm, n), lambda i: (i, 0))
    ws = [lw["w_u"], lw["w_q"], lw["w_kv"], lw["w_gd"], lw["w_z"], lw["w_xbc"]]
    tail = [(LANES, F32), (256, F32), (768, F32)]
    out_specs = [row(256), row(NSA_HEADS * LANES)]
    out_shape = [jax.ShapeDtypeStruct((m, 256), F32), jax.ShapeDtypeStruct((m, NSA_HEADS * LANES), BF16)]
    scratch = []
    if seq_len is None:
        out_specs.append(row(6 * LANES))
        out_shape.append(jax.ShapeDtypeStruct((m, 6 * LANES), F32))
    else:
        assert seq_len % tm == 0 and tm % (2 * SUBLANES * CMP_STRIDE) == 0
        tps = seq_len // tm
        per_seq = lambda n: pl.BlockSpec((1, n, tm, LANES), lambda i: (i // tps, 0, i % tps, 0))
        out_specs += [per_seq(4), per_seq(2), row(4 * LANES),
                      pl.BlockSpec((2, tm // CMP_STRIDE, CMP_STRIDE * LANES), lambda i: (0, i, 0))]
        out_shape += [jax.ShapeDtypeStruct((m // seq_len, 4, seq_len, LANES), F32),
                      jax.ShapeDtypeStruct((m // seq_len, 2, seq_len, LANES), F32),
                      jax.ShapeDtypeStruct((m, 4 * LANES), BF16),
                      jax.ShapeDtypeStruct((2, m // CMP_STRIDE, CMP_STRIDE * LANES), BF16)]
        scratch = [pltpu.VMEM((2, tm, LANES), F32)]
    out_specs += [row(n) for n, _ in tail]
    out_shape += [jax.ShapeDtypeStruct((m, n), dt) for n, dt in tail]
    return pl.pallas_call(
        functools.partial(_inproj_kernel, prompt=seq_len is not None),
        grid=(m // tm,),
        in_specs=[row(d), _const_spec((1, d))] + [_resident_spec(w.shape) for w in ws]
                 + [_const_spec((1, LANES)), _const_spec((3, LANES)), _const_spec((1, LANES)),
                    row(LANES), row(LANES)],
        out_specs=out_specs,
        out_shape=out_shape,
        scratch_shapes=scratch,
        compiler_params=_cparams("parallel"),
    )(h, lw["mix_norm"], *ws, lw["q_norm"], lw["k_norm"], lw["dt_bias"], cos, sin)


def _outproj_kernel(h_ref, yp_ref, yn_ref, ys_ref, wp_ref, wn_ref, ws_ref, o_ref):
    acc = jnp.dot(yp_ref[...].astype(BF16), wp_ref[...], preferred_element_type=F32)
    acc = acc + jnp.dot(yn_ref[...].astype(BF16), wn_ref[...], preferred_element_type=F32)
    acc = acc + jnp.dot(ys_ref[...].astype(BF16), ws_ref[...], preferred_element_type=F32)
    o_ref[...] = h_ref[...] + acc


def out_projection(h, y_pool, y_nsa, y_ssm, lw):
    m, d = h.shape
    tm = _row_tile(m, 1024)
    row = lambda n: pl.BlockSpec((tm, n), lambda i: (i, 0))
    ws = [lw["w_out_pool"], lw["w_out_nsa"], lw["w_out_ssm"]]
    return pl.pallas_call(
        _outproj_kernel,
        grid=(m // tm,),
        in_specs=[row(d), row(y_pool.shape[1]), row(y_nsa.shape[1]), row(y_ssm.shape[1])]
                 + [_resident_spec(w.shape) for w in ws],
        out_specs=row(d),
        out_shape=jax.ShapeDtypeStruct((m, d), F32),
        compiler_params=_cparams("parallel"),
    )(h, y_pool, y_nsa, y_ssm, *ws)


def _ple_kernel(h_ref, pe_ref, g_ref, wg_ref, wp_ref, o_ref):
    h = h_ref[...]
    xn = _rmsnorm(h, g_ref[...]).astype(BF16)
    gate = jax.nn.sigmoid(jnp.dot(xn, wg_ref[...], preferred_element_type=F32))
    proj = jnp.dot(pe_ref[...].astype(BF16), wp_ref[...], preferred_element_type=F32)
    o_ref[...] = h + gate * proj


def ple_step(h, pe, lw):
    m, d = h.shape
    tm = _row_tile(m, 1024)
    row = lambda n: pl.BlockSpec((tm, n), lambda i: (i, 0))
    return pl.pallas_call(
        _ple_kernel,
        grid=(m // tm,),
        in_specs=[row(d), row(pe.shape[1]), _const_spec((1, d)),
                  _resident_spec(lw["ple_w_gate"].shape), _resident_spec(lw["ple_w_proj"].shape)],
        out_specs=row(d),
        out_shape=jax.ShapeDtypeStruct((m, d), F32),
        compiler_params=_cparams("parallel"),
    )(h, pe, lw["ple_norm"], lw["ple_w_gate"], lw["ple_w_proj"])


POOL_HALO = 2 * SUBLANES


def _pool_kernel(u_ref, w_ref, sc_ref, y_ref, ext_ref, *, tm):
    c = pl.program_id(1)

    @pl.when(c == 0)
    def _():
        ext_ref[0:POOL_HALO, :] = jnp.zeros((POOL_HALO, ext_ref.shape[1]), F32)

    ext_ref[POOL_HALO:POOL_HALO + tm, :] = u_ref[0]
    lane = lax.broadcasted_iota(jnp.int32, (tm, LANES), 1)
    low = lane < HEAD_DIM
    pos1 = c * tm + lax.broadcasted_iota(jnp.int32, (tm, LANES), 0) + 1
    ds = []
    for slab, (w_lo, w_hi) in enumerate(((POOL_WINDOWS[0], POOL_WINDOWS[1]), (POOL_WINDOWS[2], POOL_WINDOWS[3]))):
        cols = slice(slab * LANES, (slab + 1) * LANES)
        x = ext_ref[POOL_HALO:POOL_HALO + tm, cols]
        run = x
        s_lo = None
        for k in range(1, w_hi):
            run = run + ext_ref[POOL_HALO - k:POOL_HALO - k + tm, cols]
            if k == w_lo - 1:
                s_lo = run
        cnt = jnp.where(low, jnp.minimum(pos1, w_lo), jnp.minimum(pos1, w_hi)).astype(F32)
        ds.append(jnp.where(low, s_lo, run) / cnt - x)
    d = jnp.concatenate(ds, axis=1).astype(BF16)
    y_ref[0] = jnp.dot(d, w_ref[...], preferred_element_type=F32) * sc_ref[...]
    ext_ref[0:POOL_HALO, :] = ext_ref[tm:tm + POOL_HALO, :]


def pool_prompt(u, lw):
    b, t, ch = u.shape
    tm = _row_tile(t, 512)
    return pl.pallas_call(
        functools.partial(_pool_kernel, tm=tm),
        grid=(b, t // tm),
        in_specs=[pl.BlockSpec((1, tm, ch), lambda i, j: (i, j, 0)),
                  _const_spec((ch, ch)), _const_spec((1, ch))],
        out_specs=pl.BlockSpec((1, tm, ch), lambda i, j: (i, j, 0)),
        out_shape=jax.ShapeDtypeStruct((b, t, ch), F32),
        scratch_shapes=[pltpu.VMEM((POOL_HALO + tm, ch), F32)],
        compiler_params=_cparams("parallel", "arbitrary"),
    )(u, lw["pool_w"], lw["pool_scale"])


def _compress_kernel(seg_ref, w_ref, pe_ref, o_ref):
    y = jnp.dot(seg_ref[0], w_ref[0], preferred_element_type=F32)
    pe = jnp.dot(pe_ref[0], w_ref[0], preferred_element_type=F32)
    nseg = y.shape[0]
    second = pltpu.roll(y[:, LANES:], nseg - 1, 0)
    row = lax.broadcasted_iota(jnp.int32, (nseg, LANES), 0)
    second = jnp.where(row < nseg - 1, second, 0.0)
    out = y[:, :LANES] + second + pe[0:1, :LANES] + pe[1:2, LANES:]
    o_ref[0, 0] = out.astype(o_ref.dtype)


def nsa_compress(seg, w, pe):
    _, b, nseg, width = seg.shape
    return pl.pallas_call(
        _compress_kernel,
        grid=(2, b),
        in_specs=[pl.BlockSpec((None, 1, nseg, width), lambda j, i: (j, i, 0, 0)),
                  pl.BlockSpec((1, width, 2 * LANES), lambda j, i: (j, 0, 0)),
                  pl.BlockSpec((1, SUBLANES, width), lambda j, i: (j, 0, 0))],
        out_specs=pl.BlockSpec((1, 1, nseg, LANES), lambda j, i: (j, i, 0, 0)),
        out_shape=jax.ShapeDtypeStruct((2, b, nseg, LANES), BF16),
        compiler_params=_cparams("parallel", "parallel"),
    )(seg, w, pe)


Q_TILE = 256
KEY_CHUNK = 512
SEL_PAD = 128


def _topk_mask_t(scores, n_sel):
    shape = scores[0].shape
    blk = lax.broadcasted_iota(jnp.int32, shape, 0)

    def body(_, carry):
        out = []
        for sc, sel in carry:
            m = jnp.max(sc, axis=0, keepdims=True)
            idx = jnp.min(jnp.where(sc == m, blk, shape[0]), axis=0, keepdims=True)
            hit = blk == idx
            out.append((jnp.where(hit, -jnp.inf, sc), jnp.where(hit, 1.0, sel)))
        return tuple(out)

    done = lax.fori_loop(0, n_sel, body, tuple((sc, jnp.zeros(shape, F32)) for sc in scores))
    return [sel for _, sel in done]


def _softmax_rows(s, mask):
    sm = jnp.where(mask, s, NEG_INF)
    m = jnp.max(sm, axis=-1, keepdims=True)
    e = jnp.exp(sm - m)
    return e / jnp.sum(e, axis=-1, keepdims=True)


def _nsa_kernel(q_ref, gd_ref, kc_ref, vc_ref, ks_ref, vs_ref, kw_ref, vw_ref, e_ref, ovt_ref, o_ref,
                sa_scr, sb_scr, m_scr, acc_scr, *, n_blocks, n_sel, win_len, last_chunk):
    i = pl.program_id(1)
    t0 = i * Q_TILE
    ncp = kc_ref.shape[2]
    qpos = t0 + lax.broadcasted_iota(jnp.int32, (Q_TILE, 1), 0)
    gd = gd_ref[0]

    blk = lax.broadcasted_iota(jnp.int32, (SEL_PAD, Q_TILE), 0)
    qpos_t = t0 + lax.broadcasted_iota(jnp.int32, (SEL_PAD, Q_TILE), 1)
    cur = qpos_t >> 6
    forced = jnp.where(blk == 0, 1.0, jnp.where(blk == cur, 1.0, jnp.where(blk == cur - 1, 1.0, 0.0)))
    valid = blk * SEL_BLOCK <= qpos_t
    real = blk < n_blocks

    n_full = t0 // KEY_CHUNK
    rows = NSA_REP * Q_TILE
    groups = range(KV_GROUPS)
    lane_row = lax.broadcasted_iota(jnp.int32, (1, LANES), 1)
    cend = lax.broadcasted_iota(jnp.int32, (1, ncp), 1) * CMP_STRIDE + (CMP_BLOCK - 1)
    qpos4 = jnp.concatenate([qpos] * NSA_REP, axis=0)
    own = [(lane_row >= g * HEAD_DIM) & (lane_row < (g + 1) * HEAD_DIM) for g in groups]
    den_lane = [(1 - g) * HEAD_DIM for g in groups]
    q_all = [jnp.concatenate([q_ref[0, :, (NSA_REP * g + r) * LANES:(NSA_REP * g + r + 1) * LANES]
                              for r in range(NSA_REP)], axis=0) for g in groups]

    def split_heads(acc):
        return [acc[r * Q_TILE:(r + 1) * Q_TILE] for r in range(NSA_REP)]

    o_cmp, score = [], []
    for g in groups:
        s = lax.dot_general(q_all[g], kc_ref[0, 0], NT_DIMS, preferred_element_type=F32)
        s = jnp.where(cend <= qpos4, s, NEG_INF)
        e = jnp.exp(s - jnp.max(s, axis=-1, keepdims=True))
        has_block = jnp.where(qpos4 >= CMP_BLOCK - 1, 1.0, 0.0)
        p = e * (has_block / jnp.sum(e, axis=-1, keepdims=True))
        o_cmp.append(split_heads(jnp.dot(p.astype(BF16), vc_ref[0, 0], preferred_element_type=F32)))
        psum = p[0:Q_TILE]
        for r in range(1, NSA_REP):
            psum = psum + p[r * Q_TILE:(r + 1) * Q_TILE]
        imp_t = None
        for part in _split3(psum):
            term = lax.dot_general(ovt_ref[...], part, NT_DIMS, preferred_element_type=F32)
            imp_t = term if imp_t is None else imp_t + term
        sc = jnp.where(valid, imp_t + SEL_BONUS * forced, -1.0)
        score.append(jnp.where(real, sc, -jnp.inf))

    sel_t = _topk_mask_t(score, n_sel)
    q_aug = []
    for g in groups:
        bias = jnp.where(sel_t[g].T > 0.5, 0.0, NEG_INF).astype(BF16)
        q_aug.append(jnp.concatenate([jnp.concatenate([bias] * NSA_REP, axis=0), q_all[g]], axis=1))

    def scores(c, g, s_ref):
        k0 = pl.multiple_of(jnp.minimum(c, last_chunk) * KEY_CHUNK, KEY_CHUNK)
        k_aug = jnp.concatenate([e_ref[pl.ds(k0, KEY_CHUNK), :], ks_ref[0, pl.ds(k0, KEY_CHUNK), :]], axis=1)
        s_ref[g] = lax.dot_general(q_aug[g], k_aug, NT_DIMS, preferred_element_type=F32)

    def reduce_chunk(c, g, s_ref, causal):
        k0 = pl.multiple_of(jnp.minimum(c, last_chunk) * KEY_CHUNK, KEY_CHUNK)
        v_one = jnp.where(own[g], vs_ref[0, pl.ds(k0, KEY_CHUNK), :], 1.0).astype(BF16)
        s = s_ref[g]
        if causal:
            kpos = c * KEY_CHUNK + lax.broadcasted_iota(jnp.int32, (1, KEY_CHUNK), 1)
            s = jnp.where(kpos <= qpos4, s, NEG_INF)
        m = m_scr[g]
        m_new = jnp.maximum(m, jnp.max(s, axis=-1, keepdims=True))
        pc = jnp.exp(s - m_new).astype(BF16)
        acc_scr[g] = jnp.exp(m - m_new) * acc_scr[g] + jnp.dot(pc, v_one, preferred_element_type=F32)
        m_scr[g] = m_new

    def pair(p, carry):
        for g in groups:
            scores(2 * p + 1, g, sb_scr)
        for g in groups:
            reduce_chunk(2 * p, g, sa_scr, False)
        for g in groups:
            scores(2 * p + 2, g, sa_scr)
        for g in groups:
            reduce_chunk(2 * p + 1, g, sb_scr, False)
        return carry

    for g in groups:
        m_scr[g] = jnp.full((rows, 1), NEG_INF, F32)
        acc_scr[g] = jnp.zeros((rows, LANES), F32)
        scores(0, g, sa_scr)
    n_pairs = n_full // 2
    lax.fori_loop(0, n_pairs, pair, 0)
    one_more = n_full > 2 * n_pairs

    @pl.when(one_more)
    def _():
        for g in groups:
            scores(2 * n_pairs + 1, g, sb_scr)

    for g in groups:
        reduce_chunk(2 * n_pairs, g, sa_scr, True)

    @pl.when(one_more)
    def _():
        for g in groups:
            reduce_chunk(2 * n_pairs + 1, g, sb_scr, True)

    w0 = pl.multiple_of(jnp.maximum(t0 + Q_TILE - win_len, 0), Q_TILE)
    kw = kw_ref[0, pl.ds(w0, win_len), :]
    dpos = (w0 + lax.broadcasted_iota(jnp.int32, (1, win_len), 1)) - qpos4
    wmask = (dpos + WINDOW).astype(jnp.uint32) <= WINDOW
    o_slc, o_win = [], []
    for g in groups:
        acc = acc_scr[g]
        o_slc.append(split_heads(acc / acc[:, den_lane[g]:den_lane[g] + 1]))
        vw_one = jnp.where(own[g], vw_ref[0, pl.ds(w0, win_len), :], 1.0).astype(BF16)
        s = jnp.where(wmask, lax.dot_general(q_all[g], kw, NT_DIMS, preferred_element_type=F32), NEG_INF)
        pw = jnp.exp(s - jnp.max(s, axis=-1, keepdims=True)).astype(BF16)
        acc = jnp.dot(pw, vw_one, preferred_element_type=F32)
        o_win.append(split_heads(acc / acc[:, den_lane[g]:den_lane[g] + 1]))

    for g in groups:
        for r in range(NSA_REP):
            hh = NSA_REP * g + r
            mix = (gd[:, 3 * hh:3 * hh + 1] * o_cmp[g][r] + gd[:, 3 * hh + 1:3 * hh + 2] * o_slc[g][r]
                   + gd[:, 3 * hh + 2:3 * hh + 3] * o_win[g][r])
            o_ref[0, :, hh * HEAD_DIM:(hh + 1) * HEAD_DIM] = mix[:, g * HEAD_DIM:(g + 1) * HEAD_DIM]


def nsa_prompt(q, gd, cmp, kvb, onehot, ovt):
    b, t, _ = q.shape
    ncp = cmp.shape[2]
    n_blocks = t // SEL_BLOCK
    assert n_blocks <= SEL_PAD and t % KEY_CHUNK == 0
    win_len = min(WINDOW + Q_TILE, t)
    kern = functools.partial(_nsa_kernel, n_blocks=n_blocks, n_sel=min(SEL_TOPN, n_blocks), win_len=win_len,
                             last_chunk=t // KEY_CHUNK - 1)
    rows = NSA_REP * Q_TILE
    slab = lambda j: pl.BlockSpec((1, t, LANES), lambda bi, i, j=j: (bi, 0, j))
    return pl.pallas_call(
        kern,
        grid=(b, t // Q_TILE),
        in_specs=[pl.BlockSpec((1, Q_TILE, NSA_HEADS * LANES), lambda bi, i: (bi, i, 0)),
                  pl.BlockSpec((1, Q_TILE, LANES), lambda bi, i: (bi, i, 0)),
                  pl.BlockSpec((1, 1, ncp, LANES), lambda bi, i: (0, bi, 0, 0)),
                  pl.BlockSpec((1, 1, ncp, LANES), lambda bi, i: (1, bi, 0, 0)),
                  slab(0), slab(1), slab(2), slab(3),
                  _const_spec(onehot.shape), _const_spec(ovt.shape)],
        out_specs=pl.BlockSpec((1, Q_TILE, NSA_HEADS * HEAD_DIM), lambda bi, i: (bi, i, 0)),
        out_shape=jax.ShapeDtypeStruct((b, t, NSA_HEADS * HEAD_DIM), F32),
        scratch_shapes=[pltpu.VMEM((KV_GROUPS, rows, KEY_CHUNK), F32), pltpu.VMEM((KV_GROUPS, rows, KEY_CHUNK), F32),
                        pltpu.VMEM((KV_GROUPS, rows, 1), F32), pltpu.VMEM((KV_GROUPS, rows, LANES), F32)],
        compiler_params=_cparams("parallel", "arbitrary"),
    )(q, gd, cmp, cmp, kvb, kvb, kvb, kvb, onehot, ovt)


def _ssd_kernel(xbc_ref, z_ref, gd_ref, cw_ref, cb_ref, a_ref, dsk_ref, ng_ref,
                y_ref, hfin_ref, ext_ref, h_ref, y_scr):
    c = pl.program_id(1)
    L = SSM_CHUNK
    pd = SSM_HEADS * HEAD_DIM
    gw = SSM_STATE

    @pl.when(c == 0)
    def _():
        ext_ref[0:SUBLANES, :] = jnp.zeros((SUBLANES, ext_ref.shape[1]), F32)
        h_ref[...] = jnp.zeros(h_ref.shape, F32)

    ext_ref[SUBLANES:SUBLANES + L, :] = xbc_ref[0]
    conv = cb_ref[...]
    for k in range(SSM_CONV):
        off = SUBLANES - (SSM_CONV - 1) + k
        conv = conv + ext_ref[off:off + L, :] * cw_ref[k:k + 1, :]
    ext_ref[0:SUBLANES, :] = ext_ref[L:L + SUBLANES, :]
    act = _silu(conv)
    xs = act[:, :pd]
    gd = gd_ref[0]

    ii = lax.broadcasted_iota(jnp.int32, (L, L), 0)
    jj = lax.broadcasted_iota(jnp.int32, (L, L), 1)
    causal = ii >= jj
    tri = jnp.where(causal, 1.0, 0.0).astype(BF16)
    acum = None
    for part in _split3(gd * a_ref[...]):
        term = jnp.dot(tri, part, preferred_element_type=F32)
        acum = term if acum is None else acum + term
    acum_t = acum.T

    for g in range(SSM_GROUPS):
        bm = act[:, pd + g * gw:pd + (g + 1) * gw].astype(BF16)
        cm = act[:, pd + SSM_GROUPS * gw + g * gw:pd + SSM_GROUPS * gw + (g + 1) * gw].astype(BF16)
        cb = lax.dot_general(cm, bm, NT_DIMS, preferred_element_type=F32)
        for hl in range(SSM_HEADS // SSM_GROUPS):
            hh = g * (SSM_HEADS // SSM_GROUPS) + hl
            col = acum[:, DT_LANE0 + hh:DT_LANE0 + hh + 1]
            row = acum_t[DT_LANE0 + hh:DT_LANE0 + hh + 1, :]
            last = acum[L - 1:L, DT_LANE0 + hh:DT_LANE0 + hh + 1]
            lmat = jnp.exp(jnp.where(causal, col - row, NEG_INF))
            x_h = xs[:, hh * HEAD_DIM:(hh + 1) * HEAD_DIM]
            xdt = x_h * gd[:, DT_LANE0 + hh:DT_LANE0 + hh + 1]
            y_diag = jnp.dot((cb * lmat).astype(BF16), xdt.astype(BF16), preferred_element_type=F32)
            h_in = h_ref[hh]
            y_off = lax.dot_general(cm, h_in.astype(BF16), NT_DIMS, preferred_element_type=F32) * jnp.exp(col)
            st = lax.dot_general((xdt * jnp.exp(last - col)).astype(BF16), bm, TN_DIMS,
                                 preferred_element_type=F32)
            h_ref[hh] = jnp.exp(last) * h_in + st
            y_scr[:, hh * HEAD_DIM:(hh + 1) * HEAD_DIM] = (
                y_diag + y_off + dsk_ref[:, hh * HEAD_DIM:(hh + 1) * HEAD_DIM] * x_h)

    y = y_scr[...] * _silu(z_ref[0])
    for g in range(SSM_GROUPS):
        cols = slice(g * gw, (g + 1) * gw)
        yg = y[:, cols]
        ms = jnp.mean(yg * yg, axis=-1, keepdims=True)
        y_ref[0, :, cols] = yg * lax.rsqrt(ms + EPS) * ng_ref[:, cols]

    @pl.when(c == pl.num_programs(1) - 1)
    def _():
        hfin_ref[0] = h_ref[...]


def ssd_prompt(xbc, z, gd, lw):
    b, t, cd = xbc.shape
    L = SSM_CHUNK
    pd = SSM_HEADS * HEAD_DIM
    tile = lambda n: pl.BlockSpec((1, L, n), lambda i, j: (i, j, 0))
    return pl.pallas_call(
        _ssd_kernel,
        grid=(b, t // L),
        in_specs=[tile(cd), tile(pd), tile(LANES),
                  _const_spec((SSM_CONV, cd)), _const_spec((1, cd)), _const_spec((1, LANES)),
                  _const_spec((1, pd)), _const_spec((1, pd))],
        out_specs=[tile(pd), pl.BlockSpec((1, SSM_HEADS, HEAD_DIM, SSM_STATE), lambda i, j: (i, 0, 0, 0))],
        out_shape=[jax.ShapeDtypeStruct((b, t, pd), F32),
                   jax.ShapeDtypeStruct((b, SSM_HEADS, HEAD_DIM, SSM_STATE), F32)],
        scratch_shapes=[pltpu.VMEM((SUBLANES + L, cd), F32),
                        pltpu.VMEM((SSM_HEADS, HEAD_DIM, SSM_STATE), F32),
                        pltpu.VMEM((L, pd), F32)],
        compiler_params=_cparams("parallel", "arbitrary"),
    )(xbc, z, gd, lw["conv_w"], lw["conv_b"], lw["a_row"], lw["d_skip"], lw["ssm_norm"])


SEG_PER_PAGE = PAGE_SIZE // CMP_STRIDE
BLOCKS_PER_PAGE = PAGE_SIZE // SEL_BLOCK
N_PICK = SEL_TOPN - 1
SEQ_PER_STEP = SUBLANES // KV_GROUPS


def _past_compress_kernel(pt_ref, *refs):
    del pt_ref
    n_pages = len(refs) - 4
    pages = refs[:n_pages]
    w_ref, pe_ref, o_ref, x_scr = refs[n_pages:]
    nrow = n_pages * SEG_PER_PAGE
    row = lax.broadcasted_iota(jnp.int32, (nrow, LANES), 0)
    for kv in range(2):
        for k, pg in enumerate(pages):
            x_scr[k * PAGE_SIZE:(k + 1) * PAGE_SIZE, :] = pg[kv].T
        seg = jnp.concatenate([x_scr[pl.ds(jj, nrow, stride=CMP_STRIDE), :].astype(BF16)
                               for jj in range(CMP_STRIDE)], axis=1)
        y = jnp.dot(jnp.concatenate([seg, pe_ref[kv]], axis=0), w_ref[kv], preferred_element_type=F32)
        second = pltpu.roll(y[:nrow, LANES:], nrow - 1, 0)
        second = jnp.where(row < nrow - 1, second, 0.0)
        out = y[:nrow, :LANES] + second + y[nrow:nrow + 1, :LANES] + y[nrow + 1:nrow + 2, LANES:]
        o_ref[kv] = out.astype(o_ref.dtype)


def past_compress(cache_t, page_table, w_all, pe_all):
    depth = cache_t.shape[1]
    db, n_pages = page_table.shape
    nrow = n_pages * SEG_PER_PAGE
    page_spec = lambda k: pl.BlockSpec(
        (None, None, 2, LANES, PAGE_SIZE), lambda l, b, pt, k=k: (pt[b, k], l, 0, 0, 0))
    grid_spec = pltpu.PrefetchScalarGridSpec(
        num_scalar_prefetch=1,
        grid=(depth, db),
        in_specs=[page_spec(k) for k in range(n_pages)]
                 + [pl.BlockSpec((None,) + w_all.shape[1:], lambda l, b, pt: (l, 0, 0, 0)),
                    pl.BlockSpec((None,) + pe_all.shape[1:], lambda l, b, pt: (l, 0, 0, 0))],
        out_specs=pl.BlockSpec((None, 2, None, nrow, LANES), lambda l, b, pt: (l, 0, b, 0, 0)),
        scratch_shapes=[pltpu.VMEM((n_pages * PAGE_SIZE, LANES), F32)],
    )
    return pl.pallas_call(
        _past_compress_kernel,
        grid_spec=grid_spec,
        out_shape=jax.ShapeDtypeStruct((depth, 2, db, nrow, LANES), BF16),
        compiler_params=_cparams("parallel", "parallel"),
    )(page_table, *([cache_t] * n_pages), w_all, pe_all)


def _sample_cmp_kernel(q_ref, kc_ref, vc_ref, ov_ref, ocmp_ref, idx_ref, *, past_len):
    ncp = kc_ref.shape[3]
    cend = lax.broadcasted_iota(jnp.int32, (1, ncp), 1) * CMP_STRIDE + (CMP_BLOCK - 1)
    cmask = cend <= past_len
    hrow = lax.broadcasted_iota(jnp.int32, (NSA_HEADS, 1), 0)
    sums = []
    for sq in range(SEQ_PER_STEP):
        q = q_ref[sq]
        s = lax.dot_general(q, kc_ref[0, 0, sq], NT_DIMS, preferred_element_type=F32)
        p = jnp.where(cmask, _softmax_rows(s, cmask), 0.0)
        ocmp_ref[sq] = jnp.dot(p.astype(BF16), vc_ref[0, 0, sq], preferred_element_type=F32)
        sums += [jnp.sum(jnp.where((hrow // NSA_REP) == g, p, 0.0), axis=0, keepdims=True)
                 for g in range(KV_GROUPS)]
    psum = jnp.concatenate(sums, axis=0)
    imp = None
    for part in _split3(psum):
        term = jnp.dot(part, ov_ref[...], preferred_element_type=F32)
        imp = term if imp is None else imp + term
    n_past = past_len // SEL_BLOCK
    blk = lax.broadcasted_iota(jnp.int32, (SUBLANES, SEL_PAD), 1)
    forced = jnp.where(blk == 0, 1.0, jnp.where(blk == n_past - 1, 1.0, 0.0))
    score = jnp.where(blk < n_past, imp + SEL_BONUS * forced, -jnp.inf)
    col = lax.broadcasted_iota(jnp.int32, (SUBLANES, LANES), 1)
    picked = jnp.zeros((SUBLANES, LANES), jnp.int32)
    for k in range(N_PICK):
        m = jnp.max(score, axis=-1, keepdims=True)
        idx = jnp.min(jnp.where(score == m, blk, SEL_PAD), axis=-1, keepdims=True)
        score = jnp.where(blk == idx, -jnp.inf, score)
        picked = jnp.where(col == k, idx, picked)
    idx_ref[0] = picked


def sample_cmp_select(q8, cmp_past, layer, ov, past_len):
    db = q8.shape[0]
    ncp = cmp_past.shape[3]
    n_past = past_len // SEL_BLOCK
    assert n_past <= SEL_PAD and n_past + 1 > SEL_TOPN and past_len % SEL_BLOCK == 0
    assert db % SEQ_PER_STEP == 0
    o_cmp, picked = pl.pallas_call(
        functools.partial(_sample_cmp_kernel, past_len=past_len),
        grid=(db // SEQ_PER_STEP,),
        in_specs=[pl.BlockSpec((SEQ_PER_STEP, NSA_HEADS, LANES), lambda b: (b, 0, 0)),
                  pl.BlockSpec((1, 1, SEQ_PER_STEP, ncp, LANES), lambda b: (layer, 0, b, 0, 0)),
                  pl.BlockSpec((1, 1, SEQ_PER_STEP, ncp, LANES), lambda b: (layer, 1, b, 0, 0)),
                  _const_spec(ov.shape)],
        out_specs=[pl.BlockSpec((SEQ_PER_STEP, NSA_HEADS, LANES), lambda b: (b, 0, 0)),
                   pl.BlockSpec((1, SUBLANES, LANES), lambda b: (b, 0, 0))],
        out_shape=[jax.ShapeDtypeStruct((db, NSA_HEADS, LANES), F32),
                   jax.ShapeDtypeStruct((db // SEQ_PER_STEP, SUBLANES, LANES), jnp.int32)],
        compiler_params=_cparams("parallel"),
    )(q8, cmp_past, cmp_past, ov)
    return o_cmp, picked.reshape(db, KV_GROUPS, LANES)[:, :, :SEL_TOPN]


def _sample_attn_kernel(idx_ref, pt_ref, *refs):
    del pt_ref
    kpage = refs[:N_PICK]
    vpage = refs[N_PICK:2 * N_PICK]
    q_ref, kv_ref, gd_ref, ocmp_ref, win_ref, o_ref = refs[2 * N_PICK:]
    b = pl.program_id(0)
    g = pl.program_id(1)
    q = q_ref[0]
    qf = q.astype(F32)
    kvn = kv_ref[0]
    lane = lax.broadcasted_iota(jnp.int32, (1, LANES), 1)
    chan = lax.broadcasted_iota(jnp.int32, (LANES, 1), 0)
    tok_half = lax.broadcasted_iota(jnp.int32, (1, PAGE_SIZE), 1) // SEL_BLOCK

    def branch(gg, keys_t, vals_t, tok_masks, j_new):
        own = (lane >= gg * HEAD_DIM) & (lane < (gg + 1) * HEAD_DIM)
        own_t = (chan >= gg * HEAD_DIM) & (chan < (gg + 1) * HEAD_DIM)
        ss = []
        for kt, tm in zip(keys_t, tok_masks):
            s = jnp.dot(q, kt.astype(BF16), preferred_element_type=F32)
            ss.append(s if tm is None else jnp.where(tm, s, NEG_INF))
        k_new = kvn[:, j_new * LANES:(j_new + 1) * LANES].astype(BF16).astype(F32)
        s_new = jnp.sum(qf * k_new, axis=-1, keepdims=True)
        m = s_new
        for s in ss:
            m = jnp.maximum(m, jnp.max(s, axis=-1, keepdims=True))
        v_new = jnp.where(own, kvn[:, (j_new + 1) * LANES:(j_new + 2) * LANES], 1.0).astype(BF16).astype(F32)
        acc = jnp.exp(s_new - m).astype(BF16).astype(F32) * v_new
        for s, vt in zip(ss, vals_t):
            v_one_t = jnp.where(own_t, vt, 1.0).astype(BF16)
            acc = acc + lax.dot_general(jnp.exp(s - m).astype(BF16), v_one_t, NT_DIMS, preferred_element_type=F32)
        return acc

    gd = gd_ref[0]
    ocmp = ocmp_ref[0]
    for gg in range(KV_GROUPS):
        @pl.when(g == gg)
        def _(gg=gg):
            den = (1 - gg) * HEAD_DIM
            masks = [tok_half == idx_ref[b, gg, j] % BLOCKS_PER_PAGE for j in range(N_PICK)]
            acc = branch(gg, [r[...] for r in kpage], [r[...] for r in vpage], masks, 2)
            o_slc = acc / acc[:, den:den + 1]
            acc = branch(gg, [win_ref[0]], [win_ref[1]], [None], 4)
            o_win = acc / acc[:, den:den + 1]
            for r in range(NSA_REP):
                hh = NSA_REP * gg + r
                mix = (gd[:, 3 * hh:3 * hh + 1] * ocmp[hh:hh + 1] + gd[:, 3 * hh + 1:3 * hh + 2] * o_slc[hh:hh + 1]
                       + gd[:, 3 * hh + 2:3 * hh + 3] * o_win[hh:hh + 1])
                o_ref[0, :, hh * HEAD_DIM:(hh + 1) * HEAD_DIM] = mix[:, gg * HEAD_DIM:(gg + 1) * HEAD_DIM]


def sample_attend(q8, kv, gd, ocmp, idx, page_table, cache_t, win_t, layer):
    db = q8.shape[0]
    page_spec = lambda which, j: pl.BlockSpec(
        (None, None, None, LANES, PAGE_SIZE),
        lambda b, g, ix, pt, j=j: (pt[b, ix[b, g, j] // BLOCKS_PER_PAGE], layer, which, 0, 0))
    per_b = lambda shape: pl.BlockSpec((1,) + shape, lambda b, g, ix, pt: (b,) + (0,) * len(shape))
    wlen = win_t.shape[4]
    grid_spec = pltpu.PrefetchScalarGridSpec(
        num_scalar_prefetch=2,
        grid=(db, KV_GROUPS),
        in_specs=[page_spec(2, j) for j in range(N_PICK)] + [page_spec(3, j) for j in range(N_PICK)]
                 + [per_b((NSA_HEADS, LANES)), per_b((1, 6 * LANES)), per_b((1, LANES)), per_b((NSA_HEADS, LANES)),
                    pl.BlockSpec((None, None, 2, LANES, wlen), lambda b, g, ix, pt: (b, layer, 0, 0, 0))],
        out_specs=per_b((1, NSA_HEADS * HEAD_DIM)),
    )
    return pl.pallas_call(
        _sample_attn_kernel,
        grid_spec=grid_spec,
        out_shape=jax.ShapeDtypeStruct((db, 1, NSA_HEADS * HEAD_DIM), F32),
        compiler_params=_cparams("parallel", "arbitrary"),
    )(idx, page_table, *([cache_t] * (2 * N_PICK)), q8, kv, gd, ocmp, win_t)


def _sample_state_kernel(u_ref, xbc_ref, z_ref, gd_ref, pool_ref, conv_ref, h_ref,
                         pw_ref, ps_ref, cw_ref, cb_ref, a_ref, dsk_ref, ng_ref,
                         ypool_ref, yssm_ref, npool_ref, nconv_ref, nh_ref):
    pd = SSM_HEADS * HEAD_DIM
    u = u_ref[0]
    ext = jnp.concatenate([pool_ref[...], u], axis=0)
    row = lax.broadcasted_iota(jnp.int32, ext.shape, 0)
    lane = lax.broadcasted_iota(jnp.int32, (1, ext.shape[1]), 1)
    d = jnp.zeros_like(u)
    for gi, win in enumerate(POOL_WINDOWS):
        s = jnp.sum(jnp.where(row >= POOL_BUF + 1 - win, ext, 0.0), axis=0, keepdims=True)
        d = jnp.where((lane >= gi * HEAD_DIM) & (lane < (gi + 1) * HEAD_DIM), s / float(win) - u, d)
    d8 = jnp.concatenate([d, jnp.zeros((SUBLANES - 1, d.shape[1]), F32)], axis=0).astype(BF16)
    ypool_ref[0] = jnp.dot(d8, pw_ref[...], preferred_element_type=F32)[0:1] * ps_ref[...]
    npool_ref[...] = ext[1:]

    xbc = xbc_ref[0]
    extc = jnp.concatenate([conv_ref[...], xbc], axis=0)
    act = _silu(jnp.sum(extc * cw_ref[...], axis=0, keepdims=True) + cb_ref[...])
    nconv_ref[...] = extc[1:]
    gd = gd_ref[0]
    dec_row = jnp.exp(gd * a_ref[...])
    eye = (lax.broadcasted_iota(jnp.int32, (HEAD_DIM, HEAD_DIM), 0)
           == lax.broadcasted_iota(jnp.int32, (HEAD_DIM, HEAD_DIM), 1))
    ys = []
    for hh in range(SSM_HEADS):
        g = hh // (SSM_HEADS // SSM_GROUPS)
        x_row = act[:, hh * HEAD_DIM:(hh + 1) * HEAD_DIM]
        dt = gd[:, DT_LANE0 + hh:DT_LANE0 + hh + 1]
        xdt_col = jnp.sum(jnp.where(eye, x_row * dt, 0.0), axis=1, keepdims=True)
        b_row = act[:, pd + g * SSM_STATE:pd + (g + 1) * SSM_STATE]
        c_row = act[:, pd + (SSM_GROUPS + g) * SSM_STATE:pd + (SSM_GROUPS + g + 1) * SSM_STATE]
        h_new = dec_row[:, DT_LANE0 + hh:DT_LANE0 + hh + 1] * h_ref[hh] + xdt_col * b_row
        nh_ref[hh] = h_new
        y_col = jnp.sum(h_new * c_row, axis=1, keepdims=True)
        y_row = jnp.sum(jnp.where(eye, y_col, 0.0), axis=0, keepdims=True)
        ys.append(y_row + dsk_ref[:, hh * HEAD_DIM:(hh + 1) * HEAD_DIM] * x_row)
    y = jnp.concatenate(ys, axis=1) * _silu(z_ref[0])
    outs = []
    for g in range(SSM_GROUPS):
        yg = y[:, g * SSM_STATE:(g + 1) * SSM_STATE]
        ms = jnp.mean(yg * yg, axis=-1, keepdims=True)
        outs.append(yg * lax.rsqrt(ms + EPS) * ng_ref[:, g * SSM_STATE:(g + 1) * SSM_STATE])
    yssm_ref[0] = jnp.concatenate(outs, axis=1)


def sample_state_mixers(u, xbc, z, gd, state_pool, state_conv, state_ssm, layer, lw):
    db = u.shape[0]
    per_b = lambda shape: pl.BlockSpec((1,) + shape, lambda b: (b,) + (0,) * len(shape))
    st = lambda shape: pl.BlockSpec((None, None) + shape, lambda b: (b, layer) + (0,) * len(shape))
    new = lambda shape: pl.BlockSpec((None,) + shape, lambda b: (b,) + (0,) * len(shape))
    ps, cs, hs = state_pool.shape[2:], state_conv.shape[2:], state_ssm.shape[2:]
    params = [lw['pool_w'], lw['pool_scale'], lw['conv_w'], lw['conv_b'], lw['a_row'], lw['d_skip'], lw['ssm_norm']]
    return pl.pallas_call(
        _sample_state_kernel,
        grid=(db,),
        in_specs=[per_b(u.shape[1:]), per_b(xbc.shape[1:]), per_b(z.shape[1:]), per_b(gd.shape[1:]),
                  st(ps), st(cs), st(hs)] + [_const_spec(p.shape) for p in params],
        out_specs=[per_b((1, ps[1])), per_b((1, z.shape[2])), new(ps), new(cs), new(hs)],
        out_shape=[jax.ShapeDtypeStruct((db, 1, ps[1]), F32), jax.ShapeDtypeStruct((db, 1, z.shape[2]), F32),
                   jax.ShapeDtypeStruct((db,) + ps, F32), jax.ShapeDtypeStruct((db,) + cs, F32),
                   jax.ShapeDtypeStruct((db,) + hs, F32)],
        compiler_params=_cparams("parallel"),
    )(u, xbc, z, gd, state_pool, state_conv, state_ssm, *params)


def _prep_layer(p, i):
    d = p['w_in'].shape[1]
    w_in = p['w_in'][i]
    o = np.cumsum([0, 256, 512, 768, GATE_LANES, 256, 768, SSM_HEADS])
    w_q = w_in[:, o[1]:o[2]].reshape(d, NSA_HEADS, HEAD_DIM)
    slot = jnp.zeros((d, NSA_HEADS, KV_GROUPS, HEAD_DIM), F32)
    for hh in range(NSA_HEADS):
        slot = slot.at[:, hh, hh // NSA_REP].set(w_q[:, hh])
    w_gd = jnp.zeros((d, LANES), F32)
    w_gd = w_gd.at[:, :GATE_LANES].set(w_in[:, o[3]:o[4]])
    w_gd = w_gd.at[:, DT_LANE0:DT_LANE0 + SSM_HEADS].set(w_in[:, o[6]:o[7]])
    lane_pad = lambda v: jnp.zeros((1, LANES), F32).at[0, DT_LANE0:DT_LANE0 + SSM_HEADS].set(v)
    eye_g = jnp.eye(KV_GROUPS, dtype=F32)
    ratio = CMP_BLOCK // CMP_STRIDE
    cmp_w, cmp_pe = [], []
    for j in range(2):
        w4 = p['nsa_cmp_w'][i, j].reshape(ratio, CMP_STRIDE, HEAD_DIM, HEAD_DIM)
        cmp_w.append(jnp.einsum('rjde,gh->jgdrhe', w4, eye_g).reshape(CMP_STRIDE * 2 * HEAD_DIM, ratio * LANES))
        pe4 = p['nsa_cmp_pe'][i, j].reshape(ratio, CMP_STRIDE, 1, HEAD_DIM)
        pe_rows = jnp.broadcast_to(pe4, (ratio, CMP_STRIDE, KV_GROUPS, HEAD_DIM)).reshape(ratio, -1)
        cmp_pe.append(jnp.zeros((SUBLANES, pe_rows.shape[1]), F32).at[:ratio].set(pe_rows))
    pool_w = jnp.zeros((256, 256), F32)
    for gi in range(len(POOL_WINDOWS)):
        sl = slice(gi * HEAD_DIM, (gi + 1) * HEAD_DIM)
        pool_w = pool_w.at[sl, sl].set(p['pool_w'][i, gi])
    w_out = p['w_out'][i]
    bf = lambda x: x.astype(BF16)
    row = lambda x: x.reshape(1, -1)
    return {
        'ffn1_norm': row(p['ffn1_norm'][i]), 'ffn1_w_gate': bf(p['ffn1_w_gate'][i]),
        'ffn1_w_up': bf(p['ffn1_w_up'][i]), 'ffn1_w_down': bf(p['ffn1_w_down'][i]),
        'ffn2_norm': row(p['ffn2_norm'][i]), 'ffn2_w_gate': bf(p['ffn2_w_gate'][i]),
        'ffn2_w_up': bf(p['ffn2_w_up'][i]), 'ffn2_w_down': bf(p['ffn2_w_down'][i]),
        'mix_norm': row(p['mix_norm'][i]),
        'w_u': bf(w_in[:, o[0]:o[1]]), 'w_q': bf(slot.reshape(d, NSA_HEADS * LANES)),
        'w_kv': bf(w_in[:, o[2]:o[3]]), 'w_gd': bf(w_gd), 'w_z': bf(w_in[:, o[4]:o[5]]),
        'w_xbc': bf(w_in[:, o[5]:o[6]]),
        'q_norm': row(jnp.tile(p['nsa_q_norm'][i], 2)), 'k_norm': jnp.tile(p['nsa_k_norm'][i], (1, 2)),
        'dt_bias': lane_pad(p['ssm_dt_bias'][i]),
        'w_out_pool': bf(w_out[:256]), 'w_out_nsa': bf(w_out[256:768]), 'w_out_ssm': bf(w_out[768:]),
        'pool_w': bf(pool_w), 'pool_scale': row(p['pool_scale'][i]),
        'cmp_w': bf(jnp.stack(cmp_w)), 'cmp_pe': bf(jnp.stack(cmp_pe)),
        'conv_w': p['ssm_conv_w'][i], 'conv_b': row(p['ssm_conv_b'][i]),
        'a_row': lane_pad(-jnp.exp(p['ssm_a_log'][i])),
        'd_skip': row(jnp.repeat(p['ssm_d'][i], HEAD_DIM)), 'ssm_norm': row(p['ssm_norm'][i]),
        'ple_norm': row(p['ple_norm'][i]), 'ple_w_gate': bf(p['ple_w_gate'][i]),
        'ple_w_proj': bf(p['ple_w_proj'][i]),
    }


def _rope_tables(pos):
    half = HEAD_DIM // 2
    inv = ROPE_THETA ** (-jnp.arange(half, dtype=F32) / half)
    ang = pos.astype(F32)[:, None] * inv[None, :]
    cos = jnp.cos(ang)
    sin = jnp.sin(ang)
    return jnp.tile(cos, (1, 4)), jnp.tile(jnp.concatenate([-sin, sin], axis=1), (1, 2))


def _selection_constants(t):
    nseg = t // CMP_STRIDE
    nc = nseg - CMP_BLOCK // CMP_STRIDE + 1
    ns = t // SEL_BLOCK
    c_start = np.arange(nseg) * CMP_STRIDE
    s_start = np.arange(SEL_PAD) * SEL_BLOCK
    ovt = ((c_start[None, :] < s_start[:, None] + SEL_BLOCK) & (c_start[None, :] + CMP_BLOCK > s_start[:, None])
           & (np.arange(nseg)[None, :] < nc) & (np.arange(SEL_PAD)[:, None] < ns))
    onehot = (np.arange(t)[:, None] // SEL_BLOCK) == np.arange(SEL_PAD)[None, :]
    return jnp.asarray(onehot, BF16), jnp.asarray(ovt, BF16)


def _channel_major(cache):
    nd = cache.ndim
    t = jnp.transpose(cache, tuple(range(nd - 3)) + (nd - 2, nd - 1, nd - 3))
    return t.reshape(cache.shape[:nd - 3] + (cache.shape[-2] * cache.shape[-1], cache.shape[-3]))


def _token_layer_front(h, lw, cos, sin, seq_len=None):
    h = ffn_halfstep(h, lw['ffn1_norm'], lw['ffn1_w_gate'], lw['ffn1_w_up'], lw['ffn1_w_down'])
    return (h,) + tuple(in_projection(h, lw, cos, sin, seq_len))


def _token_layer_back(h, y_pool, y_nsa, y_ssm, pe, lw):
    h = out_projection(h, y_pool, y_nsa, y_ssm, lw)
    h = ffn_halfstep(h, lw['ffn2_norm'], lw['ffn2_w_gate'], lw['ffn2_w_up'], lw['ffn2_w_down'])
    return ple_step(h, pe, lw)


def kernel(x_prompt, x_sample, cache_nsa_kv, cache_win_kv, state_pool, state_conv, state_ssm, page_table,
           p_prompt, p_sample, ffn1_norm, ffn1_w_gate, ffn1_w_up, ffn1_w_down, mix_norm, w_in, w_out,
           pool_w, pool_scale, nsa_q_norm, nsa_k_norm, nsa_cmp_pe, nsa_cmp_w, ssm_conv_w, ssm_conv_b,
           ssm_dt_bias, ssm_a_log, ssm_d, ssm_norm, ffn2_norm, ffn2_w_gate, ffn2_w_up, ffn2_w_down,
           ple_norm, ple_w_gate, ple_w_proj):
    params = dict(ffn1_norm=ffn1_norm, ffn1_w_gate=ffn1_w_gate, ffn1_w_up=ffn1_w_up, ffn1_w_down=ffn1_w_down,
                  mix_norm=mix_norm, w_in=w_in, w_out=w_out, pool_w=pool_w, pool_scale=pool_scale,
                  nsa_q_norm=nsa_q_norm, nsa_k_norm=nsa_k_norm, nsa_cmp_pe=nsa_cmp_pe, nsa_cmp_w=nsa_cmp_w,
                  ssm_conv_w=ssm_conv_w, ssm_conv_b=ssm_conv_b, ssm_dt_bias=ssm_dt_bias, ssm_a_log=ssm_a_log,
                  ssm_d=ssm_d, ssm_norm=ssm_norm, ffn2_norm=ffn2_norm, ffn2_w_gate=ffn2_w_gate,
                  ffn2_w_up=ffn2_w_up, ffn2_w_down=ffn2_w_down, ple_norm=ple_norm, ple_w_gate=ple_w_gate,
                  ple_w_proj=ple_w_proj)
    depth = w_in.shape[0]
    b, t, d = x_prompt.shape
    db = x_sample.shape[0]
    past_len = page_table.shape[1] * PAGE_SIZE
    wkeep = min(WINDOW, t)

    cos_p, sin_p = _rope_tables(jnp.tile(jnp.arange(t, dtype=jnp.int32), b))
    cos_s, sin_s = _rope_tables(jnp.full((db,), past_len, jnp.int32))
    onehot, ovt = _selection_constants(t)

    lws = [_prep_layer(params, i) for i in range(depth)]
    cache_t, win_t = _channel_major(cache_nsa_kv), _channel_major(cache_win_kv)
    cmp_past = past_compress(cache_t, page_table, jnp.stack([lw['cmp_w'] for lw in lws]),
                             jnp.stack([lw['cmp_pe'] for lw in lws]))
    ov_past = _selection_constants(past_len)[1].T
    s3 = lambda x: x.reshape(db, 1, x.shape[-1])

    h_p = x_prompt.reshape(b * t, d)
    h_s = x_sample.reshape(db, d)
    st_p = [[] for _ in range(5)]
    st_s = [[] for _ in range(5)]
    for i in range(depth):
        lw = lws[i]
        h_p, u, qp, rows, wins, kvb, seg, gd, z, xbc = _token_layer_front(h_p, lw, cos_p, sin_p, t)
        r3 = lambda x: x.reshape(b, t, x.shape[-1])
        y_pool = pool_prompt(r3(u), lw)
        cmp = nsa_compress(seg.reshape(2, b, t // CMP_STRIDE, CMP_STRIDE * LANES), lw['cmp_w'], lw['cmp_pe'])
        y_nsa = nsa_prompt(r3(qp), r3(gd), cmp, r3(kvb), onehot, ovt)
        y_ssm, h_fin = ssd_prompt(r3(xbc), r3(z), r3(gd), lw)
        h_p = _token_layer_back(h_p, y_pool.reshape(b * t, -1), y_nsa.reshape(b * t, -1),
                                y_ssm.reshape(b * t, -1), p_prompt[i].reshape(b * t, -1), lw)
        st_p[0].append(rows.reshape(b, 4, t, KV_GROUPS, HEAD_DIM))
        st_p[1].append(wins[:, :, t - wkeep:].reshape(b, 2, wkeep, KV_GROUPS, HEAD_DIM))
        st_p[2].append(r3(u)[:, t - POOL_BUF:])
        st_p[3].append(r3(xbc)[:, t - (SSM_CONV - 1):])
        st_p[4].append(h_fin)
        h_s, u, qp, kv, gd, z, xbc = _token_layer_front(h_s, lw, cos_s, sin_s)
        q8 = qp.reshape(db, NSA_HEADS, LANES)
        o_cmp, picked = sample_cmp_select(q8, cmp_past, i, ov_past, past_len)
        y_nsa = sample_attend(q8, s3(kv), s3(gd), o_cmp, picked, page_table, cache_t, win_t, i)
        y_pool, y_ssm, new_pool, new_conv, new_h = sample_state_mixers(
            s3(u), s3(xbc), s3(z), s3(gd), state_pool, state_conv, state_ssm, i, lw)
        h_s = _token_layer_back(h_s, y_pool.reshape(db, -1), y_nsa.reshape(db, -1), y_ssm.reshape(db, -1),
                                p_sample[i].reshape(db, -1), lw)
        kv6 = kv.reshape(db, 6, 1, KV_GROUPS, HEAD_DIM)
        rows = kv6[:, 0:4]
        new_win = jnp.concatenate([cache_win_kv[:, i, :, 1:], kv6[:, 4:6]], axis=2)
        for j, v in enumerate((rows, new_win, new_pool, new_conv, new_h)):
            st_s[j].append(v)
    outs = [h_p.reshape(b, t, d), h_s.reshape(db, 1, d)]
    for j in range(5):
        outs.append(jnp.stack(st_p[j], axis=1))
        outs.append(jnp.stack(st_s[j], axis=1))
    return tuple(outs)
```

```python
import functools

import jax
import jax.numpy as jnp
import numpy as np
from jax import lax
from jax.experimental import pallas as pl
from jax.experimental.pallas import tpu as pltpu

F32 = jnp.float32
BF16 = jnp.bfloat16

POOL_WINDOWS = (2, 4, 8, 16)
POOL_BUF = 15
HEAD_DIM = 64
NSA_HEADS = 8
KV_GROUPS = 2
NSA_REP = NSA_HEADS // KV_GROUPS
CMP_BLOCK = 32
CMP_STRIDE = 16
SEL_BLOCK = 64
SEL_TOPN = 16
WINDOW = 512
SEL_BONUS = 1.0e4
NEG_INF = -1.0e30
SSM_HEADS = 4
SSM_GROUPS = 2
SSM_STATE = 128
SSM_CONV = 4
SSM_CHUNK = 128
ROPE_THETA = 10000.0
EPS = 1e-6
PAGE_SIZE = 128

LANES = 128
SUBLANES = 8
VMEM_LIMIT_BYTES = 56 * 1024 * 1024

GATE_LANES = 3 * NSA_HEADS
DT_LANE0 = 32

NT_DIMS = (((1,), (1,)), ((), ()))
TN_DIMS = (((0,), (0,)), ((), ()))


def _cparams(*sem):
    return pltpu.CompilerParams(dimension_semantics=sem, vmem_limit_bytes=VMEM_LIMIT_BYTES)


def _const_spec(shape):
    nd = len(shape)
    return pl.BlockSpec(shape, lambda *_: (0,) * nd)


def _rmsnorm(x, g):
    ms = jnp.mean(x * x, axis=-1, keepdims=True)
    return x * lax.rsqrt(ms + EPS) * g


def _silu(x):
    return x * jax.nn.sigmoid(x)


def _split3(x):
    hi = x.astype(BF16)
    r = x - hi.astype(F32)
    mid = r.astype(BF16)
    lo = (r - mid.astype(F32)).astype(BF16)
    return hi, mid, lo


def _ffn_kernel(x_ref, g_ref, wg_ref, wu_ref, wd_ref, o_ref, *, n_chunks):
    x = x_ref[...]
    xn = _rmsnorm(x, g_ref[...]).astype(BF16)
    fc = wg_ref.shape[1] // n_chunks
    tot = None
    for c in range(n_chunks):
        sl = slice(c * fc, (c + 1) * fc)
        g = jnp.dot(xn, wg_ref[:, sl], preferred_element_type=F32)
        u = jnp.dot(xn, wu_ref[:, sl], preferred_element_type=F32)
        a = (_silu(g) * u).astype(BF16)
        d = jnp.dot(a, wd_ref[sl, :], preferred_element_type=F32)
        tot = d if tot is None else tot + d
    o_ref[...] = x + 0.5 * tot


def _row_tile(m, pref):
    return pref if m % pref == 0 else m


def _resident_spec(shape):
    nd = len(shape)
    return pl.BlockSpec(shape, lambda *_: (0,) * nd, pipeline_mode=pl.Buffered(1))


def ffn_halfstep(h, g, wg, wu, wd):
    m, d = h.shape
    f = wg.shape[1]
    tm = _row_tile(m, 1024)
    return pl.pallas_call(
        functools.partial(_ffn_kernel, n_chunks=2 if tm < 1024 else f // (2 * LANES)),
        grid=(m // tm,),
        in_specs=[pl.BlockSpec((tm, d), lambda i: (i, 0)),
                  _const_spec((1, d)), _resident_spec((d, f)), _resident_spec((d, f)), _resident_spec((f, d))],
        out_specs=pl.BlockSpec((tm, d), lambda i: (i, 0)),
        out_shape=jax.ShapeDtypeStruct((m, d), F32),
        compiler_params=_cparams("parallel"),
    )(h, g, wg, wu, wd)


def _rope128(x, cos, sin_signed, lane):
    rot = jnp.where((lane & 32) == 0, pltpu.roll(x, 96, 1), pltpu.roll(x, 32, 1))
    return x * cos + rot * sin_signed


def _inproj_kernel(x_ref, g_ref, wu_ref, wq_ref, wkv_ref, wgd_ref, wz_ref, wx_ref,
                   qn_ref, kn_ref, dtb_ref, cos_ref, sin_ref, *out_refs, prompt):
    if prompt:
        u_ref, q_ref, rows_ref, wins_ref, kvb_ref, seg_ref, gd_ref, z_ref, xbc_ref, seg_scr = out_refs
    else:
        u_ref, q_ref, kv_ref, gd_ref, z_ref, xbc_ref = out_refs
    x = x_ref[...]
    tm = x.shape[0]
    xn = _rmsnorm(x, g_ref[...]).astype(BF16)
    cos = cos_ref[...]
    sin = sin_ref[...]
    lane = lax.broadcasted_iota(jnp.int32, (tm, LANES), 1)
    inv_hd = 1.0 / HEAD_DIM

    q = jnp.dot(xn, wq_ref[...], preferred_element_type=F32)
    kv = jnp.dot(xn, wkv_ref[...], preferred_element_type=F32)

    y = jnp.dot(xn, wgd_ref[...], preferred_element_type=F32)
    yd = y + dtb_ref[...]
    softplus = jnp.maximum(yd, 0.0) + jnp.log1p(jnp.exp(-jnp.abs(yd)))
    gd_ref[...] = jnp.where(lane < DT_LANE0, jax.nn.sigmoid(y), softplus)

    qn = qn_ref[...]
    for hh in range(NSA_HEADS):
        s = q[:, hh * LANES:(hh + 1) * LANES]
        ms = jnp.sum(s * s, axis=-1, keepdims=True) * inv_hd
        s = s * lax.rsqrt(ms + EPS) * qn
        s = _rope128(s, cos, sin, lane) * (HEAD_DIM ** -0.5)
        q_ref[:, hh * LANES:(hh + 1) * LANES] = s.astype(BF16)

    u_ref[...] = jnp.dot(xn, wu_ref[...], preferred_element_type=F32)
    z_ref[...] = jnp.dot(xn, wz_ref[...], preferred_element_type=F32)
    xbc_ref[...] = jnp.dot(xn, wx_ref[...], preferred_element_type=F32)

    low = lane < HEAD_DIM
    for j in range(6):
        s = kv[:, j * LANES:(j + 1) * LANES]
        if j % 2 == 0:
            sq = s * s
            s_all = jnp.sum(sq, axis=-1, keepdims=True)
            s_low = jnp.sum(jnp.where(low, sq, 0.0), axis=-1, keepdims=True)
            ms = jnp.where(low, s_low, s_all - s_low) * inv_hd
            s = s * lax.rsqrt(ms + EPS) * kn_ref[j // 2:j // 2 + 1, :]
            s = _rope128(s, cos, sin, lane)
        if not prompt:
            kv_ref[:, j * LANES:(j + 1) * LANES] = s
            continue
        if j < 4:
            rows_ref[0, j] = s
        else:
            wins_ref[0, j - 4] = s
        if j >= 2:
            kvb_ref[:, (j - 2) * LANES:(j - 1) * LANES] = s.astype(BF16)
        else:
            seg_scr[j] = s
            seg_ref[j] = jnp.concatenate(
                [seg_scr[j, pl.ds(jj, tm // CMP_STRIDE, stride=CMP_STRIDE), :].astype(BF16)
                 for jj in range(CMP_STRIDE)], axis=1)


def in_projection(h, lw, cos, sin, seq_len=None):
    m, d = h.shape
    tm = _row_tile(m, 512)
    row = lambda n: pl.BlockSpec((tm, n), lambda i: (i, 0))
    ws = [lw["w_u"], lw["w_q"], lw["w_kv"], lw["w_gd"], lw["w_z"], lw["w_xbc"]]
    tail = [(LANES, F32), (256, F32), (768, F32)]
    out_specs = [row(256), row(NSA_HEADS * LANES)]
    out_shape = [jax.ShapeDtypeStruct((m, 256), F32), jax.ShapeDtypeStruct((m, NSA_HEADS * LANES), BF16)]
    scratch = []
    if seq_len is None:
        out_specs.append(row(6 * LANES))
        out_shape.append(jax.ShapeDtypeStruct((m, 6 * LANES), F32))
    else:
        assert seq_len % tm == 0 and tm % (2 * SUBLANES * CMP_STRIDE) == 0
        tps = seq_len // tm
        per_seq = lambda n: pl.BlockSpec((1, n, tm, LANES), lambda i: (i // tps, 0, i % tps, 0))
        out_specs += [per_seq(4), per_seq(2), row(4 * LANES),
                      pl.BlockSpec((2, tm // CMP_STRIDE, CMP_STRIDE * LANES), lambda i: (0, i, 0))]
        out_shape += [jax.ShapeDtypeStruct((m // seq_len, 4, seq_len, LANES), F32),
                      jax.ShapeDtypeStruct((m // seq_len, 2, seq_len, LANES), F32),
                      jax.ShapeDtypeStruct((m, 4 * LANES), BF16),
                      jax.ShapeDtypeStruct((2, m // CMP_STRIDE, CMP_STRIDE * LANES), BF16)]
        scratch = [pltpu.VMEM((2, tm, LANES), F32)]
    out_specs += [row(n) for n, _ in tail]
    out_shape += [jax.ShapeDtypeStruct((m, n), dt) for n, dt in tail]
    return pl.pallas_call(
        functools.partial(_inproj_kernel, prompt=seq_len is not None),
        grid=(m // tm,),
        in_specs=[row(d), _const_spec((1, d))] + [_resident_spec(w.shape) for w in ws]
                 + [_const_spec((1, LANES)), _const_spec((3, LANES)), _const_spec((1, LANES)),
                    row(LANES), row(LANES)],
        out_specs=out_specs,
        out_shape=out_shape,
        scratch_shapes=scratch,
        compiler_params=_cparams("parallel"),
    )(h, lw["mix_norm"], *ws, lw["q_norm"], lw["k_norm"], lw["dt_bias"], cos, sin)


def _outproj_kernel(h_ref, yp_ref, yn_ref, ys_ref, wp_ref, wn_ref, ws_ref, o_ref):
    acc = jnp.dot(yp_ref[...].astype(BF16), wp_ref[...], preferred_element_type=F32)
    acc = acc + jnp.dot(yn_ref[...].astype(BF16), wn_ref[...], preferred_element_type=F32)
    acc = acc + jnp.dot(ys_ref[...].astype(BF16), ws_ref[...], preferred_element_type=F32)
    o_ref[...] = h_ref[...] + acc


def out_projection(h, y_pool, y_nsa, y_ssm, lw):
    m, d = h.shape
    tm = _row_tile(m, 1024)
    row = lambda n: pl.BlockSpec((tm, n), lambda i: (i, 0))
    ws = [lw["w_out_pool"], lw["w_out_nsa"], lw["w_out_ssm"]]
    return pl.pallas_call(
        _outproj_kernel,
        grid=(m // tm,),
        in_specs=[row(d), row(y_pool.shape[1]), row(y_nsa.shape[1]), row(y_ssm.shape[1])]
                 + [_resident_spec(w.shape) for w in ws],
        out_specs=row(d),
        out_shape=jax.ShapeDtypeStruct((m, d), F32),
        compiler_params=_cparams("parallel"),
    )(h, y_pool, y_nsa, y_ssm, *ws)


def _ple_kernel(h_ref, pe_ref, g_ref, wg_ref, wp_ref, o_ref):
    proj = jnp.dot(pe_ref[...].astype(BF16), wp_ref[...], preferred_element_type=F32)
    h = h_ref[...]
    xn = _rmsnorm(h, g_ref[...]).astype(BF16)
    gate = jax.nn.sigmoid(jnp.dot(xn, wg_ref[...], preferred_element_type=F32))
    o_ref[...] = h + gate * proj


def ple_step(h, pe, lw):
    m, d = h.shape
    tm = _row_tile(m, 1024)
    row = lambda n: pl.BlockSpec((tm, n), lambda i: (i, 0))
    return pl.pallas_call(
        _ple_kernel,
        grid=(m // tm,),
        in_specs=[row(d), row(pe.shape[1]), _const_spec((1, d)),
                  _resident_spec(lw["ple_w_gate"].shape), _resident_spec(lw["ple_w_proj"].shape)],
        out_specs=row(d),
        out_shape=jax.ShapeDtypeStruct((m, d), F32),
        compiler_params=_cparams("parallel"),
    )(h, pe, lw["ple_norm"], lw["ple_w_gate"], lw["ple_w_proj"])


POOL_HALO = 2 * SUBLANES


def _pool_kernel(u_ref, w_ref, sc_ref, y_ref, ext_ref, *, tm):
    c = pl.program_id(1)

    @pl.when(c == 0)
    def _():
        ext_ref[0:POOL_HALO, :] = jnp.zeros((POOL_HALO, ext_ref.shape[1]), F32)

    ext_ref[POOL_HALO:POOL_HALO + tm, :] = u_ref[0]
    lane = lax.broadcasted_iota(jnp.int32, (tm, LANES), 1)
    low = lane < HEAD_DIM
    pos1 = c * tm + lax.broadcasted_iota(jnp.int32, (tm, LANES), 0) + 1
    ds = []
    for slab, (w_lo, w_hi) in enumerate(((POOL_WINDOWS[0], POOL_WINDOWS[1]), (POOL_WINDOWS[2], POOL_WINDOWS[3]))):
        cols = slice(slab * LANES, (slab + 1) * LANES)
        x = ext_ref[POOL_HALO:POOL_HALO + tm, cols]
        run = x
        s_lo = None
        for k in range(1, w_hi):
            run = run + ext_ref[POOL_HALO - k:POOL_HALO - k + tm, cols]
            if k == w_lo - 1:
                s_lo = run
        cnt = jnp.where(low, jnp.minimum(pos1, w_lo), jnp.minimum(pos1, w_hi)).astype(F32)
        ds.append(jnp.where(low, s_lo, run) / cnt - x)
    d = jnp.concatenate(ds, axis=1).astype(BF16)
    y_ref[0] = jnp.dot(d, w_ref[...], preferred_element_type=F32) * sc_ref[...]
    ext_ref[0:POOL_HALO, :] = ext_ref[tm:tm + POOL_HALO, :]


def pool_prompt(u, lw):
    b, t, ch = u.shape
    tm = _row_tile(t, 512)
    return pl.pallas_call(
        functools.partial(_pool_kernel, tm=tm),
        grid=(b, t // tm),
        in_specs=[pl.BlockSpec((1, tm, ch), lambda i, j: (i, j, 0)),
                  _const_spec((ch, ch)), _const_spec((1, ch))],
        out_specs=pl.BlockSpec((1, tm, ch), lambda i, j: (i, j, 0)),
        out_shape=jax.ShapeDtypeStruct((b, t, ch), F32),
        scratch_shapes=[pltpu.VMEM((POOL_HALO + tm, ch), F32)],
        compiler_params=_cparams("parallel", "arbitrary"),
    )(u, lw["pool_w"], lw["pool_scale"])


def _compress_kernel(seg_ref, w_ref, pe_ref, o_ref):
    y = jnp.dot(seg_ref[0], w_ref[0], preferred_element_type=F32)
    pe = jnp.dot(pe_ref[0], w_ref[0], preferred_element_type=F32)
    nseg = y.shape[0]
    second = pltpu.roll(y[:, LANES:], nseg - 1, 0)
    row = lax.broadcasted_iota(jnp.int32, (nseg, LANES), 0)
    second = jnp.where(row < nseg - 1, second, 0.0)
    out = y[:, :LANES] + second + pe[0:1, :LANES] + pe[1:2, LANES:]
    o_ref[0, 0] = out.astype(o_ref.dtype)


def nsa_compress(seg, w, pe):
    _, b, nseg, width = seg.shape
    return pl.pallas_call(
        _compress_kernel,
        grid=(2, b),
        in_specs=[pl.BlockSpec((None, 1, nseg, width), lambda j, i: (j, i, 0, 0)),
                  pl.BlockSpec((1, width, 2 * LANES), lambda j, i: (j, 0, 0)),
                  pl.BlockSpec((1, SUBLANES, width), lambda j, i: (j, 0, 0))],
        out_specs=pl.BlockSpec((1, 1, nseg, LANES), lambda j, i: (j, i, 0, 0)),
        out_shape=jax.ShapeDtypeStruct((2, b, nseg, LANES), BF16),
        compiler_params=_cparams("parallel", "parallel"),
    )(seg, w, pe)


Q_TILE = 256
KEY_CHUNK = 512
SEL_PAD = 128


def _topk_mask_t(scores, n_sel):
    shape = scores[0].shape
    blk = lax.broadcasted_iota(jnp.int32, shape, 0)

    def body(_, carry):
        out = []
        for sc, sel in carry:
            m = jnp.max(sc, axis=0, keepdims=True)
            idx = jnp.min(jnp.where(sc == m, blk, shape[0]), axis=0, keepdims=True)
            hit = blk == idx
            out.append((jnp.where(hit, -jnp.inf, sc), jnp.where(hit, 1.0, sel)))
        return tuple(out)

    done = lax.fori_loop(0, n_sel, body, tuple((sc, jnp.zeros(shape, F32)) for sc in scores))
    return [sel for _, sel in done]


def _softmax_rows(s, mask):
    sm = jnp.where(mask, s, NEG_INF)
    m = jnp.max(sm, axis=-1, keepdims=True)
    e = jnp.exp(sm - m)
    return e / jnp.sum(e, axis=-1, keepdims=True)


def _nsa_kernel(q_ref, gd_ref, kc_ref, vc_ref, ks_ref, vs_ref, kw_ref, vw_ref, e_ref, ovt_ref, o_ref,
                sa_scr, sb_scr, m_scr, acc_scr, *, n_blocks, n_sel, win_len, last_chunk):
    i = pl.program_id(1)
    t0 = i * Q_TILE
    ncp = kc_ref.shape[2]
    qpos = t0 + lax.broadcasted_iota(jnp.int32, (Q_TILE, 1), 0)
    gd = gd_ref[0]

    blk = lax.broadcasted_iota(jnp.int32, (SEL_PAD, Q_TILE), 0)
    qpos_t = t0 + lax.broadcasted_iota(jnp.int32, (SEL_PAD, Q_TILE), 1)
    cur = qpos_t >> 6
    forced = jnp.where(blk == 0, 1.0, jnp.where(blk == cur, 1.0, jnp.where(blk == cur - 1, 1.0, 0.0)))
    valid = blk * SEL_BLOCK <= qpos_t
    real = blk < n_blocks

    n_full = t0 // KEY_CHUNK
    rows = NSA_REP * Q_TILE
    groups = range(KV_GROUPS)
    lane_row = lax.broadcasted_iota(jnp.int32, (1, LANES), 1)
    cend = lax.broadcasted_iota(jnp.int32, (1, ncp), 1) * CMP_STRIDE + (CMP_BLOCK - 1)
    qpos4 = jnp.concatenate([qpos] * NSA_REP, axis=0)
    own = [(lane_row >= g * HEAD_DIM) & (lane_row < (g + 1) * HEAD_DIM) for g in groups]
    den_lane = [(1 - g) * HEAD_DIM for g in groups]
    q_all = [jnp.concatenate([q_ref[0, :, (NSA_REP * g + r) * LANES:(NSA_REP * g + r + 1) * LANES]
                              for r in range(NSA_REP)], axis=0) for g in groups]

    def split_heads(acc):
        return [acc[r * Q_TILE:(r + 1) * Q_TILE] for r in range(NSA_REP)]

    o_cmp, score = [], []
    for g in groups:
        s = lax.dot_general(q_all[g], kc_ref[0, 0], NT_DIMS, preferred_element_type=F32)
        s = jnp.where(cend <= qpos4, s, NEG_INF)
        e = jnp.exp(s - jnp.max(s, axis=-1, keepdims=True))
        has_block = jnp.where(qpos4 >= CMP_BLOCK - 1, 1.0, 0.0)
        p = e * (has_block / jnp.sum(e, axis=-1, keepdims=True))
        o_cmp.append(split_heads(jnp.dot(p.astype(BF16), vc_ref[0, 0], preferred_element_type=F32)))
        psum = p[0:Q_TILE]
        for r in range(1, NSA_REP):
            psum = psum + p[r * Q_TILE:(r + 1) * Q_TILE]
        imp_t = None
        for part in _split3(psum):
            term = lax.dot_general(ovt_ref[...], part, NT_DIMS, preferred_element_type=F32)
            imp_t = term if imp_t is None else imp_t + term
        sc = jnp.where(valid, imp_t + SEL_BONUS * forced, -1.0)
        score.append(jnp.where(real, sc, -jnp.inf))

    sel_t = _topk_mask_t(score, n_sel)
    q_aug = []
    for g in groups:
        bias = jnp.where(sel_t[g].T > 0.5, 0.0, NEG_INF).astype(BF16)
        q_aug.append(jnp.concatenate([jnp.concatenate([bias] * NSA_REP, axis=0), q_all[g]], axis=1))

    def scores(c, g, s_ref):
        k0 = pl.multiple_of(jnp.minimum(c, last_chunk) * KEY_CHUNK, KEY_CHUNK)
        k_aug = jnp.concatenate([e_ref[pl.ds(k0, KEY_CHUNK), :], ks_ref[0, pl.ds(k0, KEY_CHUNK), :]], axis=1)
        s_ref[g] = lax.dot_general(q_aug[g], k_aug, NT_DIMS, preferred_element_type=F32)

    def reduce_chunk(c, g, s_ref, causal):
        k0 = pl.multiple_of(jnp.minimum(c, last_chunk) * KEY_CHUNK, KEY_CHUNK)
        v_one = jnp.where(own[g], vs_ref[0, pl.ds(k0, KEY_CHUNK), :], 1.0).astype(BF16)
        s = s_ref[g]
        if causal:
            kpos = c * KEY_CHUNK + lax.broadcasted_iota(jnp.int32, (1, KEY_CHUNK), 1)
            s = jnp.where(kpos <= qpos4, s, NEG_INF)
        m = m_scr[g]
        m_new = jnp.maximum(m, jnp.max(s, axis=-1, keepdims=True))
        pc = jnp.exp(s - m_new).astype(BF16)
        acc_scr[g] = jnp.exp(m - m_new) * acc_scr[g] + jnp.dot(pc, v_one, preferred_element_type=F32)
        m_scr[g] = m_new

    def pair(p, carry):
        for g in groups:
            scores(2 * p + 1, g, sb_scr)
        for g in groups:
            reduce_chunk(2 * p, g, sa_scr, False)
        for g in groups:
            scores(2 * p + 2, g, sa_scr)
        for g in groups:
            reduce_chunk(2 * p + 1, g, sb_scr, False)
        return carry

    for g in groups:
        m_scr[g] = jnp.full((rows, 1), NEG_INF, F32)
        acc_scr[g] = jnp.zeros((rows, LANES), F32)
        scores(0, g, sa_scr)
    n_pairs = n_full // 2
    lax.fori_loop(0, n_pairs, pair, 0)
    one_more = n_full > 2 * n_pairs

    @pl.when(one_more)
    def _():
        for g in groups:
            scores(2 * n_pairs + 1, g, sb_scr)

    for g in groups:
        reduce_chunk(2 * n_pairs, g, sa_scr, True)

    @pl.when(one_more)
    def _():
        for g in groups:
            reduce_chunk(2 * n_pairs + 1, g, sb_scr, True)

    w0 = pl.multiple_of(jnp.maximum(t0 + Q_TILE - win_len, 0), Q_TILE)
    kw = kw_ref[0, pl.ds(w0, win_len), :]
    dpos = (w0 + lax.broadcasted_iota(jnp.int32, (1, win_len), 1)) - qpos4
    wmask = (dpos + WINDOW).astype(jnp.uint32) <= WINDOW
    o_slc, o_win = [], []
    for g in groups:
        acc = acc_scr[g]
        o_slc.append(split_heads(acc / acc[:, den_lane[g]:den_lane[g] + 1]))
        vw_one = jnp.where(own[g], vw_ref[0, pl.ds(w0, win_len), :], 1.0).astype(BF16)
        s = jnp.where(wmask, lax.dot_general(q_all[g], kw, NT_DIMS, preferred_element_type=F32), NEG_INF)
        pw = jnp.exp(s - jnp.max(s, axis=-1, keepdims=True)).astype(BF16)
        acc = jnp.dot(pw, vw_one, preferred_element_type=F32)
        o_win.append(split_heads(acc / acc[:, den_lane[g]:den_lane[g] + 1]))

    for g in groups:
        for r in range(NSA_REP):
            hh = NSA_REP * g + r
            mix = (gd[:, 3 * hh:3 * hh + 1] * o_cmp[g][r] + gd[:, 3 * hh + 1:3 * hh + 2] * o_slc[g][r]
                   + gd[:, 3 * hh + 2:3 * hh + 3] * o_win[g][r])
            o_ref[0, :, hh * HEAD_DIM:(hh + 1) * HEAD_DIM] = mix[:, g * HEAD_DIM:(g + 1) * HEAD_DIM]


def nsa_prompt(q, gd, cmp, kvb, onehot, ovt):
    b, t, _ = q.shape
    ncp = cmp.shape[2]
    n_blocks = t // SEL_BLOCK
    assert n_blocks <= SEL_PAD and t % KEY_CHUNK == 0
    win_len = min(WINDOW + Q_TILE, t)
    kern = functools.partial(_nsa_kernel, n_blocks=n_blocks, n_sel=min(SEL_TOPN, n_blocks), win_len=win_len,
                             last_chunk=t // KEY_CHUNK - 1)
    rows = NSA_REP * Q_TILE
    slab = lambda j: pl.BlockSpec((1, t, LANES), lambda bi, i, j=j: (bi, 0, j))
    return pl.pallas_call(
        kern,
        grid=(b, t // Q_TILE),
        in_specs=[pl.BlockSpec((1, Q_TILE, NSA_HEADS * LANES), lambda bi, i: (bi, i, 0)),
                  pl.BlockSpec((1, Q_TILE, LANES), lambda bi, i: (bi, i, 0)),
                  pl.BlockSpec((1, 1, ncp, LANES), lambda bi, i: (0, bi, 0, 0)),
                  pl.BlockSpec((1, 1, ncp, LANES), lambda bi, i: (1, bi, 0, 0)),
                  slab(0), slab(1), slab(2), slab(3),
                  _const_spec(onehot.shape), _const_spec(ovt.shape)],
        out_specs=pl.BlockSpec((1, Q_TILE, NSA_HEADS * HEAD_DIM), lambda bi, i: (bi, i, 0)),
        out_shape=jax.ShapeDtypeStruct((b, t, NSA_HEADS * HEAD_DIM), F32),
        scratch_shapes=[pltpu.VMEM((KV_GROUPS, rows, KEY_CHUNK), F32), pltpu.VMEM((KV_GROUPS, rows, KEY_CHUNK), F32),
                        pltpu.VMEM((KV_GROUPS, rows, 1), F32), pltpu.VMEM((KV_GROUPS, rows, LANES), F32)],
        compiler_params=_cparams("parallel", "arbitrary"),
    )(q, gd, cmp, cmp, kvb, kvb, kvb, kvb, onehot, ovt)


def _ssd_kernel(xbc_ref, z_ref, gd_ref, cw_ref, cb_ref, a_ref, dsk_ref, ng_ref,
                y_ref, hfin_ref, ext_ref, h_ref, y_scr):
    c = pl.program_id(1)
    L = SSM_CHUNK
    pd = SSM_HEADS * HEAD_DIM
    gw = SSM_STATE

    @pl.when(c == 0)
    def _():
        ext_ref[0:SUBLANES, :] = jnp.zeros((SUBLANES, ext_ref.shape[1]), F32)
        h_ref[...] = jnp.zeros(h_ref.shape, F32)

    ext_ref[SUBLANES:SUBLANES + L, :] = xbc_ref[0]
    conv = cb_ref[...]
    for k in range(SSM_CONV):
        off = SUBLANES - (SSM_CONV - 1) + k
        conv = conv + ext_ref[off:off + L, :] * cw_ref[k:k + 1, :]
    ext_ref[0:SUBLANES, :] = ext_ref[L:L + SUBLANES, :]
    act = _silu(conv)
    xs = act[:, :pd]
    gd = gd_ref[0]

    ii = lax.broadcasted_iota(jnp.int32, (L, L), 0)
    jj = lax.broadcasted_iota(jnp.int32, (L, L), 1)
    causal = ii >= jj
    tri = jnp.where(causal, 1.0, 0.0).astype(BF16)
    acum = None
    for part in _split3(gd * a_ref[...]):
        term = jnp.dot(tri, part, preferred_element_type=F32)
        acum = term if acum is None else acum + term
    acum_t = acum.T

    hpg = SSM_HEADS // SSM_GROUPS
    bm = [act[:, pd + g * gw:pd + (g + 1) * gw].astype(BF16) for g in range(SSM_GROUPS)]
    cm = [act[:, pd + (SSM_GROUPS + g) * gw:pd + (SSM_GROUPS + g + 1) * gw].astype(BF16) for g in range(SSM_GROUPS)]
    cb = [lax.dot_general(cm[g], bm[g], NT_DIMS, preferred_element_type=F32) for g in range(SSM_GROUPS)]
    h_prev = [h_ref[hh] for hh in range(SSM_HEADS)]
    c_h = [lax.dot_general(cm[hh // hpg], h_prev[hh].astype(BF16), NT_DIMS, preferred_element_type=F32)
           for hh in range(SSM_HEADS)]
    for g in range(SSM_GROUPS):
        for hl in range(hpg):
            hh = g * hpg + hl
            col = acum[:, DT_LANE0 + hh:DT_LANE0 + hh + 1]
            row = acum_t[DT_LANE0 + hh:DT_LANE0 + hh + 1, :]
            last = acum[L - 1:L, DT_LANE0 + hh:DT_LANE0 + hh + 1]
            lmat = jnp.exp(jnp.where(causal, col - row, NEG_INF))
            x_h = xs[:, hh * HEAD_DIM:(hh + 1) * HEAD_DIM]
            xdt = x_h * gd[:, DT_LANE0 + hh:DT_LANE0 + hh + 1]
            y_diag = jnp.dot((cb[g] * lmat).astype(BF16), xdt.astype(BF16), preferred_element_type=F32)
            y_off = c_h[hh] * jnp.exp(col)
            st = lax.dot_general((xdt * jnp.exp(last - col)).astype(BF16), bm[g], TN_DIMS,
                                 preferred_element_type=F32)
            h_ref[hh] = jnp.exp(last) * h_prev[hh] + st
            y_scr[:, hh * HEAD_DIM:(hh + 1) * HEAD_DIM] = (
                y_diag + y_off + dsk_ref[:, hh * HEAD_DIM:(hh + 1) * HEAD_DIM] * x_h)

    y = y_scr[...] * _silu(z_ref[0])
    for g in range(SSM_GROUPS):
        cols = slice(g * gw, (g + 1) * gw)
        yg = y[:, cols]
        ms = jnp.mean(yg * yg, axis=-1, keepdims=True)
        y_ref[0, :, cols] = yg * lax.rsqrt(ms + EPS) * ng_ref[:, cols]

    @pl.when(c == pl.num_programs(1) - 1)
    def _():
        hfin_ref[0] = h_ref[...]


def ssd_prompt(xbc, z, gd, lw):
    b, t, cd = xbc.shape
    L = SSM_CHUNK
    pd = SSM_HEADS * HEAD_DIM
    tile = lambda n: pl.BlockSpec((1, L, n), lambda i, j: (i, j, 0))
    return pl.pallas_call(
        _ssd_kernel,
        grid=(b, t // L),
        in_specs=[tile(cd), tile(pd), tile(LANES),
                  _const_spec((SSM_CONV, cd)), _const_spec((1, cd)), _const_spec((1, LANES)),
                  _const_spec((1, pd)), _const_spec((1, pd))],
        out_specs=[tile(pd), pl.BlockSpec((1, SSM_HEADS, HEAD_DIM, SSM_STATE), lambda i, j: (i, 0, 0, 0))],
        out_shape=[jax.ShapeDtypeStruct((b, t, pd), F32),
                   jax.ShapeDtypeStruct((b, SSM_HEADS, HEAD_DIM, SSM_STATE), F32)],
        scratch_shapes=[pltpu.VMEM((SUBLANES + L, cd), F32),
                        pltpu.VMEM((SSM_HEADS, HEAD_DIM, SSM_STATE), F32),
                        pltpu.VMEM((L, pd), F32)],
        compiler_params=_cparams("parallel", "arbitrary"),
    )(xbc, z, gd, lw["conv_w"], lw["conv_b"], lw["a_row"], lw["d_skip"], lw["ssm_norm"])


SEG_PER_PAGE = PAGE_SIZE // CMP_STRIDE
BLOCKS_PER_PAGE = PAGE_SIZE // SEL_BLOCK
N_PICK = SEL_TOPN - 1
SEQ_PER_STEP = SUBLANES // KV_GROUPS


def _past_compress_kernel(pt_ref, *refs):
    del pt_ref
    n_pages = len(refs) - 4
    pages = refs[:n_pages]
    w_ref, pe_ref, o_ref, x_scr = refs[n_pages:]
    nrow = n_pages * SEG_PER_PAGE
    row = lax.broadcasted_iota(jnp.int32, (nrow, LANES), 0)
    for kv in range(2):
        for k, pg in enumerate(pages):
            x_scr[kv, k * PAGE_SIZE:(k + 1) * PAGE_SIZE, :] = pg[kv].T
    for kv in range(2):
        seg = jnp.concatenate([x_scr[kv, pl.ds(jj, nrow, stride=CMP_STRIDE), :].astype(BF16)
                               for jj in range(CMP_STRIDE)], axis=1)
        y = jnp.dot(jnp.concatenate([seg, pe_ref[kv]], axis=0), w_ref[kv], preferred_element_type=F32)
        second = pltpu.roll(y[:nrow, LANES:], nrow - 1, 0)
        second = jnp.where(row < nrow - 1, second, 0.0)
        out = y[:nrow, :LANES] + second + y[nrow:nrow + 1, :LANES] + y[nrow + 1:nrow + 2, LANES:]
        o_ref[kv] = out.astype(o_ref.dtype)


def past_compress(cache_t, page_table, w_all, pe_all):
    depth = cache_t.shape[1]
    db, n_pages = page_table.shape
    nrow = n_pages * SEG_PER_PAGE
    page_spec = lambda k: pl.BlockSpec(
        (None, None, 2, LANES, PAGE_SIZE), lambda l, b, pt, k=k: (pt[b, k], l, 0, 0, 0))
    grid_spec = pltpu.PrefetchScalarGridSpec(
        num_scalar_prefetch=1,
        grid=(depth, db),
        in_specs=[page_spec(k) for k in range(n_pages)]
                 + [pl.BlockSpec((None,) + w_all.shape[1:], lambda l, b, pt: (l, 0, 0, 0)),
                    pl.BlockSpec((None,) + pe_all.shape[1:], lambda l, b, pt: (l, 0, 0, 0))],
        out_specs=pl.BlockSpec((None, 2, None, nrow, LANES), lambda l, b, pt: (l, 0, b, 0, 0)),
        scratch_shapes=[pltpu.VMEM((2, n_pages * PAGE_SIZE, LANES), F32)],
    )
    return pl.pallas_call(
        _past_compress_kernel,
        grid_spec=grid_spec,
        out_shape=jax.ShapeDtypeStruct((depth, 2, db, nrow, LANES), BF16),
        compiler_params=_cparams("parallel", "parallel"),
    )(page_table, *([cache_t] * n_pages), w_all, pe_all)


def _sample_cmp_kernel(q_ref, kc_ref, vc_ref, ov_ref, ocmp_ref, idx_ref, *, past_len):
    ncp = kc_ref.shape[3]
    cend = lax.broadcasted_iota(jnp.int32, (1, ncp), 1) * CMP_STRIDE + (CMP_BLOCK - 1)
    cmask = cend <= past_len
    hrow = lax.broadcasted_iota(jnp.int32, (NSA_HEADS, 1), 0)
    sums = []
    for sq in range(SEQ_PER_STEP):
        q = q_ref[sq]
        s = lax.dot_general(q, kc_ref[0, 0, sq], NT_DIMS, preferred_element_type=F32)
        p = jnp.where(cmask, _softmax_rows(s, cmask), 0.0)
        ocmp_ref[sq] = jnp.dot(p.astype(BF16), vc_ref[0, 0, sq], preferred_element_type=F32)
        sums += [jnp.sum(jnp.where((hrow // NSA_REP) == g, p, 0.0), axis=0, keepdims=True)
                 for g in range(KV_GROUPS)]
    psum = jnp.concatenate(sums, axis=0)
    imp = None
    for part in _split3(psum):
        term = jnp.dot(part, ov_ref[...], preferred_element_type=F32)
        imp = term if imp is None else imp + term
    n_past = past_len // SEL_BLOCK
    blk = lax.broadcasted_iota(jnp.int32, (SUBLANES, SEL_PAD), 1)
    forced = jnp.where(blk == 0, 1.0, jnp.where(blk == n_past - 1, 1.0, 0.0))
    score = jnp.where(blk < n_past, imp + SEL_BONUS * forced, -jnp.inf)
    col = lax.broadcasted_iota(jnp.int32, (SUBLANES, LANES), 1)
    picked = jnp.zeros((SUBLANES, LANES), jnp.int32)
    for k in range(N_PICK):
        m = jnp.max(score, axis=-1, keepdims=True)
        idx = jnp.min(jnp.where(score == m, blk, SEL_PAD), axis=-1, keepdims=True)
        score = jnp.where(blk == idx, -jnp.inf, score)
        picked = jnp.where(col == k, idx, picked)
    idx_ref[0] = picked


def sample_cmp_select(q8, cmp_past, layer, ov, past_len):
    db = q8.shape[0]
    ncp = cmp_past.shape[3]
    n_past = past_len // SEL_BLOCK
    assert n_past <= SEL_PAD and n_past + 1 > SEL_TOPN and past_len % SEL_BLOCK == 0
    assert db % SEQ_PER_STEP == 0
    o_cmp, picked = pl.pallas_call(
        functools.partial(_sample_cmp_kernel, past_len=past_len),
        grid=(db // SEQ_PER_STEP,),
        in_specs=[pl.BlockSpec((SEQ_PER_STEP, NSA_HEADS, LANES), lambda b: (b, 0, 0)),
                  pl.BlockSpec((1, 1, SEQ_PER_STEP, ncp, LANES), lambda b: (layer, 0, b, 0, 0)),
                  pl.BlockSpec((1, 1, SEQ_PER_STEP, ncp, LANES), lambda b: (layer, 1, b, 0, 0)),
                  _const_spec(ov.shape)],
        out_specs=[pl.BlockSpec((SEQ_PER_STEP, NSA_HEADS, LANES), lambda b: (b, 0, 0)),
                   pl.BlockSpec((1, SUBLANES, LANES), lambda b: (b, 0, 0))],
        out_shape=[jax.ShapeDtypeStruct((db, NSA_HEADS, LANES), F32),
                   jax.ShapeDtypeStruct((db // SEQ_PER_STEP, SUBLANES, LANES), jnp.int32)],
        compiler_params=_cparams("parallel"),
    )(q8, cmp_past, cmp_past, ov)
    return o_cmp, picked.reshape(db, KV_GROUPS, LANES)[:, :, :SEL_TOPN]


def _sample_attn_kernel(idx_ref, pt_ref, *refs):
    del pt_ref
    kpage = refs[:N_PICK]
    vpage = refs[N_PICK:2 * N_PICK]
    q_ref, kv_ref, gd_ref, ocmp_ref, win_ref, o_ref = refs[2 * N_PICK:]
    b = pl.program_id(0)
    g = pl.program_id(1)
    q = q_ref[0]
    qf = q.astype(F32)
    kvn = kv_ref[0]
    lane = lax.broadcasted_iota(jnp.int32, (1, LANES), 1)
    chan = lax.broadcasted_iota(jnp.int32, (LANES, 1), 0)
    tok_half = lax.broadcasted_iota(jnp.int32, (1, PAGE_SIZE), 1) // SEL_BLOCK

    def branch(gg, keys_t, vals_t, tok_masks, j_new):
        own = (lane >= gg * HEAD_DIM) & (lane < (gg + 1) * HEAD_DIM)
        own_t = (chan >= gg * HEAD_DIM) & (chan < (gg + 1) * HEAD_DIM)
        ss = []
        for kt, tm in zip(keys_t, tok_masks):
            s = jnp.dot(q, kt.astype(BF16), preferred_element_type=F32)
            ss.append(s if tm is None else jnp.where(tm, s, NEG_INF))
        k_new = kvn[:, j_new * LANES:(j_new + 1) * LANES].astype(BF16).astype(F32)
        s_new = jnp.sum(qf * k_new, axis=-1, keepdims=True)
        m = s_new
        for s in ss:
            m = jnp.maximum(m, jnp.max(s, axis=-1, keepdims=True))
        v_new = jnp.where(own, kvn[:, (j_new + 1) * LANES:(j_new + 2) * LANES], 1.0).astype(BF16).astype(F32)
        acc = jnp.exp(s_new - m).astype(BF16).astype(F32) * v_new
        for s, vt in zip(ss, vals_t):
            v_one_t = jnp.where(own_t, vt, 1.0).astype(BF16)
            acc = acc + lax.dot_general(jnp.exp(s - m).astype(BF16), v_one_t, NT_DIMS, preferred_element_type=F32)
        return acc

    gd = gd_ref[0]
    ocmp = ocmp_ref[0]
    for gg in range(KV_GROUPS):
        @pl.when(g == gg)
        def _(gg=gg):
            den = (1 - gg) * HEAD_DIM
            masks = [tok_half == idx_ref[b, gg, j] % BLOCKS_PER_PAGE for j in range(N_PICK)]
            acc = branch(gg, [r[...] for r in kpage], [r[...] for r in vpage], masks, 2)
            o_slc = acc / acc[:, den:den + 1]
            acc = branch(gg, [win_ref[0]], [win_ref[1]], [None], 4)
            o_win = acc / acc[:, den:den + 1]
            for r in range(NSA_REP):
                hh = NSA_REP * gg + r
                mix = (gd[:, 3 * hh:3 * hh + 1] * ocmp[hh:hh + 1] + gd[:, 3 * hh + 1:3 * hh + 2] * o_slc[hh:hh + 1]
                       + gd[:, 3 * hh + 2:3 * hh + 3] * o_win[hh:hh + 1])
                o_ref[0, :, hh * HEAD_DIM:(hh + 1) * HEAD_DIM] = mix[:, gg * HEAD_DIM:(gg + 1) * HEAD_DIM]


def sample_attend(q8, kv, gd, ocmp, idx, page_table, cache_t, win_t, layer):
    db = q8.shape[0]
    page_spec = lambda which, j: pl.BlockSpec(
        (None, None, None, LANES, PAGE_SIZE),
        lambda b, g, ix, pt, j=j: (pt[b, ix[b, g, j] // BLOCKS_PER_PAGE], layer, which, 0, 0))
    per_b = lambda shape: pl.BlockSpec((1,) + shape, lambda b, g, ix, pt: (b,) + (0,) * len(shape))
    wlen = win_t.shape[4]
    grid_spec = pltpu.PrefetchScalarGridSpec(
        num_scalar_prefetch=2,
        grid=(db, KV_GROUPS),
        in_specs=[page_spec(2, j) for j in range(N_PICK)] + [page_spec(3, j) for j in range(N_PICK)]
                 + [per_b((NSA_HEADS, LANES)), per_b((1, 6 * LANES)), per_b((1, LANES)), per_b((NSA_HEADS, LANES)),
                    pl.BlockSpec((None, None, 2, LANES, wlen), lambda b, g, ix, pt: (b, layer, 0, 0, 0))],
        out_specs=per_b((1, NSA_HEADS * HEAD_DIM)),
    )
    return pl.pallas_call(
        _sample_attn_kernel,
        grid_spec=grid_spec,
        out_shape=jax.ShapeDtypeStruct((db, 1, NSA_HEADS * HEAD_DIM), F32),
        compiler_params=_cparams("parallel", "arbitrary"),
    )(idx, page_table, *([cache_t] * (2 * N_PICK)), q8, kv, gd, ocmp, win_t)


def _sample_state_kernel(u_ref, xbc_ref, z_ref, gd_ref, pool_ref, conv_ref, h_ref,
                         pw_ref, ps_ref, cw_ref, cb_ref, a_ref, dsk_ref, ng_ref,
                         ypool_ref, yssm_ref, npool_ref, nconv_ref, nh_ref):
    pd = SSM_HEADS * HEAD_DIM
    u = u_ref[0]
    ext = jnp.concatenate([pool_ref[...], u], axis=0)
    row = lax.broadcasted_iota(jnp.int32, ext.shape, 0)
    lane = lax.broadcasted_iota(jnp.int32, (1, ext.shape[1]), 1)
    d = jnp.zeros_like(u)
    for gi, win in enumerate(POOL_WINDOWS):
        s = jnp.sum(jnp.where(row >= POOL_BUF + 1 - win, ext, 0.0), axis=0, keepdims=True)
        d = jnp.where((lane >= gi * HEAD_DIM) & (lane < (gi + 1) * HEAD_DIM), s / float(win) - u, d)
    d8 = jnp.concatenate([d, jnp.zeros((SUBLANES - 1, d.shape[1]), F32)], axis=0).astype(BF16)
    ypool_ref[0] = jnp.dot(d8, pw_ref[...], preferred_element_type=F32)[0:1] * ps_ref[...]
    npool_ref[...] = ext[1:]

    xbc = xbc_ref[0]
    extc = jnp.concatenate([conv_ref[...], xbc], axis=0)
    act = _silu(jnp.sum(extc * cw_ref[...], axis=0, keepdims=True) + cb_ref[...])
    nconv_ref[...] = extc[1:]
    gd = gd_ref[0]
    dec_row = jnp.exp(gd * a_ref[...])
    eye = (lax.broadcasted_iota(jnp.int32, (HEAD_DIM, HEAD_DIM), 0)
           == lax.broadcasted_iota(jnp.int32, (HEAD_DIM, HEAD_DIM), 1))
    ys = []
    for hh in range(SSM_HEADS):
        g = hh // (SSM_HEADS // SSM_GROUPS)
        x_row = act[:, hh * HEAD_DIM:(hh + 1) * HEAD_DIM]
        dt = gd[:, DT_LANE0 + hh:DT_LANE0 + hh + 1]
        xdt_col = jnp.sum(jnp.where(eye, x_row * dt, 0.0), axis=1, keepdims=True)
        b_row = act[:, pd + g * SSM_STATE:pd + (g + 1) * SSM_STATE]
        c_row = act[:, pd + (SSM_GROUPS + g) * SSM_STATE:pd + (SSM_GROUPS + g + 1) * SSM_STATE]
        h_new = dec_row[:, DT_LANE0 + hh:DT_LANE0 + hh + 1] * h_ref[hh] + xdt_col * b_row
        nh_ref[hh] = h_new
        y_col = jnp.sum(h_new * c_row, axis=1, keepdims=True)
        y_row = jnp.sum(jnp.where(eye, y_col, 0.0), axis=0, keepdims=True)
        ys.append(y_row + dsk_ref[:, hh * HEAD_DIM:(hh + 1) * HEAD_DIM] * x_row)
    y = jnp.concatenate(ys, axis=1) * _silu(z_ref[0])
    outs = []
    for g in range(SSM_GROUPS):
        yg = y[:, g * SSM_STATE:(g + 1) * SSM_STATE]
        ms = jnp.mean(yg * yg, axis=-1, keepdims=True)
        outs.append(yg * lax.rsqrt(ms + EPS) * ng_ref[:, g * SSM_STATE:(g + 1) * SSM_STATE])
    yssm_ref[0] = jnp.concatenate(outs, axis=1)


def sample_state_mixers(u, xbc, z, gd, state_pool, state_conv, state_ssm, layer, lw):
    db = u.shape[0]
    per_b = lambda shape: pl.BlockSpec((1,) + shape, lambda b: (b,) + (0,) * len(shape))
    st = lambda shape: pl.BlockSpec((None, None) + shape, lambda b: (b, layer) + (0,) * len(shape))
    new = lambda shape: pl.BlockSpec((None,) + shape, lambda b: (b,) + (0,) * len(shape))
    ps, cs, hs = state_pool.shape[2:], state_conv.shape[2:], state_ssm.shape[2:]
    params = [lw['pool_w'], lw['pool_scale'], lw['conv_w'], lw['conv_b'], lw['a_row'], lw['d_skip'], lw['ssm_norm']]
    return pl.pallas_call(
        _sample_state_kernel,
        grid=(db,),
        in_specs=[per_b(u.shape[1:]), per_b(xbc.shape[1:]), per_b(z.shape[1:]), per_b(gd.shape[1:]),
                  st(ps), st(cs), st(hs)] + [_const_spec(p.shape) for p in params],
        out_specs=[per_b((1, ps[1])), per_b((1, z.shape[2])), new(ps), new(cs), new(hs)],
        out_shape=[jax.ShapeDtypeStruct((db, 1, ps[1]), F32), jax.ShapeDtypeStruct((db, 1, z.shape[2]), F32),
                   jax.ShapeDtypeStruct((db,) + ps, F32), jax.ShapeDtypeStruct((db,) + cs, F32),
                   jax.ShapeDtypeStruct((db,) + hs, F32)],
        compiler_params=_cparams("parallel"),
    )(u, xbc, z, gd, state_pool, state_conv, state_ssm, *params)


def _prep_layer(p, i):
    d = p['w_in'].shape[1]
    w_in = p['w_in'][i]
    o = np.cumsum([0, 256, 512, 768, GATE_LANES, 256, 768, SSM_HEADS])
    w_q = w_in[:, o[1]:o[2]].reshape(d, NSA_HEADS, HEAD_DIM)
    slot = jnp.zeros((d, NSA_HEADS, KV_GROUPS, HEAD_DIM), F32)
    for hh in range(NSA_HEADS):
        slot = slot.at[:, hh, hh // NSA_REP].set(w_q[:, hh])
    w_gd = jnp.zeros((d, LANES), F32)
    w_gd = w_gd.at[:, :GATE_LANES].set(w_in[:, o[3]:o[4]])
    w_gd = w_gd.at[:, DT_LANE0:DT_LANE0 + SSM_HEADS].set(w_in[:, o[6]:o[7]])
    lane_pad = lambda v: jnp.zeros((1, LANES), F32).at[0, DT_LANE0:DT_LANE0 + SSM_HEADS].set(v)
    eye_g = jnp.eye(KV_GROUPS, dtype=F32)
    ratio = CMP_BLOCK // CMP_STRIDE
    cmp_w, cmp_pe = [], []
    for j in range(2):
        w4 = p['nsa_cmp_w'][i, j].reshape(ratio, CMP_STRIDE, HEAD_DIM, HEAD_DIM)
        cmp_w.append(jnp.einsum('rjde,gh->jgdrhe', w4, eye_g).reshape(CMP_STRIDE * 2 * HEAD_DIM, ratio * LANES))
        pe4 = p['nsa_cmp_pe'][i, j].reshape(ratio, CMP_STRIDE, 1, HEAD_DIM)
        pe_rows = jnp.broadcast_to(pe4, (ratio, CMP_STRIDE, KV_GROUPS, HEAD_DIM)).reshape(ratio, -1)
        cmp_pe.append(jnp.zeros((SUBLANES, pe_rows.shape[1]), F32).at[:ratio].set(pe_rows))
    pool_w = jnp.zeros((256, 256), F32)
    for gi in range(len(POOL_WINDOWS)):
        sl = slice(gi * HEAD_DIM, (gi + 1) * HEAD_DIM)
        pool_w = pool_w.at[sl, sl].set(p['pool_w'][i, gi])
    w_out = p['w_out'][i]
    bf = lambda x: x.astype(BF16)
    row = lambda x: x.reshape(1, -1)
    return {
        'ffn1_norm': row(p['ffn1_norm'][i]), 'ffn1_w_gate': bf(p['ffn1_w_gate'][i]),
        'ffn1_w_up': bf(p['ffn1_w_up'][i]), 'ffn1_w_down': bf(p['ffn1_w_down'][i]),
        'ffn2_norm': row(p['ffn2_norm'][i]), 'ffn2_w_gate': bf(p['ffn2_w_gate'][i]),
        'ffn2_w_up': bf(p['ffn2_w_up'][i]), 'ffn2_w_down': bf(p['ffn2_w_down'][i]),
        'mix_norm': row(p['mix_norm'][i]),
        'w_u': bf(w_in[:, o[0]:o[1]]), 'w_q': bf(slot.reshape(d, NSA_HEADS * LANES)),
        'w_kv': bf(w_in[:, o[2]:o[3]]), 'w_gd': bf(w_gd), 'w_z': bf(w_in[:, o[4]:o[5]]),
        'w_xbc': bf(w_in[:, o[5]:o[6]]),
        'q_norm': row(jnp.tile(p['nsa_q_norm'][i], 2)), 'k_norm': jnp.tile(p['nsa_k_norm'][i], (1, 2)),
        'dt_bias': lane_pad(p['ssm_dt_bias'][i]),
        'w_out_pool': bf(w_out[:256]), 'w_out_nsa': bf(w_out[256:768]), 'w_out_ssm': bf(w_out[768:]),
        'pool_w': bf(pool_w), 'pool_scale': row(p['pool_scale'][i]),
        'cmp_w': bf(jnp.stack(cmp_w)), 'cmp_pe': bf(jnp.stack(cmp_pe)),
        'conv_w': p['ssm_conv_w'][i], 'conv_b': row(p['ssm_conv_b'][i]),
        'a_row': lane_pad(-jnp.exp(p['ssm_a_log'][i])),
        'd_skip': row(jnp.repeat(p['ssm_d'][i], HEAD_DIM)), 'ssm_norm': row(p['ssm_norm'][i]),
        'ple_norm': row(p['ple_norm'][i]), 'ple_w_gate': bf(p['ple_w_gate'][i]),
        'ple_w_proj': bf(p['ple_w_proj'][i]),
    }


def _rope_tables(pos):
    half = HEAD_DIM // 2
    inv = ROPE_THETA ** (-jnp.arange(half, dtype=F32) / half)
    ang = pos.astype(F32)[:, None] * inv[None, :]
    cos = jnp.cos(ang)
    sin = jnp.sin(ang)
    return jnp.tile(cos, (1, 4)), jnp.tile(jnp.concatenate([-sin, sin], axis=1), (1, 2))


def _selection_constants(t):
    nseg = t // CMP_STRIDE
    nc = nseg - CMP_BLOCK // CMP_STRIDE + 1
    ns = t // SEL_BLOCK
    c_start = np.arange(nseg) * CMP_STRIDE
    s_start = np.arange(SEL_PAD) * SEL_BLOCK
    ovt = ((c_start[None, :] < s_start[:, None] + SEL_BLOCK) & (c_start[None, :] + CMP_BLOCK > s_start[:, None])
           & (np.arange(nseg)[None, :] < nc) & (np.arange(SEL_PAD)[:, None] < ns))
    onehot = (np.arange(t)[:, None] // SEL_BLOCK) == np.arange(SEL_PAD)[None, :]
    return jnp.asarray(onehot, BF16), jnp.asarray(ovt, BF16)


def _channel_major(cache):
    nd = cache.ndim
    t = jnp.transpose(cache, tuple(range(nd - 3)) + (nd - 2, nd - 1, nd - 3))
    return t.reshape(cache.shape[:nd - 3] + (cache.shape[-2] * cache.shape[-1], cache.shape[-3]))


def _token_layer_front(h, lw, cos, sin, seq_len=None):
    h = ffn_halfstep(h, lw['ffn1_norm'], lw['ffn1_w_gate'], lw['ffn1_w_up'], lw['ffn1_w_down'])
    return (h,) + tuple(in_projection(h, lw, cos, sin, seq_len))


def _token_layer_back(h, y_pool, y_nsa, y_ssm, pe, lw):
    h = out_projection(h, y_pool, y_nsa, y_ssm, lw)
    h = ffn_halfstep(h, lw['ffn2_norm'], lw['ffn2_w_gate'], lw['ffn2_w_up'], lw['ffn2_w_down'])
    return ple_step(h, pe, lw)


def kernel(x_prompt, x_sample, cache_nsa_kv, cache_win_kv, state_pool, state_conv, state_ssm, page_table,
           p_prompt, p_sample, ffn1_norm, ffn1_w_gate, ffn1_w_up, ffn1_w_down, mix_norm, w_in, w_out,
           pool_w, pool_scale, nsa_q_norm, nsa_k_norm, nsa_cmp_pe, nsa_cmp_w, ssm_conv_w, ssm_conv_b,
           ssm_dt_bias, ssm_a_log, ssm_d, ssm_norm, ffn2_norm, ffn2_w_gate, ffn2_w_up, ffn2_w_down,
           ple_norm, ple_w_gate, ple_w_proj):
    params = dict(ffn1_norm=ffn1_norm, ffn1_w_gate=ffn1_w_gate, ffn1_w_up=ffn1_w_up, ffn1_w_down=ffn1_w_down,
                  mix_norm=mix_norm, w_in=w_in, w_out=w_out, pool_w=pool_w, pool_scale=pool_scale,
                  nsa_q_norm=nsa_q_norm, nsa_k_norm=nsa_k_norm, nsa_cmp_pe=nsa_cmp_pe, nsa_cmp_w=nsa_cmp_w,
                  ssm_conv_w=ssm_conv_w, ssm_conv_b=ssm_conv_b, ssm_dt_bias=ssm_dt_bias, ssm_a_log=ssm_a_log,
                  ssm_d=ssm_d, ssm_norm=ssm_norm, ffn2_norm=ffn2_norm, ffn2_w_gate=ffn2_w_gate,
                  ffn2_w_up=ffn2_w_up, ffn2_w_down=ffn2_w_down, ple_norm=ple_norm, ple_w_gate=ple_w_gate,
                  ple_w_proj=ple_w_proj)
    depth = w_in.shape[0]
    b, t, d = x_prompt.shape
    db = x_sample.shape[0]
    past_len = page_table.shape[1] * PAGE_SIZE
    wkeep = min(WINDOW, t)

    cos_p, sin_p = _rope_tables(jnp.tile(jnp.arange(t, dtype=jnp.int32), b))
    cos_s, sin_s = _rope_tables(jnp.full((db,), past_len, jnp.int32))
    onehot, ovt = _selection_constants(t)

    lws = [_prep_layer(params, i) for i in range(depth)]
    cache_t, win_t = _channel_major(cache_nsa_kv), _channel_major(cache_win_kv)
    cmp_past = past_compress(cache_t, page_table, jnp.stack([lw['cmp_w'] for lw in lws]),
                             jnp.stack([lw['cmp_pe'] for lw in lws]))
    ov_past = _selection_constants(past_len)[1].T
    s3 = lambda x: x.reshape(db, 1, x.shape[-1])

    h_p = x_prompt.reshape(b * t, d)
    h_s = x_sample.reshape(db, d)
    st_p = [[] for _ in range(5)]
    st_s = [[] for _ in range(5)]
    for i in range(depth):
        lw = lws[i]
        h_p, u, qp, rows, wins, kvb, seg, gd, z, xbc = _token_layer_front(h_p, lw, cos_p, sin_p, t)
        r3 = lambda x: x.reshape(b, t, x.shape[-1])
        y_pool = pool_prompt(r3(u), lw)
        cmp = nsa_compress(seg.reshape(2, b, t // CMP_STRIDE, CMP_STRIDE * LANES), lw['cmp_w'], lw['cmp_pe'])
        y_nsa = nsa_prompt(r3(qp), r3(gd), cmp, r3(kvb), onehot, ovt)
        y_ssm, h_fin = ssd_prompt(r3(xbc), r3(z), r3(gd), lw)
        h_p = _token_layer_back(h_p, y_pool.reshape(b * t, -1), y_nsa.reshape(b * t, -1),
                                y_ssm.reshape(b * t, -1), p_prompt[i].reshape(b * t, -1), lw)
        st_p[0].append(rows.reshape(b, 4, t, KV_GROUPS, HEAD_DIM))
        st_p[1].append(wins[:, :, t - wkeep:].reshape(b, 2, wkeep, KV_GROUPS, HEAD_DIM))
        st_p[2].append(r3(u)[:, t - POOL_BUF:])
        st_p[3].append(r3(xbc)[:, t - (SSM_CONV - 1):])
        st_p[4].append(h_fin)
        h_s, u, qp, kv, gd, z, xbc = _token_layer_front(h_s, lw, cos_s, sin_s)
        q8 = qp.reshape(db, NSA_HEADS, LANES)
        o_cmp, picked = sample_cmp_select(q8, cmp_past, i, ov_past, past_len)
        y_nsa = sample_attend(q8, s3(kv), s3(gd), o_cmp, picked, page_table, cache_t, win_t, i)
        y_pool, y_ssm, new_pool, new_conv, new_h = sample_state_mixers(
            s3(u), s3(xbc), s3(z), s3(gd), state_pool, state_conv, state_ssm, i, lw)
        h_s = _token_layer_back(h_s, y_pool.reshape(db, -1), y_nsa.reshape(db, -1), y_ssm.reshape(db, -1),
                                p_sample[i].reshape(db, -1), lw)
        kv6 = kv.reshape(db, 6, 1, KV_GROUPS, HEAD_DIM)
        rows = kv6[:, 0:4]
        new_win = jnp.concatenate([cache_win_kv[:, i, :, 1:], kv6[:, 4:6]], axis=2)
        for j, v in enumerate((rows, new_win, new_pool, new_conv, new_h)):
            st_s[j].append(v)
    outs = [h_p.reshape(b, t, d), h_s.reshape(db, 1, d)]
    for j in range(5):
        outs.append(jnp.stack(st_p[j], axis=1))
        outs.append(jnp.stack(st_s[j], axis=1))
    return tuple(outs)
```

```python
import functools

import jax
import jax.numpy as jnp
import numpy as np
from jax import lax
from jax.experimental import pallas as pl
from jax.experimental.pallas import tpu as pltpu

F32 = jnp.float32
BF16 = jnp.bfloat16

POOL_WINDOWS = (2, 4, 8, 16)
POOL_BUF = 15
HEAD_DIM = 64
NSA_HEADS = 8
KV_GROUPS = 2
NSA_REP = NSA_HEADS // KV_GROUPS
CMP_BLOCK = 32
CMP_STRIDE = 16
SEL_BLOCK = 64
SEL_TOPN = 16
WINDOW = 512
SEL_BONUS = 1.0e4
NEG_INF = -1.0e30
SSM_HEADS = 4
SSM_GROUPS = 2
SSM_STATE = 128
SSM_CONV = 4
SSM_CHUNK = 128
ROPE_THETA = 10000.0
EPS = 1e-6
PAGE_SIZE = 128

LANES = 128
SUBLANES = 8
VMEM_LIMIT_BYTES = 56 * 1024 * 1024

GATE_LANES = 3 * NSA_HEADS
DT_LANE0 = 32

Q_SCALE = HEAD_DIM ** -0.5 * float(np.log2(np.e))

NT_DIMS = (((1,), (1,)), ((), ()))
TN_DIMS = (((0,), (0,)), ((), ()))


def _cparams(*sem):
    return pltpu.CompilerParams(dimension_semantics=sem, vmem_limit_bytes=VMEM_LIMIT_BYTES)


def _const_spec(shape):
    nd = len(shape)
    return pl.BlockSpec(shape, lambda *_: (0,) * nd)


def _rmsnorm(x, g):
    ms = jnp.mean(x * x, axis=-1, keepdims=True)
    return x * lax.rsqrt(ms + EPS) * g


def _silu(x):
    return x * jax.nn.sigmoid(x)


def _split3(x):
    hi = x.astype(BF16)
    r = x - hi.astype(F32)
    mid = r.astype(BF16)
    lo = (r - mid.astype(F32)).astype(BF16)
    return hi, mid, lo


def _ffn_kernel(x_ref, g_ref, wg_ref, wu_ref, wd_ref, o_ref, *, n_chunks):
    x = x_ref[...]
    xn = _rmsnorm(x, g_ref[...]).astype(BF16)
    fc = wg_ref.shape[1] // n_chunks
    tot = None
    for c in range(n_chunks):
        sl = slice(c * fc, (c + 1) * fc)
        g = jnp.dot(xn, wg_ref[:, sl], preferred_element_type=F32)
        u = jnp.dot(xn, wu_ref[:, sl], preferred_element_type=F32)
        a = (_silu(g) * u).astype(BF16)
        d = jnp.dot(a, wd_ref[sl, :], preferred_element_type=F32)
        tot = d if tot is None else tot + d
    o_ref[...] = x + 0.5 * tot


def _row_tile(m, pref):
    return pref if m % pref == 0 else m


def _resident_spec(shape):
    nd = len(shape)
    return pl.BlockSpec(shape, lambda *_: (0,) * nd, pipeline_mode=pl.Buffered(1))


def ffn_halfstep(h, g, wg, wu, wd):
    m, d = h.shape
    f = wg.shape[1]
    tm = _row_tile(m, 1024)
    return pl.pallas_call(
        functools.partial(_ffn_kernel, n_chunks=2 if tm < 1024 else f // (2 * LANES)),
        grid=(m // tm,),
        in_specs=[pl.BlockSpec((tm, d), lambda i: (i, 0)),
                  _const_spec((1, d)), _resident_spec((d, f)), _resident_spec((d, f)), _resident_spec((f, d))],
        out_specs=pl.BlockSpec((tm, d), lambda i: (i, 0)),
        out_shape=jax.ShapeDtypeStruct((m, d), F32),
        compiler_params=_cparams("parallel"),
    )(h, g, wg, wu, wd)


def _rope128(x, cos, sin_signed, lane):
    rot = jnp.where((lane & 32) == 0, pltpu.roll(x, 96, 1), pltpu.roll(x, 32, 1))
    return x * cos + rot * sin_signed


def _inproj_kernel(x_ref, g_ref, wu_ref, wq_ref, wkv_ref, wgd_ref, wz_ref, wx_ref,
                   qn_ref, kn_ref, dtb_ref, cos_ref, sin_ref, *out_refs, prompt):
    if prompt:
        u_ref, q_ref, rows_ref, wins_ref, kvb_ref, seg_ref, gd_ref, z_ref, xbc_ref, seg_scr = out_refs
    else:
        u_ref, q_ref, kv_ref, gd_ref, z_ref, xbc_ref = out_refs
    x = x_ref[...]
    tm = x.shape[0]
    xn = _rmsnorm(x, g_ref[...]).astype(BF16)
    cos = cos_ref[...]
    sin = sin_ref[...]
    lane = lax.broadcasted_iota(jnp.int32, (tm, LANES), 1)
    inv_hd = 1.0 / HEAD_DIM

    q = jnp.dot(xn, wq_ref[...], preferred_element_type=F32)
    kv = jnp.dot(xn, wkv_ref[...], preferred_element_type=F32)

    y = jnp.dot(xn, wgd_ref[...], preferred_element_type=F32)
    yd = y + dtb_ref[...]
    softplus = jnp.maximum(yd, 0.0) + jnp.log1p(jnp.exp(-jnp.abs(yd)))
    gd_ref[...] = jnp.where(lane < DT_LANE0, jax.nn.sigmoid(y), softplus)

    qn = qn_ref[...]
    for hh in range(NSA_HEADS):
        s = q[:, hh * LANES:(hh + 1) * LANES]
        ms = jnp.sum(s * s, axis=-1, keepdims=True) * inv_hd
        s = s * lax.rsqrt(ms + EPS) * qn
        s = _rope128(s, cos, sin, lane) * Q_SCALE
        q_ref[:, hh * LANES:(hh + 1) * LANES] = s.astype(BF16)

    u_ref[...] = jnp.dot(xn, wu_ref[...], preferred_element_type=F32)
    z_ref[...] = jnp.dot(xn, wz_ref[...], preferred_element_type=F32)
    xbc_ref[...] = jnp.dot(xn, wx_ref[...], preferred_element_type=F32)

    low = lane < HEAD_DIM
    for j in range(6):
        s = kv[:, j * LANES:(j + 1) * LANES]
        if j % 2 == 0:
            sq = s * s
            s_all = jnp.sum(sq, axis=-1, keepdims=True)
            s_low = jnp.sum(jnp.where(low, sq, 0.0), axis=-1, keepdims=True)
            ms = jnp.where(low, s_low, s_all - s_low) * inv_hd
            s = s * lax.rsqrt(ms + EPS) * kn_ref[j // 2:j // 2 + 1, :]
            s = _rope128(s, cos, sin, lane)
        if not prompt:
            kv_ref[:, j * LANES:(j + 1) * LANES] = s
            continue
        if j < 4:
            rows_ref[0, j] = s
        else:
            wins_ref[0, j - 4] = s
        if j >= 2:
            kvb_ref[:, (j - 2) * LANES:(j - 1) * LANES] = s.astype(BF16)
        else:
            seg_scr[j] = s
            seg_ref[j] = jnp.concatenate(
                [seg_scr[j, pl.ds(jj, tm // CMP_STRIDE, stride=CMP_STRIDE), :].astype(BF16)
                 for jj in range(CMP_STRIDE)], axis=1)


def in_projection(h, lw, cos, sin, seq_len=None):
    m, d = h.shape
    tm = _row_tile(m, 512)
    row = lambda n: pl.BlockSpec((tm, n), lambda i: (i, 0))
    ws = [lw["w_u"], lw["w_q"], lw["w_kv"], lw["w_gd"], lw["w_z"], lw["w_xbc"]]
    tail = [(LANES, F32), (256, F32), (768, F32)]
    out_specs = [row(256), row(NSA_HEADS * LANES)]
    out_shape = [jax.ShapeDtypeStruct((m, 256), F32), jax.ShapeDtypeStruct((m, NSA_HEADS * LANES), BF16)]
    scratch = []
    if seq_len is None:
        out_specs.append(row(6 * LANES))
        out_shape.append(jax.ShapeDtypeStruct((m, 6 * LANES), F32))
    else:
        assert seq_len % tm == 0 and tm % (2 * SUBLANES * CMP_STRIDE) == 0
        tps = seq_len // tm
        per_seq = lambda n: pl.BlockSpec((1, n, tm, LANES), lambda i: (i // tps, 0, i % tps, 0))
        out_specs += [per_seq(4), per_seq(2), row(4 * LANES),
                      pl.BlockSpec((2, tm // CMP_STRIDE, CMP_STRIDE * LANES), lambda i: (0, i, 0))]
        out_shape += [jax.ShapeDtypeStruct((m // seq_len, 4, seq_len, LANES), F32),
                      jax.ShapeDtypeStruct((m // seq_len, 2, seq_len, LANES), F32),
                      jax.ShapeDtypeStruct((m, 4 * LANES), BF16),
                      jax.ShapeDtypeStruct((2, m // CMP_STRIDE, CMP_STRIDE * LANES), BF16)]
        scratch = [pltpu.VMEM((2, tm, LANES), F32)]
    out_specs += [row(n) for n, _ in tail]
    out_shape += [jax.ShapeDtypeStruct((m, n), dt) for n, dt in tail]
    return pl.pallas_call(
        functools.partial(_inproj_kernel, prompt=seq_len is not None),
        grid=(m // tm,),
        in_specs=[row(d), _const_spec((1, d))] + [_resident_spec(w.shape) for w in ws]
                 + [_const_spec((1, LANES)), _const_spec((3, LANES)), _const_spec((1, LANES)),
                    row(LANES), row(LANES)],
        out_specs=out_specs,
        out_shape=out_shape,
        scratch_shapes=scratch,
        compiler_params=_cparams("parallel"),
    )(h, lw["mix_norm"], *ws, lw["q_norm"], lw["k_norm"], lw["dt_bias"], cos, sin)


def _outproj_kernel(h_ref, yp_ref, yn_ref, ys_ref, wp_ref, wn_ref, ws_ref, o_ref):
    acc = jnp.dot(yp_ref[...].astype(BF16), wp_ref[...], preferred_element_type=F32)
    acc = acc + jnp.dot(yn_ref[...].astype(BF16), wn_ref[...], preferred_element_type=F32)
    acc = acc + jnp.dot(ys_ref[...].astype(BF16), ws_ref[...], preferred_element_type=F32)
    o_ref[...] = h_ref[...] + acc


def out_projection(h, y_pool, y_nsa, y_ssm, lw):
    m, d = h.shape
    tm = _row_tile(m, 1024)
    row = lambda n: pl.BlockSpec((tm, n), lambda i: (i, 0))
    ws = [lw["w_out_pool"], lw["w_out_nsa"], lw["w_out_ssm"]]
    return pl.pallas_call(
        _outproj_kernel,
        grid=(m // tm,),
        in_specs=[row(d), row(y_pool.shape[1]), row(y_nsa.shape[1]), row(y_ssm.shape[1])]
                 + [_resident_spec(w.shape) for w in ws],
        out_specs=row(d),
        out_shape=jax.ShapeDtypeStruct((m, d), F32),
        compiler_params=_cparams("parallel"),
    )(h, y_pool, y_nsa, y_ssm, *ws)


def _ple_kernel(h_ref, pe_ref, g_ref, wg_ref, wp_ref, o_ref):
    proj = jnp.dot(pe_ref[...].astype(BF16), wp_ref[...], preferred_element_type=F32)
    h = h_ref[...]
    xn = _rmsnorm(h, g_ref[...]).astype(BF16)
    gate = jax.nn.sigmoid(jnp.dot(xn, wg_ref[...], preferred_element_type=F32))
    o_ref[...] = h + gate * proj


def ple_step(h, pe, lw):
    m, d = h.shape
    tm = _row_tile(m, 1024)
    row = lambda n: pl.BlockSpec((tm, n), lambda i: (i, 0))
    return pl.pallas_call(
        _ple_kernel,
        grid=(m // tm,),
        in_specs=[row(d), row(pe.shape[1]), _const_spec((1, d)),
                  _resident_spec(lw["ple_w_gate"].shape), _resident_spec(lw["ple_w_proj"].shape)],
        out_specs=row(d),
        out_shape=jax.ShapeDtypeStruct((m, d), F32),
        compiler_params=_cparams("parallel"),
    )(h, pe, lw["ple_norm"], lw["ple_w_gate"], lw["ple_w_proj"])


POOL_HALO = 2 * SUBLANES


def _pool_kernel(u_ref, w_ref, sc_ref, y_ref, ext_ref, *, tm):
    c = pl.program_id(1)

    @pl.when(c == 0)
    def _():
        ext_ref[0:POOL_HALO, :] = jnp.zeros((POOL_HALO, ext_ref.shape[1]), F32)

    ext_ref[POOL_HALO:POOL_HALO + tm, :] = u_ref[0]
    lane = lax.broadcasted_iota(jnp.int32, (tm, LANES), 1)
    low = lane < HEAD_DIM
    pos1 = c * tm + lax.broadcasted_iota(jnp.int32, (tm, LANES), 0) + 1
    ds = []
    for slab, (w_lo, w_hi) in enumerate(((POOL_WINDOWS[0], POOL_WINDOWS[1]), (POOL_WINDOWS[2], POOL_WINDOWS[3]))):
        cols = slice(slab * LANES, (slab + 1) * LANES)
        x = ext_ref[POOL_HALO:POOL_HALO + tm, cols]
        run = x
        s_lo = None
        for k in range(1, w_hi):
            run = run + ext_ref[POOL_HALO - k:POOL_HALO - k + tm, cols]
            if k == w_lo - 1:
                s_lo = run
        cnt = jnp.where(low, jnp.minimum(pos1, w_lo), jnp.minimum(pos1, w_hi)).astype(F32)
        ds.append(jnp.where(low, s_lo, run) / cnt - x)
    d = jnp.concatenate(ds, axis=1).astype(BF16)
    y_ref[0] = jnp.dot(d, w_ref[...], preferred_element_type=F32) * sc_ref[...]
    ext_ref[0:POOL_HALO, :] = ext_ref[tm:tm + POOL_HALO, :]


def pool_prompt(u, lw):
    b, t, ch = u.shape
    tm = _row_tile(t, 512)
    return pl.pallas_call(
        functools.partial(_pool_kernel, tm=tm),
        grid=(b, t // tm),
        in_specs=[pl.BlockSpec((1, tm, ch), lambda i, j: (i, j, 0)),
                  _const_spec((ch, ch)), _const_spec((1, ch))],
        out_specs=pl.BlockSpec((1, tm, ch), lambda i, j: (i, j, 0)),
        out_shape=jax.ShapeDtypeStruct((b, t, ch), F32),
        scratch_shapes=[pltpu.VMEM((POOL_HALO + tm, ch), F32)],
        compiler_params=_cparams("parallel", "arbitrary"),
    )(u, lw["pool_w"], lw["pool_scale"])


def _compress_kernel(seg_ref, w_ref, pe_ref, o_ref):
    y = jnp.dot(seg_ref[0], w_ref[0], preferred_element_type=F32)
    pe = jnp.dot(pe_ref[0], w_ref[0], preferred_element_type=F32)
    nseg = y.shape[0]
    second = pltpu.roll(y[:, LANES:], nseg - 1, 0)
    row = lax.broadcasted_iota(jnp.int32, (nseg, LANES), 0)
    second = jnp.where(row < nseg - 1, second, 0.0)
    out = y[:, :LANES] + second + pe[0:1, :LANES] + pe[1:2, LANES:]
    o_ref[0, 0] = out.astype(o_ref.dtype)


def nsa_compress(seg, w, pe):
    _, b, nseg, width = seg.shape
    return pl.pallas_call(
        _compress_kernel,
        grid=(2, b),
        in_specs=[pl.BlockSpec((None, 1, nseg, width), lambda j, i: (j, i, 0, 0)),
                  pl.BlockSpec((1, width, 2 * LANES), lambda j, i: (j, 0, 0)),
                  pl.BlockSpec((1, SUBLANES, width), lambda j, i: (j, 0, 0))],
        out_specs=pl.BlockSpec((1, 1, nseg, LANES), lambda j, i: (j, i, 0, 0)),
        out_shape=jax.ShapeDtypeStruct((2, b, nseg, LANES), BF16),
        compiler_params=_cparams("parallel", "parallel"),
    )(seg, w, pe)


Q_TILE = 256
KEY_CHUNK = 512
SEL_PAD = 128


def _topk_knockout_t(score, n_sel):
    blk = lax.broadcasted_iota(jnp.int32, score.shape, 0)

    def body(_, sc):
        m = jnp.max(sc, axis=0, keepdims=True)
        idx = jnp.min(jnp.where(sc == m, blk, score.shape[0]), axis=0, keepdims=True)
        return jnp.where(blk == idx, -jnp.inf, sc)

    return lax.fori_loop(0, n_sel, body, score)


def _softmax_rows(s, mask):
    sm = jnp.where(mask, s, NEG_INF)
    m = jnp.max(sm, axis=-1, keepdims=True)
    e = jnp.exp2(sm - m)
    return e / jnp.sum(e, axis=-1, keepdims=True)


def _nsa_kernel(q_ref, gd_ref, kc_ref, vc_ref, ks_ref, vs_ref, kw_ref, vw_ref, e_ref, ovt_ref, o_ref,
                sa_scr, sb_scr, m_scr, acc_scr, *, n_blocks, n_sel, win_len, last_chunk):
    i = pl.program_id(1)
    t0 = i * Q_TILE
    ncp = kc_ref.shape[2]
    qpos = t0 + lax.broadcasted_iota(jnp.int32, (Q_TILE, 1), 0)
    gd = gd_ref[0]

    blk = lax.broadcasted_iota(jnp.int32, (SEL_PAD, Q_TILE), 0)
    qpos_t = t0 + lax.broadcasted_iota(jnp.int32, (SEL_PAD, Q_TILE), 1)
    cur = qpos_t >> 6
    forced = jnp.where(blk == 0, 1.0, jnp.where(blk == cur, 1.0, jnp.where(blk == cur - 1, 1.0, 0.0)))
    valid = blk * SEL_BLOCK <= qpos_t
    real = blk < n_blocks

    n_full = t0 // KEY_CHUNK
    rows = NSA_REP * Q_TILE
    groups = range(KV_GROUPS)
    lane_row = lax.broadcasted_iota(jnp.int32, (1, LANES), 1)
    cend = lax.broadcasted_iota(jnp.int32, (1, ncp), 1) * CMP_STRIDE + (CMP_BLOCK - 1)
    qpos4 = jnp.concatenate([qpos] * NSA_REP, axis=0)
    own = [(lane_row >= g * HEAD_DIM) & (lane_row < (g + 1) * HEAD_DIM) for g in groups]
    den_lane = [(1 - g) * HEAD_DIM for g in groups]
    q_all = [jnp.concatenate([q_ref[0, :, (NSA_REP * g + r) * LANES:(NSA_REP * g + r + 1) * LANES]
                              for r in range(NSA_REP)], axis=0) for g in groups]

    def split_heads(acc):
        return [acc[r * Q_TILE:(r + 1) * Q_TILE] for r in range(NSA_REP)]

    o_cmp, score = [], []
    for g in groups:
        s = lax.dot_general(q_all[g], kc_ref[0, 0], NT_DIMS, preferred_element_type=F32)
        s = jnp.where(cend <= qpos4, s, NEG_INF)
        e = jnp.exp2(s - jnp.max(s, axis=-1, keepdims=True))
        has_block = jnp.where(qpos4 >= CMP_BLOCK - 1, 1.0, 0.0)
        p = e * (has_block / jnp.sum(e, axis=-1, keepdims=True))
        o_cmp.append(split_heads(jnp.dot(p.astype(BF16), vc_ref[0, 0], preferred_element_type=F32)))
        psum = p[0:Q_TILE]
        for r in range(1, NSA_REP):
            psum = psum + p[r * Q_TILE:(r + 1) * Q_TILE]
        imp_t = None
        for part in _split3(psum):
            term = lax.dot_general(ovt_ref[...], part, NT_DIMS, preferred_element_type=F32)
            imp_t = term if imp_t is None else imp_t + term
        sc = jnp.where(valid, imp_t + SEL_BONUS * forced, -1.0)
        score.append(jnp.where(real, sc, -jnp.inf))

    knocked = _topk_knockout_t(jnp.concatenate(score, axis=1), n_sel)
    q_aug = []
    for g in groups:
        picked = (knocked[:, g * Q_TILE:(g + 1) * Q_TILE] == -jnp.inf) & real
        bias = jnp.where(picked, 0.0, NEG_INF).T.astype(BF16)
        q_aug.append(jnp.concatenate([jnp.concatenate([bias] * NSA_REP, axis=0), q_all[g]], axis=1))

    def scores(c, g, s_ref):
        k0 = pl.multiple_of(jnp.minimum(c, last_chunk) * KEY_CHUNK, KEY_CHUNK)
        k_aug = jnp.concatenate([e_ref[pl.ds(k0, KEY_CHUNK), :], ks_ref[0, pl.ds(k0, KEY_CHUNK), :]], axis=1)
        s_ref[g] = lax.dot_general(q_aug[g], k_aug, NT_DIMS, preferred_element_type=F32)

    def reduce_chunk(c, g, s_ref, causal):
        k0 = pl.multiple_of(jnp.minimum(c, last_chunk) * KEY_CHUNK, KEY_CHUNK)
        v_one = jnp.where(own[g], vs_ref[0, pl.ds(k0, KEY_CHUNK), :], 1.0).astype(BF16)
        s = s_ref[g]
        if causal:
            kpos = c * KEY_CHUNK + lax.broadcasted_iota(jnp.int32, (1, KEY_CHUNK), 1)
            s = jnp.where(kpos <= qpos4, s, NEG_INF)
        m = m_scr[g]
        m_new = jnp.maximum(m, jnp.max(s, axis=-1, keepdims=True))
        pc = jnp.exp2(s - m_new).astype(BF16)
        acc_scr[g] = jnp.exp2(m - m_new) * acc_scr[g] + jnp.dot(pc, v_one, preferred_element_type=F32)
        m_scr[g] = m_new

    def pair(p, carry):
        for g in groups:
            scores(2 * p + 1, g, sb_scr)
        for g in groups:
            reduce_chunk(2 * p, g, sa_scr, False)
        for g in groups:
            scores(2 * p + 2, g, sa_scr)
        for g in groups:
            reduce_chunk(2 * p + 1, g, sb_scr, False)
        return carry

    for g in groups:
        m_scr[g] = jnp.full((rows, 1), NEG_INF, F32)
        acc_scr[g] = jnp.zeros((rows, LANES), F32)
        scores(0, g, sa_scr)
    n_pairs = n_full // 2
    lax.fori_loop(0, n_pairs, pair, 0)
    one_more = n_full > 2 * n_pairs

    @pl.when(one_more)
    def _():
        for g in groups:
            scores(2 * n_pairs + 1, g, sb_scr)

    for g in groups:
        reduce_chunk(2 * n_pairs, g, sa_scr, True)

    @pl.when(one_more)
    def _():
        for g in groups:
            reduce_chunk(2 * n_pairs + 1, g, sb_scr, True)

    w0 = pl.multiple_of(jnp.maximum(t0 + Q_TILE - win_len, 0), Q_TILE)
    kw = kw_ref[0, pl.ds(w0, win_len), :]
    dpos = (w0 + lax.broadcasted_iota(jnp.int32, (1, win_len), 1)) - qpos4
    wmask = (dpos + WINDOW).astype(jnp.uint32) <= WINDOW
    o_slc, o_win = [], []
    for g in groups:
        acc = acc_scr[g]
        o_slc.append(split_heads(acc / acc[:, den_lane[g]:den_lane[g] + 1]))
        vw_one = jnp.where(own[g], vw_ref[0, pl.ds(w0, win_len), :], 1.0).astype(BF16)
        s = jnp.where(wmask, lax.dot_general(q_all[g], kw, NT_DIMS, preferred_element_type=F32), NEG_INF)
        pw = jnp.exp2(s - jnp.max(s, axis=-1, keepdims=True)).astype(BF16)
        acc = jnp.dot(pw, vw_one, preferred_element_type=F32)
        o_win.append(split_heads(acc / acc[:, den_lane[g]:den_lane[g] + 1]))

    for g in groups:
        for r in range(NSA_REP):
            hh = NSA_REP * g + r
            mix = (gd[:, 3 * hh:3 * hh + 1] * o_cmp[g][r] + gd[:, 3 * hh + 1:3 * hh + 2] * o_slc[g][r]
                   + gd[:, 3 * hh + 2:3 * hh + 3] * o_win[g][r])
            o_ref[0, :, hh * HEAD_DIM:(hh + 1) * HEAD_DIM] = mix[:, g * HEAD_DIM:(g + 1) * HEAD_DIM]


def nsa_prompt(q, gd, cmp, kvb, onehot, ovt):
    b, t, _ = q.shape
    ncp = cmp.shape[2]
    n_blocks = t // SEL_BLOCK
    assert n_blocks <= SEL_PAD and t % KEY_CHUNK == 0
    win_len = min(WINDOW + Q_TILE, t)
    kern = functools.partial(_nsa_kernel, n_blocks=n_blocks, n_sel=min(SEL_TOPN, n_blocks), win_len=win_len,
                             last_chunk=t // KEY_CHUNK - 1)
    rows = NSA_REP * Q_TILE
    slab = lambda j: pl.BlockSpec((1, t, LANES), lambda bi, i, j=j: (bi, 0, j))
    return pl.pallas_call(
        kern,
        grid=(b, t // Q_TILE),
        in_specs=[pl.BlockSpec((1, Q_TILE, NSA_HEADS * LANES), lambda bi, i: (bi, i, 0)),
                  pl.BlockSpec((1, Q_TILE, LANES), lambda bi, i: (bi, i, 0)),
                  pl.BlockSpec((1, 1, ncp, LANES), lambda bi, i: (0, bi, 0, 0)),
                  pl.BlockSpec((1, 1, ncp, LANES), lambda bi, i: (1, bi, 0, 0)),
                  slab(0), slab(1), slab(2), slab(3),
                  _const_spec(onehot.shape), _const_spec(ovt.shape)],
        out_specs=pl.BlockSpec((1, Q_TILE, NSA_HEADS * HEAD_DIM), lambda bi, i: (bi, i, 0)),
        out_shape=jax.ShapeDtypeStruct((b, t, NSA_HEADS * HEAD_DIM), F32),
        scratch_shapes=[pltpu.VMEM((KV_GROUPS, rows, KEY_CHUNK), F32), pltpu.VMEM((KV_GROUPS, rows, KEY_CHUNK), F32),
                        pltpu.VMEM((KV_GROUPS, rows, 1), F32), pltpu.VMEM((KV_GROUPS, rows, LANES), F32)],
        compiler_params=_cparams("parallel", "arbitrary"),
    )(q, gd, cmp, cmp, kvb, kvb, kvb, kvb, onehot, ovt)


def _ssd_kernel(xbc_ref, z_ref, gd_ref, cw_ref, cb_ref, a_ref, dsk_ref, ng_ref,
                y_ref, hfin_ref, ext_ref, h_ref, y_scr):
    c = pl.program_id(1)
    L = SSM_CHUNK
    pd = SSM_HEADS * HEAD_DIM
    gw = SSM_STATE

    @pl.when(c == 0)
    def _():
        ext_ref[0:SUBLANES, :] = jnp.zeros((SUBLANES, ext_ref.shape[1]), F32)
        h_ref[...] = jnp.zeros(h_ref.shape, F32)

    ext_ref[SUBLANES:SUBLANES + L, :] = xbc_ref[0]
    conv = cb_ref[...]
    for k in range(SSM_CONV):
        off = SUBLANES - (SSM_CONV - 1) + k
        conv = conv + ext_ref[off:off + L, :] * cw_ref[k:k + 1, :]
    ext_ref[0:SUBLANES, :] = ext_ref[L:L + SUBLANES, :]
    act = _silu(conv)
    xs = act[:, :pd]
    gd = gd_ref[0]

    ii = lax.broadcasted_iota(jnp.int32, (L, L), 0)
    jj = lax.broadcasted_iota(jnp.int32, (L, L), 1)
    causal = ii >= jj
    tri = jnp.where(causal, 1.0, 0.0).astype(BF16)
    acum = None
    for part in _split3(gd * a_ref[...]):
        term = jnp.dot(tri, part, preferred_element_type=F32)
        acum = term if acum is None else acum + term
    acum_t = acum.T

    hpg = SSM_HEADS // SSM_GROUPS
    bm = [act[:, pd + g * gw:pd + (g + 1) * gw].astype(BF16) for g in range(SSM_GROUPS)]
    cm = [act[:, pd + (SSM_GROUPS + g) * gw:pd + (SSM_GROUPS + g + 1) * gw].astype(BF16) for g in range(SSM_GROUPS)]
    cb = [lax.dot_general(cm[g], bm[g], NT_DIMS, preferred_element_type=F32) for g in range(SSM_GROUPS)]
    h_prev = [h_ref[hh] for hh in range(SSM_HEADS)]
    c_h = [lax.dot_general(cm[hh // hpg], h_prev[hh].astype(BF16), NT_DIMS, preferred_element_type=F32)
           for hh in range(SSM_HEADS)]
    for g in range(SSM_GROUPS):
        for hl in range(hpg):
            hh = g * hpg + hl
            col = acum[:, DT_LANE0 + hh:DT_LANE0 + hh + 1]
            row = acum_t[DT_LANE0 + hh:DT_LANE0 + hh + 1, :]
            last = acum[L - 1:L, DT_LANE0 + hh:DT_LANE0 + hh + 1]
            lmat = jnp.exp(jnp.where(causal, col - row, NEG_INF))
            x_h = xs[:, hh * HEAD_DIM:(hh + 1) * HEAD_DIM]
            xdt = x_h * gd[:, DT_LANE0 + hh:DT_LANE0 + hh + 1]
            y_diag = jnp.dot((cb[g] * lmat).astype(BF16), xdt.astype(BF16), preferred_element_type=F32)
            y_off = c_h[hh] * jnp.exp(col)
            st = lax.dot_general((xdt * jnp.exp(last - col)).astype(BF16), bm[g], TN_DIMS,
                                 preferred_element_type=F32)
            h_ref[hh] = jnp.exp(last) * h_prev[hh] + st
            y_scr[:, hh * HEAD_DIM:(hh + 1) * HEAD_DIM] = (
                y_diag + y_off + dsk_ref[:, hh * HEAD_DIM:(hh + 1) * HEAD_DIM] * x_h)

    y = y_scr[...] * _silu(z_ref[0])
    for g in range(SSM_GROUPS):
        cols = slice(g * gw, (g + 1) * gw)
        yg = y[:, cols]
        ms = jnp.mean(yg * yg, axis=-1, keepdims=True)
        y_ref[0, :, cols] = yg * lax.rsqrt(ms + EPS) * ng_ref[:, cols]

    @pl.when(c == pl.num_programs(1) - 1)
    def _():
        hfin_ref[0] = h_ref[...]


def ssd_prompt(xbc, z, gd, lw):
    b, t, cd = xbc.shape
    L = SSM_CHUNK
    pd = SSM_HEADS * HEAD_DIM
    tile = lambda n: pl.BlockSpec((1, L, n), lambda i, j: (i, j, 0))
    return pl.pallas_call(
        _ssd_kernel,
        grid=(b, t // L),
        in_specs=[tile(cd), tile(pd), tile(LANES),
                  _const_spec((SSM_CONV, cd)), _const_spec((1, cd)), _const_spec((1, LANES)),
                  _const_spec((1, pd)), _const_spec((1, pd))],
        out_specs=[tile(pd), pl.BlockSpec((1, SSM_HEADS, HEAD_DIM, SSM_STATE), lambda i, j: (i, 0, 0, 0))],
        out_shape=[jax.ShapeDtypeStruct((b, t, pd), F32),
                   jax.ShapeDtypeStruct((b, SSM_HEADS, HEAD_DIM, SSM_STATE), F32)],
        scratch_shapes=[pltpu.VMEM((SUBLANES + L, cd), F32),
                        pltpu.VMEM((SSM_HEADS, HEAD_DIM, SSM_STATE), F32),
                        pltpu.VMEM((L, pd), F32)],
        compiler_params=_cparams("parallel", "arbitrary"),
    )(xbc, z, gd, lw["conv_w"], lw["conv_b"], lw["a_row"], lw["d_skip"], lw["ssm_norm"])


SEG_PER_PAGE = PAGE_SIZE // CMP_STRIDE
BLOCKS_PER_PAGE = PAGE_SIZE // SEL_BLOCK
N_PICK = SEL_TOPN - 1
SEQ_PER_STEP = SUBLANES // KV_GROUPS


def _past_compress_kernel(pt_ref, *refs):
    del pt_ref
    n_pages = len(refs) - 4
    pages = refs[:n_pages]
    w_ref, pe_ref, o_ref, x_scr = refs[n_pages:]
    nrow = n_pages * SEG_PER_PAGE
    row = lax.broadcasted_iota(jnp.int32, (nrow, LANES), 0)
    for kv in range(2):
        for k, pg in enumerate(pages):
            x_scr[kv, k * PAGE_SIZE:(k + 1) * PAGE_SIZE, :] = pg[kv].T
    for kv in range(2):
        seg = jnp.concatenate([x_scr[kv, pl.ds(jj, nrow, stride=CMP_STRIDE), :].astype(BF16)
                               for jj in range(CMP_STRIDE)], axis=1)
        y = jnp.dot(jnp.concatenate([seg, pe_ref[kv]], axis=0), w_ref[kv], preferred_element_type=F32)
        second = pltpu.roll(y[:nrow, LANES:], nrow - 1, 0)
        second = jnp.where(row < nrow - 1, second, 0.0)
        out = y[:nrow, :LANES] + second + y[nrow:nrow + 1, :LANES] + y[nrow + 1:nrow + 2, LANES:]
        o_ref[kv] = out.astype(o_ref.dtype)


def past_compress(cache_t, page_table, w_all, pe_all):
    depth = cache_t.shape[1]
    db, n_pages = page_table.shape
    nrow = n_pages * SEG_PER_PAGE
    page_spec = lambda k: pl.BlockSpec(
        (None, None, 2, LANES, PAGE_SIZE), lambda l, b, pt, k=k: (pt[b, k], l, 0, 0, 0))
    grid_spec = pltpu.PrefetchScalarGridSpec(
        num_scalar_prefetch=1,
        grid=(depth, db),
        in_specs=[page_spec(k) for k in range(n_pages)]
                 + [pl.BlockSpec((None,) + w_all.shape[1:], lambda l, b, pt: (l, 0, 0, 0)),
                    pl.BlockSpec((None,) + pe_all.shape[1:], lambda l, b, pt: (l, 0, 0, 0))],
        out_specs=pl.BlockSpec((None, 2, None, nrow, LANES), lambda l, b, pt: (l, 0, b, 0, 0)),
        scratch_shapes=[pltpu.VMEM((2, n_pages * PAGE_SIZE, LANES), F32)],
    )
    return pl.pallas_call(
        _past_compress_kernel,
        grid_spec=grid_spec,
        out_shape=jax.ShapeDtypeStruct((depth, 2, db, nrow, LANES), BF16),
        compiler_params=_cparams("parallel", "parallel"),
    )(page_table, *([cache_t] * n_pages), w_all, pe_all)


def _sample_cmp_kernel(q_ref, kc_ref, vc_ref, ov_ref, ocmp_ref, idx_ref, *, past_len):
    ncp = kc_ref.shape[3]
    cend = lax.broadcasted_iota(jnp.int32, (1, ncp), 1) * CMP_STRIDE + (CMP_BLOCK - 1)
    cmask = cend <= past_len
    hrow = lax.broadcasted_iota(jnp.int32, (NSA_HEADS, 1), 0)
    sums = []
    for sq in range(SEQ_PER_STEP):
        q = q_ref[sq]
        s = lax.dot_general(q, kc_ref[0, 0, sq], NT_DIMS, preferred_element_type=F32)
        p = jnp.where(cmask, _softmax_rows(s, cmask), 0.0)
        ocmp_ref[sq] = jnp.dot(p.astype(BF16), vc_ref[0, 0, sq], preferred_element_type=F32)
        sums += [jnp.sum(jnp.where((hrow // NSA_REP) == g, p, 0.0), axis=0, keepdims=True)
                 for g in range(KV_GROUPS)]
    psum = jnp.concatenate(sums, axis=0)
    imp = None
    for part in _split3(psum):
        term = jnp.dot(part, ov_ref[...], preferred_element_type=F32)
        imp = term if imp is None else imp + term
    n_past = past_len // SEL_BLOCK
    blk = lax.broadcasted_iota(jnp.int32, (SUBLANES, SEL_PAD), 1)
    forced = jnp.where(blk == 0, 1.0, jnp.where(blk == n_past - 1, 1.0, 0.0))
    score = jnp.where(blk < n_past, imp + SEL_BONUS * forced, -jnp.inf)
    col = lax.broadcasted_iota(jnp.int32, (SUBLANES, LANES), 1)
    picked = jnp.zeros((SUBLANES, LANES), jnp.int32)
    for k in range(N_PICK):
        m = jnp.max(score, axis=-1, keepdims=True)
        idx = jnp.min(jnp.where(score == m, blk, SEL_PAD), axis=-1, keepdims=True)
        score = jnp.where(blk == idx, -jnp.inf, score)
        picked = jnp.where(col == k, idx, picked)
    idx_ref[0] = picked


def sample_cmp_select(q8, cmp_past, layer, ov, past_len):
    db = q8.shape[0]
    ncp = cmp_past.shape[3]
    n_past = past_len // SEL_BLOCK
    assert n_past <= SEL_PAD and n_past + 1 > SEL_TOPN and past_len % SEL_BLOCK == 0
    assert db % SEQ_PER_STEP == 0
    o_cmp, picked = pl.pallas_call(
        functools.partial(_sample_cmp_kernel, past_len=past_len),
        grid=(db // SEQ_PER_STEP,),
        in_specs=[pl.BlockSpec((SEQ_PER_STEP, NSA_HEADS, LANES), lambda b: (b, 0, 0)),
                  pl.BlockSpec((1, 1, SEQ_PER_STEP, ncp, LANES), lambda b: (layer, 0, b, 0, 0)),
                  pl.BlockSpec((1, 1, SEQ_PER_STEP, ncp, LANES), lambda b: (layer, 1, b, 0, 0)),
                  _const_spec(ov.shape)],
        out_specs=[pl.BlockSpec((SEQ_PER_STEP, NSA_HEADS, LANES), lambda b: (b, 0, 0)),
                   pl.BlockSpec((1, SUBLANES, LANES), lambda b: (b, 0, 0))],
        out_shape=[jax.ShapeDtypeStruct((db, NSA_HEADS, LANES), F32),
                   jax.ShapeDtypeStruct((db // SEQ_PER_STEP, SUBLANES, LANES), jnp.int32)],
        compiler_params=_cparams("parallel"),
    )(q8, cmp_past, cmp_past, ov)
    return o_cmp, picked.reshape(db, KV_GROUPS, LANES)[:, :, :SEL_TOPN]


def _sample_attn_kernel(idx_ref, pt_ref, *refs):
    del pt_ref
    kpage = refs[:N_PICK]
    vpage = refs[N_PICK:2 * N_PICK]
    q_ref, kv_ref, gd_ref, ocmp_ref, win_ref, o_ref = refs[2 * N_PICK:]
    b = pl.program_id(0)
    g = pl.program_id(1)
    q = q_ref[0]
    qf = q.astype(F32)
    kvn = kv_ref[0]
    lane = lax.broadcasted_iota(jnp.int32, (1, LANES), 1)
    chan = lax.broadcasted_iota(jnp.int32, (LANES, 1), 0)
    tok_half = lax.broadcasted_iota(jnp.int32, (1, PAGE_SIZE), 1) // SEL_BLOCK

    def branch(gg, keys_t, vals_t, tok_masks, j_new):
        own = (lane >= gg * HEAD_DIM) & (lane < (gg + 1) * HEAD_DIM)
        own_t = (chan >= gg * HEAD_DIM) & (chan < (gg + 1) * HEAD_DIM)
        ss = []
        for kt, tm in zip(keys_t, tok_masks):
            s = jnp.dot(q, kt.astype(BF16), preferred_element_type=F32)
            ss.append(s if tm is None else jnp.where(tm, s, NEG_INF))
        k_new = kvn[:, j_new * LANES:(j_new + 1) * LANES].astype(BF16).astype(F32)
        s_new = jnp.sum(qf * k_new, axis=-1, keepdims=True)
        m = s_new
        for s in ss:
            m = jnp.maximum(m, jnp.max(s, axis=-1, keepdims=True))
        v_new = jnp.where(own, kvn[:, (j_new + 1) * LANES:(j_new + 2) * LANES], 1.0).astype(BF16).astype(F32)
        acc = jnp.exp2(s_new - m).astype(BF16).astype(F32) * v_new
        for s, vt in zip(ss, vals_t):
            v_one_t = jnp.where(own_t, vt, 1.0).astype(BF16)
            acc = acc + lax.dot_general(jnp.exp2(s - m).astype(BF16), v_one_t, NT_DIMS, preferred_element_type=F32)
        return acc

    gd = gd_ref[0]
    ocmp = ocmp_ref[0]
    for gg in range(KV_GROUPS):
        @pl.when(g == gg)
        def _(gg=gg):
            den = (1 - gg) * HEAD_DIM
            masks = [tok_half == idx_ref[b, gg, j] % BLOCKS_PER_PAGE for j in range(N_PICK)]
            acc = branch(gg, [r[...] for r in kpage], [r[...] for r in vpage], masks, 2)
            o_slc = acc / acc[:, den:den + 1]
            acc = branch(gg, [win_ref[0]], [win_ref[1]], [None], 4)
            o_win = acc / acc[:, den:den + 1]
            for r in range(NSA_REP):
                hh = NSA_REP * gg + r
                mix = (gd[:, 3 * hh:3 * hh + 1] * ocmp[hh:hh + 1] + gd[:, 3 * hh + 1:3 * hh + 2] * o_slc[hh:hh + 1]
                       + gd[:, 3 * hh + 2:3 * hh + 3] * o_win[hh:hh + 1])
                o_ref[0, :, hh * HEAD_DIM:(hh + 1) * HEAD_DIM] = mix[:, gg * HEAD_DIM:(gg + 1) * HEAD_DIM]


def sample_attend(q8, kv, gd, ocmp, idx, page_table, cache_t, win_t, layer):
    db = q8.shape[0]
    page_spec = lambda which, j: pl.BlockSpec(
        (None, None, None, LANES, PAGE_SIZE),
        lambda b, g, ix, pt, j=j: (pt[b, ix[b, g, j] // BLOCKS_PER_PAGE], layer, which, 0, 0))
    per_b = lambda shape: pl.BlockSpec((1,) + shape, lambda b, g, ix, pt: (b,) + (0,) * len(shape))
    wlen = win_t.shape[4]
    grid_spec = pltpu.PrefetchScalarGridSpec(
        num_scalar_prefetch=2,
        grid=(db, KV_GROUPS),
        in_specs=[page_spec(2, j) for j in range(N_PICK)] + [page_spec(3, j) for j in range(N_PICK)]
                 + [per_b((NSA_HEADS, LANES)), per_b((1, 6 * LANES)), per_b((1, LANES)), per_b((NSA_HEADS, LANES)),
                    pl.BlockSpec((None, None, 2, LANES, wlen), lambda b, g, ix, pt: (b, layer, 0, 0, 0))],
        out_specs=per_b((1, NSA_HEADS * HEAD_DIM)),
    )
    return pl.pallas_call(
        _sample_attn_kernel,
        grid_spec=grid_spec,
        out_shape=jax.ShapeDtypeStruct((db, 1, NSA_HEADS * HEAD_DIM), F32),
        compiler_params=_cparams("parallel", "arbitrary"),
    )(idx, page_table, *([cache_t] * (2 * N_PICK)), q8, kv, gd, ocmp, win_t)


def _sample_state_kernel(u_ref, xbc_ref, z_ref, gd_ref, pool_ref, conv_ref, h_ref,
                         pw_ref, ps_ref, cw_ref, cb_ref, a_ref, dsk_ref, ng_ref,
                         ypool_ref, yssm_ref, npool_ref, nconv_ref, nh_ref):
    pd = SSM_HEADS * HEAD_DIM
    u = u_ref[0]
    ext = jnp.concatenate([pool_ref[...], u], axis=0)
    row = lax.broadcasted_iota(jnp.int32, ext.shape, 0)
    lane = lax.broadcasted_iota(jnp.int32, (1, ext.shape[1]), 1)
    d = jnp.zeros_like(u)
    for gi, win in enumerate(POOL_WINDOWS):
        s = jnp.sum(jnp.where(row >= POOL_BUF + 1 - win, ext, 0.0), axis=0, keepdims=True)
        d = jnp.where((lane >= gi * HEAD_DIM) & (lane < (gi + 1) * HEAD_DIM), s / float(win) - u, d)
    d8 = jnp.concatenate([d, jnp.zeros((SUBLANES - 1, d.shape[1]), F32)], axis=0).astype(BF16)
    ypool_ref[0] = jnp.dot(d8, pw_ref[...], preferred_element_type=F32)[0:1] * ps_ref[...]
    npool_ref[...] = ext[1:]

    xbc = xbc_ref[0]
    extc = jnp.concatenate([conv_ref[...], xbc], axis=0)
    act = _silu(jnp.sum(extc * cw_ref[...], axis=0, keepdims=True) + cb_ref[...])
    nconv_ref[...] = extc[1:]
    gd = gd_ref[0]
    dec_row = jnp.exp(gd * a_ref[...])
    eye = (lax.broadcasted_iota(jnp.int32, (HEAD_DIM, HEAD_DIM), 0)
           == lax.broadcasted_iota(jnp.int32, (HEAD_DIM, HEAD_DIM), 1))
    ys = []
    for hh in range(SSM_HEADS):
        g = hh // (SSM_HEADS // SSM_GROUPS)
        x_row = act[:, hh * HEAD_DIM:(hh + 1) * HEAD_DIM]
        dt = gd[:, DT_LANE0 + hh:DT_LANE0 + hh + 1]
        xdt_col = jnp.sum(jnp.where(eye, x_row * dt, 0.0), axis=1, keepdims=True)
        b_row = act[:, pd + g * SSM_STATE:pd + (g + 1) * SSM_STATE]
        c_row = act[:, pd + (SSM_GROUPS + g) * SSM_STATE:pd + (SSM_GROUPS + g + 1) * SSM_STATE]
        h_new = dec_row[:, DT_LANE0 + hh:DT_LANE0 + hh + 1] * h_ref[hh] + xdt_col * b_row
        nh_ref[hh] = h_new
        y_col = jnp.sum(h_new * c_row, axis=1, keepdims=True)
        y_row = jnp.sum(jnp.where(eye, y_col, 0.0), axis=0, keepdims=True)
        ys.append(y_row + dsk_ref[:, hh * HEAD_DIM:(hh + 1) * HEAD_DIM] * x_row)
    y = jnp.concatenate(ys, axis=1) * _silu(z_ref[0])
    outs = []
    for g in range(SSM_GROUPS):
        yg = y[:, g * SSM_STATE:(g + 1) * SSM_STATE]
        ms = jnp.mean(yg * yg, axis=-1, keepdims=True)
        outs.append(yg * lax.rsqrt(ms + EPS) * ng_ref[:, g * SSM_STATE:(g + 1) * SSM_STATE])
    yssm_ref[0] = jnp.concatenate(outs, axis=1)


def sample_state_mixers(u, xbc, z, gd, state_pool, state_conv, state_ssm, layer, lw):
    db = u.shape[0]
    per_b = lambda shape: pl.BlockSpec((1,) + shape, lambda b: (b,) + (0,) * len(shape))
    st = lambda shape: pl.BlockSpec((None, None) + shape, lambda b: (b, layer) + (0,) * len(shape))
    new = lambda shape: pl.BlockSpec((None,) + shape, lambda b: (b,) + (0,) * len(shape))
    ps, cs, hs = state_pool.shape[2:], state_conv.shape[2:], state_ssm.shape[2:]
    params = [lw['pool_w'], lw['pool_scale'], lw['conv_w'], lw['conv_b'], lw['a_row'], lw['d_skip'], lw['ssm_norm']]
    return pl.pallas_call(
        _sample_state_kernel,
        grid=(db,),
        in_specs=[per_b(u.shape[1:]), per_b(xbc.shape[1:]), per_b(z.shape[1:]), per_b(gd.shape[1:]),
                  st(ps), st(cs), st(hs)] + [_const_spec(p.shape) for p in params],
        out_specs=[per_b((1, ps[1])), per_b((1, z.shape[2])), new(ps), new(cs), new(hs)],
        out_shape=[jax.ShapeDtypeStruct((db, 1, ps[1]), F32), jax.ShapeDtypeStruct((db, 1, z.shape[2]), F32),
                   jax.ShapeDtypeStruct((db,) + ps, F32), jax.ShapeDtypeStruct((db,) + cs, F32),
                   jax.ShapeDtypeStruct((db,) + hs, F32)],
        compiler_params=_cparams("parallel"),
    )(u, xbc, z, gd, state_pool, state_conv, state_ssm, *params)


def _prep_layer(p, i):
    d = p['w_in'].shape[1]
    w_in = p['w_in'][i]
    o = np.cumsum([0, 256, 512, 768, GATE_LANES, 256, 768, SSM_HEADS])
    w_q = w_in[:, o[1]:o[2]].reshape(d, NSA_HEADS, HEAD_DIM)
    slot = jnp.zeros((d, NSA_HEADS, KV_GROUPS, HEAD_DIM), F32)
    for hh in range(NSA_HEADS):
        slot = slot.at[:, hh, hh // NSA_REP].set(w_q[:, hh])
    w_gd = jnp.zeros((d, LANES), F32)
    w_gd = w_gd.at[:, :GATE_LANES].set(w_in[:, o[3]:o[4]])
    w_gd = w_gd.at[:, DT_LANE0:DT_LANE0 + SSM_HEADS].set(w_in[:, o[6]:o[7]])
    lane_pad = lambda v: jnp.zeros((1, LANES), F32).at[0, DT_LANE0:DT_LANE0 + SSM_HEADS].set(v)
    eye_g = jnp.eye(KV_GROUPS, dtype=F32)
    ratio = CMP_BLOCK // CMP_STRIDE
    cmp_w, cmp_pe = [], []
    for j in range(2):
        w4 = p['nsa_cmp_w'][i, j].reshape(ratio, CMP_STRIDE, HEAD_DIM, HEAD_DIM)
        cmp_w.append(jnp.einsum('rjde,gh->jgdrhe', w4, eye_g).reshape(CMP_STRIDE * 2 * HEAD_DIM, ratio * LANES))
        pe4 = p['nsa_cmp_pe'][i, j].reshape(ratio, CMP_STRIDE, 1, HEAD_DIM)
        pe_rows = jnp.broadcast_to(pe4, (ratio, CMP_STRIDE, KV_GROUPS, HEAD_DIM)).reshape(ratio, -1)
        cmp_pe.append(jnp.zeros((SUBLANES, pe_rows.shape[1]), F32).at[:ratio].set(pe_rows))
    pool_w = jnp.zeros((256, 256), F32)
    for gi in range(len(POOL_WINDOWS)):
        sl = slice(gi * HEAD_DIM, (gi + 1) * HEAD_DIM)
        pool_w = pool_w.at[sl, sl].set(p['pool_w'][i, gi])
    w_out = p['w_out'][i]
    bf = lambda x: x.astype(BF16)
    row = lambda x: x.reshape(1, -1)
    return {
        'ffn1_norm': row(p['ffn1_norm'][i]), 'ffn1_w_gate': bf(p['ffn1_w_gate'][i]),
        'ffn1_w_up': bf(p['ffn1_w_up'][i]), 'ffn1_w_down': bf(p['ffn1_w_down'][i]),
        'ffn2_norm': row(p['ffn2_norm'][i]), 'ffn2_w_gate': bf(p['ffn2_w_gate'][i]),
        'ffn2_w_up': bf(p['ffn2_w_up'][i]), 'ffn2_w_down': bf(p['ffn2_w_down'][i]),
        'mix_norm': row(p['mix_norm'][i]),
        'w_u': bf(w_in[:, o[0]:o[1]]), 'w_q': bf(slot.reshape(d, NSA_HEADS * LANES)),
        'w_kv': bf(w_in[:, o[2]:o[3]]), 'w_gd': bf(w_gd), 'w_z': bf(w_in[:, o[4]:o[5]]),
        'w_xbc': bf(w_in[:, o[5]:o[6]]),
        'q_norm': row(jnp.tile(p['nsa_q_norm'][i], 2)), 'k_norm': jnp.tile(p['nsa_k_norm'][i], (1, 2)),
        'dt_bias': lane_pad(p['ssm_dt_bias'][i]),
        'w_out_pool': bf(w_out[:256]), 'w_out_nsa': bf(w_out[256:768]), 'w_out_ssm': bf(w_out[768:]),
        'pool_w': bf(pool_w), 'pool_scale': row(p['pool_scale'][i]),
        'cmp_w': bf(jnp.stack(cmp_w)), 'cmp_pe': bf(jnp.stack(cmp_pe)),
        'conv_w': p['ssm_conv_w'][i], 'conv_b': row(p['ssm_conv_b'][i]),
        'a_row': lane_pad(-jnp.exp(p['ssm_a_log'][i])),
        'd_skip': row(jnp.repeat(p['ssm_d'][i], HEAD_DIM)), 'ssm_norm': row(p['ssm_norm'][i]),
        'ple_norm': row(p['ple_norm'][i]), 'ple_w_gate': bf(p['ple_w_gate'][i]),
        'ple_w_proj': bf(p['ple_w_proj'][i]),
    }


def _rope_tables(pos):
    half = HEAD_DIM // 2
    inv = ROPE_THETA ** (-jnp.arange(half, dtype=F32) / half)
    ang = pos.astype(F32)[:, None] * inv[None, :]
    cos = jnp.cos(ang)
    sin = jnp.sin(ang)
    return jnp.tile(cos, (1, 4)), jnp.tile(jnp.concatenate([-sin, sin], axis=1), (1, 2))


def _selection_constants(t):
    nseg = t // CMP_STRIDE
    nc = nseg - CMP_BLOCK // CMP_STRIDE + 1
    ns = t // SEL_BLOCK
    c_start = np.arange(nseg) * CMP_STRIDE
    s_start = np.arange(SEL_PAD) * SEL_BLOCK
    ovt = ((c_start[None, :] < s_start[:, None] + SEL_BLOCK) & (c_start[None, :] + CMP_BLOCK > s_start[:, None])
           & (np.arange(nseg)[None, :] < nc) & (np.arange(SEL_PAD)[:, None] < ns))
    onehot = (np.arange(t)[:, None] // SEL_BLOCK) == np.arange(SEL_PAD)[None, :]
    return jnp.asarray(onehot, BF16), jnp.asarray(ovt, BF16)


def _channel_major(cache):
    nd = cache.ndim
    t = jnp.transpose(cache, tuple(range(nd - 3)) + (nd - 2, nd - 1, nd - 3))
    return t.reshape(cache.shape[:nd - 3] + (cache.shape[-2] * cache.shape[-1], cache.shape[-3]))


def _token_layer_front(h, lw, cos, sin, seq_len=None):
    h = ffn_halfstep(h, lw['ffn1_norm'], lw['ffn1_w_gate'], lw['ffn1_w_up'], lw['ffn1_w_down'])
    return (h,) + tuple(in_projection(h, lw, cos, sin, seq_len))


def _token_layer_back(h, y_pool, y_nsa, y_ssm, pe, lw):
    h = out_projection(h, y_pool, y_nsa, y_ssm, lw)
    h = ffn_halfstep(h, lw['ffn2_norm'], lw['ffn2_w_gate'], lw['ffn2_w_up'], lw['ffn2_w_down'])
    return ple_step(h, pe, lw)


def kernel(x_prompt, x_sample, cache_nsa_kv, cache_win_kv, state_pool, state_conv, state_ssm, page_table,
           p_prompt, p_sample, ffn1_norm, ffn1_w_gate, ffn1_w_up, ffn1_w_down, mix_norm, w_in, w_out,
           pool_w, pool_scale, nsa_q_norm, nsa_k_norm, nsa_cmp_pe, nsa_cmp_w, ssm_conv_w, ssm_conv_b,
           ssm_dt_bias, ssm_a_log, ssm_d, ssm_norm, ffn2_norm, ffn2_w_gate, ffn2_w_up, ffn2_w_down,
           ple_norm, ple_w_gate, ple_w_proj):
    params = dict(ffn1_norm=ffn1_norm, ffn1_w_gate=ffn1_w_gate, ffn1_w_up=ffn1_w_up, ffn1_w_down=ffn1_w_down,
                  mix_norm=mix_norm, w_in=w_in, w_out=w_out, pool_w=pool_w, pool_scale=pool_scale,
                  nsa_q_norm=nsa_q_norm, nsa_k_norm=nsa_k_norm, nsa_cmp_pe=nsa_cmp_pe, nsa_cmp_w=nsa_cmp_w,
                  ssm_conv_w=ssm_conv_w, ssm_conv_b=ssm_conv_b, ssm_dt_bias=ssm_dt_bias, ssm_a_log=ssm_a_log,
                  ssm_d=ssm_d, ssm_norm=ssm_norm, ffn2_norm=ffn2_norm, ffn2_w_gate=ffn2_w_gate,
                  ffn2_w_up=ffn2_w_up, ffn2_w_down=ffn2_w_down, ple_norm=ple_norm, ple_w_gate=ple_w_gate,
                  ple_w_proj=ple_w_proj)
    depth = w_in.shape[0]
    b, t, d = x_prompt.shape
    db = x_sample.shape[0]
    past_len = page_table.shape[1] * PAGE_SIZE
    wkeep = min(WINDOW, t)

    cos_p, sin_p = _rope_tables(jnp.tile(jnp.arange(t, dtype=jnp.int32), b))
    cos_s, sin_s = _rope_tables(jnp.full((db,), past_len, jnp.int32))
    onehot, ovt = _selection_constants(t)

    lws = [_prep_layer(params, i) for i in range(depth)]
    cache_t, win_t = _channel_major(cache_nsa_kv), _channel_major(cache_win_kv)
    cmp_past = past_compress(cache_t, page_table, jnp.stack([lw['cmp_w'] for lw in lws]),
                             jnp.stack([lw['cmp_pe'] for lw in lws]))
    ov_past = _selection_constants(past_len)[1].T
    s3 = lambda x: x.reshape(db, 1, x.shape[-1])

    h_p = x_prompt.reshape(b * t, d)
    h_s = x_sample.reshape(db, d)
    st_p = [[] for _ in range(5)]
    st_s = [[] for _ in range(5)]
    for i in range(depth):
        lw = lws[i]
        h_p, u, qp, rows, wins, kvb, seg, gd, z, xbc = _token_layer_front(h_p, lw, cos_p, sin_p, t)
        r3 = lambda x: x.reshape(b, t, x.shape[-1])
        y_pool = pool_prompt(r3(u), lw)
        cmp = nsa_compress(seg.reshape(2, b, t // CMP_STRIDE, CMP_STRIDE * LANES), lw['cmp_w'], lw['cmp_pe'])
        y_nsa = nsa_prompt(r3(qp), r3(gd), cmp, r3(kvb), onehot, ovt)
        y_ssm, h_fin = ssd_prompt(r3(xbc), r3(z), r3(gd), lw)
        h_p = _token_layer_back(h_p, y_pool.reshape(b * t, -1), y_nsa.reshape(b * t, -1),
                                y_ssm.reshape(b * t, -1), p_prompt[i].reshape(b * t, -1), lw)
        st_p[0].append(rows.reshape(b, 4, t, KV_GROUPS, HEAD_DIM))
        st_p[1].append(wins[:, :, t - wkeep:].reshape(b, 2, wkeep, KV_GROUPS, HEAD_DIM))
        st_p[2].append(r3(u)[:, t - POOL_BUF:])
        st_p[3].append(r3(xbc)[:, t - (SSM_CONV - 1):])
        st_p[4].append(h_fin)
        h_s, u, qp, kv, gd, z, xbc = _token_layer_front(h_s, lw, cos_s, sin_s)
        q8 = qp.reshape(db, NSA_HEADS, LANES)
        o_cmp, picked = sample_cmp_select(q8, cmp_past, i, ov_past, past_len)
        y_nsa = sample_attend(q8, s3(kv), s3(gd), o_cmp, picked, page_table, cache_t, win_t, i)
        y_pool, y_ssm, new_pool, new_conv, new_h = sample_state_mixers(
            s3(u), s3(xbc), s3(z), s3(gd), state_pool, state_conv, state_ssm, i, lw)
        h_s = _token_layer_back(h_s, y_pool.reshape(db, -1), y_nsa.reshape(db, -1), y_ssm.reshape(db, -1),
                                p_sample[i].reshape(db, -1), lw)
        kv6 = kv.reshape(db, 6, 1, KV_GROUPS, HEAD_DIM)
        rows = kv6[:, 0:4]
        new_win = jnp.concatenate([cache_win_kv[:, i, :, 1:], kv6[:, 4:6]], axis=2)
        for j, v in enumerate((rows, new_win, new_pool, new_conv, new_h)):
            st_s[j].append(v)
    outs = [h_p.reshape(b, t, d), h_s.reshape(db, 1, d)]
    for j in range(5):
        outs.append(jnp.stack(st_p[j], axis=1))
        outs.append(jnp.stack(st_s[j], axis=1))
    return tuple(outs)
```

```python
import functools

import jax
import jax.numpy as jnp
import numpy as np
from jax import lax
from jax.experimental import pallas as pl
from jax.experimental.pallas import tpu as pltpu

F32 = jnp.float32
BF16 = jnp.bfloat16

POOL_WINDOWS = (2, 4, 8, 16)
POOL_BUF = 15
HEAD_DIM = 64
NSA_HEADS = 8
KV_GROUPS = 2
NSA_REP = NSA_HEADS // KV_GROUPS
CMP_BLOCK = 32
CMP_STRIDE = 16
SEL_BLOCK = 64
SEL_TOPN = 16
WINDOW = 512
SEL_BONUS = 1.0e4
NEG_INF = -1.0e30
SSM_HEADS = 4
SSM_GROUPS = 2
SSM_STATE = 128
SSM_CONV = 4
SSM_CHUNK = 128
ROPE_THETA = 10000.0
EPS = 1e-6
PAGE_SIZE = 128
POOL_DIM = len(POOL_WINDOWS) * HEAD_DIM
NSA_DIM = NSA_HEADS * HEAD_DIM
SSM_DIM = SSM_HEADS * HEAD_DIM
SSM_CONV_DIM = SSM_DIM + 2 * SSM_GROUPS * SSM_STATE

LANES = 128
SUBLANES = 8
VMEM_LIMIT_BYTES = 56 * 1024 * 1024

GATE_LANES = 3 * NSA_HEADS
DT_LANE0 = 32

Q_SCALE = HEAD_DIM ** -0.5 * float(np.log2(np.e))

NT_DIMS = (((1,), (1,)), ((), ()))
TN_DIMS = (((0,), (0,)), ((), ()))


def _cparams(*sem):
    return pltpu.CompilerParams(dimension_semantics=sem, vmem_limit_bytes=VMEM_LIMIT_BYTES)


def _const_spec(shape):
    nd = len(shape)
    return pl.BlockSpec(shape, lambda *_: (0,) * nd)


def _rmsnorm(x, g):
    ms = jnp.mean(x * x, axis=-1, keepdims=True)
    return x * lax.rsqrt(ms + EPS) * g


def _silu(x):
    return x * jax.nn.sigmoid(x)


def _split3(x):
    hi = x.astype(BF16)
    r = x - hi.astype(F32)
    mid = r.astype(BF16)
    lo = (r - mid.astype(F32)).astype(BF16)
    return hi, mid, lo


def _ffn_kernel(x_ref, g_ref, wg_ref, wu_ref, wd_ref, o_ref, *, n_chunks):
    x = x_ref[...]
    xn = _rmsnorm(x, g_ref[...]).astype(BF16)
    fc = wg_ref.shape[1] // n_chunks
    tot = None
    for c in range(n_chunks):
        sl = slice(c * fc, (c + 1) * fc)
        g = jnp.dot(xn, wg_ref[:, sl], preferred_element_type=F32)
        u = jnp.dot(xn, wu_ref[:, sl], preferred_element_type=F32)
        a = (_silu(g) * u).astype(BF16)
        d = jnp.dot(a, wd_ref[sl, :], preferred_element_type=F32)
        tot = d if tot is None else tot + d
    o_ref[...] = x + 0.5 * tot


def _row_tile(m, pref):
    return pref if m % pref == 0 else m


def _resident_spec(shape):
    nd = len(shape)
    return pl.BlockSpec(shape, lambda *_: (0,) * nd, pipeline_mode=pl.Buffered(1))


def ffn_halfstep(h, g, wg, wu, wd):
    m, d = h.shape
    f = wg.shape[1]
    tm = _row_tile(m, 1024)
    return pl.pallas_call(
        functools.partial(_ffn_kernel, n_chunks=2 if tm < 1024 else f // (2 * LANES)),
        grid=(m // tm,),
        in_specs=[pl.BlockSpec((tm, d), lambda i: (i, 0)),
                  _const_spec((1, d)), _resident_spec((d, f)), _resident_spec((d, f)), _resident_spec((f, d))],
        out_specs=pl.BlockSpec((tm, d), lambda i: (i, 0)),
        out_shape=jax.ShapeDtypeStruct((m, d), F32),
        compiler_params=_cparams("parallel"),
    )(h, g, wg, wu, wd)


def _rope128(x, cos, sin_signed, lane):
    rot = jnp.where((lane & 32) == 0, pltpu.roll(x, 96, 1), pltpu.roll(x, 32, 1))
    return x * cos + rot * sin_signed


def _inproj_kernel(x_ref, g_ref, wu_ref, wq_ref, wkv_ref, wgd_ref, wz_ref, wx_ref,
                   qn_ref, kn_ref, dtb_ref, cos_ref, sin_ref, *out_refs, prompt):
    if prompt:
        u_ref, q_ref, rows_ref, wins_ref, kvb_ref, seg_ref, gd_ref, z_ref, xbc_ref, seg_scr = out_refs
    else:
        u_ref, q_ref, kv_ref, gd_ref, z_ref, xbc_ref = out_refs
    x = x_ref[...]
    tm = x.shape[0]
    xn = _rmsnorm(x, g_ref[...]).astype(BF16)
    cos = cos_ref[...]
    sin = sin_ref[...]
    lane = lax.broadcasted_iota(jnp.int32, (tm, LANES), 1)
    inv_hd = 1.0 / HEAD_DIM

    q = jnp.dot(xn, wq_ref[...], preferred_element_type=F32)
    kv = jnp.dot(xn, wkv_ref[...], preferred_element_type=F32)

    y = jnp.dot(xn, wgd_ref[...], preferred_element_type=F32)
    yd = y + dtb_ref[...]
    softplus = jnp.maximum(yd, 0.0) + jnp.log1p(jnp.exp(-jnp.abs(yd)))
    gd_ref[...] = jnp.where(lane < DT_LANE0, jax.nn.sigmoid(y), softplus)

    qn = qn_ref[...]
    for hh in range(NSA_HEADS):
        s = q[:, hh * LANES:(hh + 1) * LANES]
        ms = jnp.sum(s * s, axis=-1, keepdims=True) * inv_hd
        s = s * lax.rsqrt(ms + EPS) * qn
        s = _rope128(s, cos, sin, lane) * Q_SCALE
        q_ref[:, hh * LANES:(hh + 1) * LANES] = s.astype(BF16)

    u_ref[...] = jnp.dot(xn, wu_ref[...], preferred_element_type=F32)
    z_ref[...] = jnp.dot(xn, wz_ref[...], preferred_element_type=F32)
    xbc_ref[...] = jnp.dot(xn, wx_ref[...], preferred_element_type=F32)

    low = lane < HEAD_DIM
    for j in range(6):
        s = kv[:, j * LANES:(j + 1) * LANES]
        if j % 2 == 0:
            sq = s * s
            s_all = jnp.sum(sq, axis=-1, keepdims=True)
            s_low = jnp.sum(jnp.where(low, sq, 0.0), axis=-1, keepdims=True)
            ms = jnp.where(low, s_low, s_all - s_low) * inv_hd
            s = s * lax.rsqrt(ms + EPS) * kn_ref[j // 2:j // 2 + 1, :]
            s = _rope128(s, cos, sin, lane)
        if not prompt:
            kv_ref[:, j * LANES:(j + 1) * LANES] = s
            continue
        if j < 4:
            rows_ref[0, j] = s
        else:
            wins_ref[0, j - 4] = s
        if j >= 2:
            kvb_ref[:, (j - 2) * LANES:(j - 1) * LANES] = s.astype(BF16)
        else:
            seg_scr[j] = s
            seg_ref[j] = jnp.concatenate(
                [seg_scr[j, pl.ds(jj, tm // CMP_STRIDE, stride=CMP_STRIDE), :].astype(BF16)
                 for jj in range(CMP_STRIDE)], axis=1)


def in_projection(h, lw, cos, sin, seq_len=None):
    m, d = h.shape
    tm = _row_tile(m, 512)
    row = lambda n: pl.BlockSpec((tm, n), lambda i: (i, 0))
    ws = [lw["w_u"], lw["w_q"], lw["w_kv"], lw["w_gd"], lw["w_z"], lw["w_xbc"]]
    tail = [(LANES, F32), (SSM_DIM, F32), (SSM_CONV_DIM, F32)]
    out_specs = [row(POOL_DIM), row(NSA_HEADS * LANES)]
    out_shape = [jax.ShapeDtypeStruct((m, POOL_DIM), F32), jax.ShapeDtypeStruct((m, NSA_HEADS * LANES), BF16)]
    scratch = []
    if seq_len is None:
        out_specs.append(row(6 * LANES))
        out_shape.append(jax.ShapeDtypeStruct((m, 6 * LANES), F32))
    else:
        assert seq_len % tm == 0 and tm % (2 * SUBLANES * CMP_STRIDE) == 0
        tps = seq_len // tm
        per_seq = lambda n: pl.BlockSpec((1, n, tm, LANES), lambda i: (i // tps, 0, i % tps, 0))
        out_specs += [per_seq(4), per_seq(2), row(4 * LANES),
                      pl.BlockSpec((2, tm // CMP_STRIDE, CMP_STRIDE * LANES), lambda i: (0, i, 0))]
        out_shape += [jax.ShapeDtypeStruct((m // seq_len, 4, seq_len, LANES), F32),
                      jax.ShapeDtypeStruct((m // seq_len, 2, seq_len, LANES), F32),
                      jax.ShapeDtypeStruct((m, 4 * LANES), BF16),
                      jax.ShapeDtypeStruct((2, m // CMP_STRIDE, CMP_STRIDE * LANES), BF16)]
        scratch = [pltpu.VMEM((2, tm, LANES), F32)]
    out_specs += [row(n) for n, _ in tail]
    out_shape += [jax.ShapeDtypeStruct((m, n), dt) for n, dt in tail]
    return pl.pallas_call(
        functools.partial(_inproj_kernel, prompt=seq_len is not None),
        grid=(m // tm,),
        in_specs=[row(d), _const_spec((1, d))] + [_resident_spec(w.shape) for w in ws]
                 + [_const_spec((1, LANES)), _const_spec((3, LANES)), _const_spec((1, LANES)),
                    row(LANES), row(LANES)],
        out_specs=out_specs,
        out_shape=out_shape,
        scratch_shapes=scratch,
        compiler_params=_cparams("parallel"),
    )(h, lw["mix_norm"], *ws, lw["q_norm"], lw["k_norm"], lw["dt_bias"], cos, sin)


def _outproj_kernel(h_ref, yp_ref, yn_ref, ys_ref, wp_ref, wn_ref, ws_ref, o_ref):
    acc = jnp.dot(yp_ref[...].astype(BF16), wp_ref[...], preferred_element_type=F32)
    acc = acc + jnp.dot(yn_ref[...].astype(BF16), wn_ref[...], preferred_element_type=F32)
    acc = acc + jnp.dot(ys_ref[...].astype(BF16), ws_ref[...], preferred_element_type=F32)
    o_ref[...] = h_ref[...] + acc


def out_projection(h, y_pool, y_nsa, y_ssm, lw):
    m, d = h.shape
    tm = _row_tile(m, 1024)
    row = lambda n: pl.BlockSpec((tm, n), lambda i: (i, 0))
    ws = [lw["w_out_pool"], lw["w_out_nsa"], lw["w_out_ssm"]]
    return pl.pallas_call(
        _outproj_kernel,
        grid=(m // tm,),
        in_specs=[row(d), row(y_pool.shape[1]), row(y_nsa.shape[1]), row(y_ssm.shape[1])]
                 + [_resident_spec(w.shape) for w in ws],
        out_specs=row(d),
        out_shape=jax.ShapeDtypeStruct((m, d), F32),
        compiler_params=_cparams("parallel"),
    )(h, y_pool, y_nsa, y_ssm, *ws)


def _ple_kernel(h_ref, pe_ref, g_ref, wg_ref, wp_ref, o_ref):
    proj = jnp.dot(pe_ref[...].astype(BF16), wp_ref[...], preferred_element_type=F32)
    h = h_ref[...]
    xn = _rmsnorm(h, g_ref[...]).astype(BF16)
    gate = jax.nn.sigmoid(jnp.dot(xn, wg_ref[...], preferred_element_type=F32))
    o_ref[...] = h + gate * proj


def ple_step(h, pe, lw):
    m, d = h.shape
    tm = _row_tile(m, 1024)
    row = lambda n: pl.BlockSpec((tm, n), lambda i: (i, 0))
    return pl.pallas_call(
        _ple_kernel,
        grid=(m // tm,),
        in_specs=[row(d), row(pe.shape[1]), _const_spec((1, d)),
                  _resident_spec(lw["ple_w_gate"].shape), _resident_spec(lw["ple_w_proj"].shape)],
        out_specs=row(d),
        out_shape=jax.ShapeDtypeStruct((m, d), F32),
        compiler_params=_cparams("parallel"),
    )(h, pe, lw["ple_norm"], lw["ple_w_gate"], lw["ple_w_proj"])


POOL_HALO = 2 * SUBLANES


def _pool_kernel(u_ref, w_ref, sc_ref, y_ref, ext_ref, *, tm):
    c = pl.program_id(1)

    @pl.when(c == 0)
    def _():
        ext_ref[0:POOL_HALO, :] = jnp.zeros((POOL_HALO, ext_ref.shape[1]), F32)

    ext_ref[POOL_HALO:POOL_HALO + tm, :] = u_ref[0]
    lane = lax.broadcasted_iota(jnp.int32, (tm, LANES), 1)
    low = lane < HEAD_DIM
    pos1 = c * tm + lax.broadcasted_iota(jnp.int32, (tm, LANES), 0) + 1
    ds = []
    for slab, (w_lo, w_hi) in enumerate(((POOL_WINDOWS[0], POOL_WINDOWS[1]), (POOL_WINDOWS[2], POOL_WINDOWS[3]))):
        cols = slice(slab * LANES, (slab + 1) * LANES)
        x = ext_ref[POOL_HALO:POOL_HALO + tm, cols]
        run = x
        s_lo = None
        for k in range(1, w_hi):
            run = run + ext_ref[POOL_HALO - k:POOL_HALO - k + tm, cols]
            if k == w_lo - 1:
                s_lo = run
        cnt = jnp.where(low, jnp.minimum(pos1, w_lo), jnp.minimum(pos1, w_hi)).astype(F32)
        ds.append(jnp.where(low, s_lo, run) / cnt - x)
    d = jnp.concatenate(ds, axis=1).astype(BF16)
    y_ref[0] = jnp.dot(d, w_ref[...], preferred_element_type=F32) * sc_ref[...]
    ext_ref[0:POOL_HALO, :] = ext_ref[tm:tm + POOL_HALO, :]


def pool_prompt(u, lw):
    b, t, ch = u.shape
    tm = _row_tile(t, 512)
    return pl.pallas_call(
        functools.partial(_pool_kernel, tm=tm),
        grid=(b, t // tm),
        in_specs=[pl.BlockSpec((1, tm, ch), lambda i, j: (i, j, 0)),
                  _const_spec((ch, ch)), _const_spec((1, ch))],
        out_specs=pl.BlockSpec((1, tm, ch), lambda i, j: (i, j, 0)),
        out_shape=jax.ShapeDtypeStruct((b, t, ch), F32),
        scratch_shapes=[pltpu.VMEM((POOL_HALO + tm, ch), F32)],
        compiler_params=_cparams("parallel", "arbitrary"),
    )(u, lw["pool_w"], lw["pool_scale"])


def _compress_kernel(seg_ref, w_ref, pe_ref, o_ref):
    y = jnp.dot(seg_ref[0], w_ref[0], preferred_element_type=F32)
    pe = jnp.dot(pe_ref[0], w_ref[0], preferred_element_type=F32)
    nseg = y.shape[0]
    second = pltpu.roll(y[:, LANES:], nseg - 1, 0)
    row = lax.broadcasted_iota(jnp.int32, (nseg, LANES), 0)
    second = jnp.where(row < nseg - 1, second, 0.0)
    out = y[:, :LANES] + second + pe[0:1, :LANES] + pe[1:2, LANES:]
    o_ref[0, 0] = out.astype(o_ref.dtype)


def nsa_compress(seg, w, pe):
    _, b, nseg, width = seg.shape
    return pl.pallas_call(
        _compress_kernel,
        grid=(2, b),
        in_specs=[pl.BlockSpec((None, 1, nseg, width), lambda j, i: (j, i, 0, 0)),
                  pl.BlockSpec((1, width, 2 * LANES), lambda j, i: (j, 0, 0)),
                  pl.BlockSpec((1, SUBLANES, width), lambda j, i: (j, 0, 0))],
        out_specs=pl.BlockSpec((1, 1, nseg, LANES), lambda j, i: (j, i, 0, 0)),
        out_shape=jax.ShapeDtypeStruct((2, b, nseg, LANES), BF16),
        compiler_params=_cparams("parallel", "parallel"),
    )(seg, w, pe)


Q_TILE = 256
KEY_CHUNK = 512
SEL_PAD = 128


def _topk_knockout_t(score, n_sel):
    blk = lax.broadcasted_iota(jnp.int32, score.shape, 0)

    def body(_, sc):
        m = jnp.max(sc, axis=0, keepdims=True)
        idx = jnp.min(jnp.where(sc == m, blk, score.shape[0]), axis=0, keepdims=True)
        return jnp.where(blk == idx, -jnp.inf, sc)

    return lax.fori_loop(0, n_sel, body, score)


def _softmax_rows(s, mask):
    sm = jnp.where(mask, s, NEG_INF)
    m = jnp.max(sm, axis=-1, keepdims=True)
    e = jnp.exp2(sm - m)
    return e / jnp.sum(e, axis=-1, keepdims=True)


def _nsa_kernel(q_ref, gd_ref, kc_ref, vc_ref, ks_ref, vs_ref, kw_ref, vw_ref, e_ref, ovt_ref, o_ref,
                sa_scr, sb_scr, m_scr, acc_scr, *, n_blocks, n_sel, win_len, last_chunk):
    i = pl.program_id(1)
    t0 = i * Q_TILE
    ncp = kc_ref.shape[2]
    qpos = t0 + lax.broadcasted_iota(jnp.int32, (Q_TILE, 1), 0)
    gd = gd_ref[0]

    blk = lax.broadcasted_iota(jnp.int32, (SEL_PAD, Q_TILE), 0)
    qpos_t = t0 + lax.broadcasted_iota(jnp.int32, (SEL_PAD, Q_TILE), 1)
    cur = qpos_t // SEL_BLOCK
    forced = jnp.where(blk == 0, 1.0, jnp.where(blk == cur, 1.0, jnp.where(blk == cur - 1, 1.0, 0.0)))
    valid = blk * SEL_BLOCK <= qpos_t
    real = blk < n_blocks

    n_full = t0 // KEY_CHUNK
    rows = NSA_REP * Q_TILE
    groups = range(KV_GROUPS)
    lane_row = lax.broadcasted_iota(jnp.int32, (1, LANES), 1)
    cend = lax.broadcasted_iota(jnp.int32, (1, ncp), 1) * CMP_STRIDE + (CMP_BLOCK - 1)
    qpos4 = jnp.concatenate([qpos] * NSA_REP, axis=0)
    own = [(lane_row >= g * HEAD_DIM) & (lane_row < (g + 1) * HEAD_DIM) for g in groups]
    den_lane = [(1 - g) * HEAD_DIM for g in groups]
    q_all = [jnp.concatenate([q_ref[0, :, (NSA_REP * g + r) * LANES:(NSA_REP * g + r + 1) * LANES]
                              for r in range(NSA_REP)], axis=0) for g in groups]

    def split_heads(acc):
        return [acc[r * Q_TILE:(r + 1) * Q_TILE] for r in range(NSA_REP)]

    o_cmp, score = [], []
    for g in groups:
        s = lax.dot_general(q_all[g], kc_ref[0, 0], NT_DIMS, preferred_element_type=F32)
        s = jnp.where(cend <= qpos4, s, NEG_INF)
        e = jnp.exp2(s - jnp.max(s, axis=-1, keepdims=True))
        has_block = jnp.where(qpos4 >= CMP_BLOCK - 1, 1.0, 0.0)
        p = e * (has_block / jnp.sum(e, axis=-1, keepdims=True))
        o_cmp.append(split_heads(jnp.dot(p.astype(BF16), vc_ref[0, 0], preferred_element_type=F32)))
        psum = p[0:Q_TILE]
        for r in range(1, NSA_REP):
            psum = psum + p[r * Q_TILE:(r + 1) * Q_TILE]
        imp_t = None
        for part in _split3(psum):
            term = lax.dot_general(ovt_ref[...], part, NT_DIMS, preferred_element_type=F32)
            imp_t = term if imp_t is None else imp_t + term
        sc = jnp.where(valid, imp_t + SEL_BONUS * forced, -1.0)
        score.append(jnp.where(real, sc, -jnp.inf))

    knocked = _topk_knockout_t(jnp.concatenate(score, axis=1), n_sel)
    q_aug = []
    for g in groups:
        picked = (knocked[:, g * Q_TILE:(g + 1) * Q_TILE] == -jnp.inf) & real
        bias = jnp.where(picked, 0.0, NEG_INF).T.astype(BF16)
        q_aug.append(jnp.concatenate([jnp.concatenate([bias] * NSA_REP, axis=0), q_all[g]], axis=1))

    def scores(c, g, s_ref):
        k0 = pl.multiple_of(jnp.minimum(c, last_chunk) * KEY_CHUNK, KEY_CHUNK)
        k_aug = jnp.concatenate([e_ref[pl.ds(k0, KEY_CHUNK), :], ks_ref[0, pl.ds(k0, KEY_CHUNK), :]], axis=1)
        s_ref[g] = lax.dot_general(q_aug[g], k_aug, NT_DIMS, preferred_element_type=F32)

    def reduce_chunk(c, g, s_ref, causal):
        k0 = pl.multiple_of(jnp.minimum(c, last_chunk) * KEY_CHUNK, KEY_CHUNK)
        v_one = jnp.where(own[g], vs_ref[0, pl.ds(k0, KEY_CHUNK), :], 1.0).astype(BF16)
        s = s_ref[g]
        if causal:
            kpos = c * KEY_CHUNK + lax.broadcasted_iota(jnp.int32, (1, KEY_CHUNK), 1)
            s = jnp.where(kpos <= qpos4, s, NEG_INF)
        m = m_scr[g]
        m_new = jnp.maximum(m, jnp.max(s, axis=-1, keepdims=True))
        pc = jnp.exp2(s - m_new).astype(BF16)
        acc_scr[g] = jnp.exp2(m - m_new) * acc_scr[g] + jnp.dot(pc, v_one, preferred_element_type=F32)
        m_scr[g] = m_new

    def pair(p, carry):
        for g in groups:
            scores(2 * p + 1, g, sb_scr)
        for g in groups:
            reduce_chunk(2 * p, g, sa_scr, False)
        for g in groups:
            scores(2 * p + 2, g, sa_scr)
        for g in groups:
            reduce_chunk(2 * p + 1, g, sb_scr, False)
        return carry

    for g in groups:
        m_scr[g] = jnp.full((rows, 1), NEG_INF, F32)
        acc_scr[g] = jnp.zeros((rows, LANES), F32)
        scores(0, g, sa_scr)
    n_pairs = n_full // 2
    lax.fori_loop(0, n_pairs, pair, 0)
    one_more = n_full > 2 * n_pairs

    @pl.when(one_more)
    def _():
        for g in groups:
            scores(2 * n_pairs + 1, g, sb_scr)

    for g in groups:
        reduce_chunk(2 * n_pairs, g, sa_scr, True)

    @pl.when(one_more)
    def _():
        for g in groups:
            reduce_chunk(2 * n_pairs + 1, g, sb_scr, True)

    w0 = pl.multiple_of(jnp.maximum(t0 + Q_TILE - win_len, 0), Q_TILE)
    kw = kw_ref[0, pl.ds(w0, win_len), :]
    dpos = (w0 + lax.broadcasted_iota(jnp.int32, (1, win_len), 1)) - qpos4
    wmask = (dpos + WINDOW).astype(jnp.uint32) <= WINDOW
    o_slc, o_win = [], []
    for g in groups:
        acc = acc_scr[g]
        o_slc.append(split_heads(acc / acc[:, den_lane[g]:den_lane[g] + 1]))
        vw_one = jnp.where(own[g], vw_ref[0, pl.ds(w0, win_len), :], 1.0).astype(BF16)
        s = jnp.where(wmask, lax.dot_general(q_all[g], kw, NT_DIMS, preferred_element_type=F32), NEG_INF)
        pw = jnp.exp2(s - jnp.max(s, axis=-1, keepdims=True)).astype(BF16)
        acc = jnp.dot(pw, vw_one, preferred_element_type=F32)
        o_win.append(split_heads(acc / acc[:, den_lane[g]:den_lane[g] + 1]))

    for g in groups:
        for r in range(NSA_REP):
            hh = NSA_REP * g + r
            mix = (gd[:, 3 * hh:3 * hh + 1] * o_cmp[g][r] + gd[:, 3 * hh + 1:3 * hh + 2] * o_slc[g][r]
                   + gd[:, 3 * hh + 2:3 * hh + 3] * o_win[g][r])
            o_ref[0, :, hh * HEAD_DIM:(hh + 1) * HEAD_DIM] = mix[:, g * HEAD_DIM:(g + 1) * HEAD_DIM]


def nsa_prompt(q, gd, cmp, kvb, onehot, ovt):
    b, t, _ = q.shape
    ncp = cmp.shape[2]
    n_blocks = t // SEL_BLOCK
    assert n_blocks <= SEL_PAD and t % KEY_CHUNK == 0
    win_len = min(WINDOW + Q_TILE, t)
    kern = functools.partial(_nsa_kernel, n_blocks=n_blocks, n_sel=min(SEL_TOPN, n_blocks), win_len=win_len,
                             last_chunk=t // KEY_CHUNK - 1)
    rows = NSA_REP * Q_TILE
    slab = lambda j: pl.BlockSpec((1, t, LANES), lambda bi, i, j=j: (bi, 0, j))
    return pl.pallas_call(
        kern,
        grid=(b, t // Q_TILE),
        in_specs=[pl.BlockSpec((1, Q_TILE, NSA_HEADS * LANES), lambda bi, i: (bi, i, 0)),
                  pl.BlockSpec((1, Q_TILE, LANES), lambda bi, i: (bi, i, 0)),
                  pl.BlockSpec((1, 1, ncp, LANES), lambda bi, i: (0, bi, 0, 0)),
                  pl.BlockSpec((1, 1, ncp, LANES), lambda bi, i: (1, bi, 0, 0)),
                  slab(0), slab(1), slab(2), slab(3),
                  _const_spec(onehot.shape), _const_spec(ovt.shape)],
        out_specs=pl.BlockSpec((1, Q_TILE, NSA_HEADS * HEAD_DIM), lambda bi, i: (bi, i, 0)),
        out_shape=jax.ShapeDtypeStruct((b, t, NSA_HEADS * HEAD_DIM), F32),
        scratch_shapes=[pltpu.VMEM((KV_GROUPS, rows, KEY_CHUNK), F32), pltpu.VMEM((KV_GROUPS, rows, KEY_CHUNK), F32),
                        pltpu.VMEM((KV_GROUPS, rows, 1), F32), pltpu.VMEM((KV_GROUPS, rows, LANES), F32)],
        compiler_params=_cparams("parallel", "arbitrary"),
    )(q, gd, cmp, cmp, kvb, kvb, kvb, kvb, onehot, ovt)


def _ssd_kernel(xbc_ref, z_ref, gd_ref, cw_ref, cb_ref, a_ref, dsk_ref, ng_ref,
                y_ref, hfin_ref, ext_ref, h_ref, y_scr):
    c = pl.program_id(1)
    L = SSM_CHUNK
    pd = SSM_HEADS * HEAD_DIM
    gw = SSM_STATE

    @pl.when(c == 0)
    def _():
        ext_ref[0:SUBLANES, :] = jnp.zeros((SUBLANES, ext_ref.shape[1]), F32)
        h_ref[...] = jnp.zeros(h_ref.shape, F32)

    ext_ref[SUBLANES:SUBLANES + L, :] = xbc_ref[0]
    conv = cb_ref[...]
    for k in range(SSM_CONV):
        off = SUBLANES - (SSM_CONV - 1) + k
        conv = conv + ext_ref[off:off + L, :] * cw_ref[k:k + 1, :]
    ext_ref[0:SUBLANES, :] = ext_ref[L:L + SUBLANES, :]
    act = _silu(conv)
    xs = act[:, :pd]
    gd = gd_ref[0]

    ii = lax.broadcasted_iota(jnp.int32, (L, L), 0)
    jj = lax.broadcasted_iota(jnp.int32, (L, L), 1)
    causal = ii >= jj
    tri = jnp.where(causal, 1.0, 0.0).astype(BF16)
    acum = None
    for part in _split3(gd * a_ref[...]):
        term = jnp.dot(tri, part, preferred_element_type=F32)
        acum = term if acum is None else acum + term
    acum_t = acum.T

    hpg = SSM_HEADS // SSM_GROUPS
    bm = [act[:, pd + g * gw:pd + (g + 1) * gw].astype(BF16) for g in range(SSM_GROUPS)]
    cm = [act[:, pd + (SSM_GROUPS + g) * gw:pd + (SSM_GROUPS + g + 1) * gw].astype(BF16) for g in range(SSM_GROUPS)]
    cb = [lax.dot_general(cm[g], bm[g], NT_DIMS, preferred_element_type=F32) for g in range(SSM_GROUPS)]
    h_prev = [h_ref[hh] for hh in range(SSM_HEADS)]
    c_h = [lax.dot_general(cm[hh // hpg], h_prev[hh].astype(BF16), NT_DIMS, preferred_element_type=F32)
           for hh in range(SSM_HEADS)]
    for g in range(SSM_GROUPS):
        for hl in range(hpg):
            hh = g * hpg + hl
            col = acum[:, DT_LANE0 + hh:DT_LANE0 + hh + 1]
            row = acum_t[DT_LANE0 + hh:DT_LANE0 + hh + 1, :]
            last = acum[L - 1:L, DT_LANE0 + hh:DT_LANE0 + hh + 1]
            lmat = jnp.exp(jnp.where(causal, col - row, NEG_INF))
            x_h = xs[:, hh * HEAD_DIM:(hh + 1) * HEAD_DIM]
            xdt = x_h * gd[:, DT_LANE0 + hh:DT_LANE0 + hh + 1]
            y_diag = jnp.dot((cb[g] * lmat).astype(BF16), xdt.astype(BF16), preferred_element_type=F32)
            y_off = c_h[hh] * jnp.exp(col)
            st = lax.dot_general((xdt * jnp.exp(last - col)).astype(BF16), bm[g], TN_DIMS,
                                 preferred_element_type=F32)
            h_ref[hh] = jnp.exp(last) * h_prev[hh] + st
            y_scr[:, hh * HEAD_DIM:(hh + 1) * HEAD_DIM] = (
                y_diag + y_off + dsk_ref[:, hh * HEAD_DIM:(hh + 1) * HEAD_DIM] * x_h)

    y = y_scr[...] * _silu(z_ref[0])
    for g in range(SSM_GROUPS):
        cols = slice(g * gw, (g + 1) * gw)
        yg = y[:, cols]
        ms = jnp.mean(yg * yg, axis=-1, keepdims=True)
        y_ref[0, :, cols] = yg * lax.rsqrt(ms + EPS) * ng_ref[:, cols]

    @pl.when(c == pl.num_programs(1) - 1)
    def _():
        hfin_ref[0] = h_ref[...]


def ssd_prompt(xbc, z, gd, lw):
    b, t, cd = xbc.shape
    L = SSM_CHUNK
    pd = SSM_HEADS * HEAD_DIM
    tile = lambda n: pl.BlockSpec((1, L, n), lambda i, j: (i, j, 0))
    return pl.pallas_call(
        _ssd_kernel,
        grid=(b, t // L),
        in_specs=[tile(cd), tile(pd), tile(LANES),
                  _const_spec((SSM_CONV, cd)), _const_spec((1, cd)), _const_spec((1, LANES)),
                  _const_spec((1, pd)), _const_spec((1, pd))],
        out_specs=[tile(pd), pl.BlockSpec((1, SSM_HEADS, HEAD_DIM, SSM_STATE), lambda i, j: (i, 0, 0, 0))],
        out_shape=[jax.ShapeDtypeStruct((b, t, pd), F32),
                   jax.ShapeDtypeStruct((b, SSM_HEADS, HEAD_DIM, SSM_STATE), F32)],
        scratch_shapes=[pltpu.VMEM((SUBLANES + L, cd), F32),
                        pltpu.VMEM((SSM_HEADS, HEAD_DIM, SSM_STATE), F32),
                        pltpu.VMEM((L, pd), F32)],
        compiler_params=_cparams("parallel", "arbitrary"),
    )(xbc, z, gd, lw["conv_w"], lw["conv_b"], lw["a_row"], lw["d_skip"], lw["ssm_norm"])


SEG_PER_PAGE = PAGE_SIZE // CMP_STRIDE
BLOCKS_PER_PAGE = PAGE_SIZE // SEL_BLOCK
N_PICK = SEL_TOPN - 1
SEQ_PER_STEP = SUBLANES // KV_GROUPS


def _past_compress_kernel(pt_ref, *refs):
    del pt_ref
    n_pages = len(refs) - 4
    pages = refs[:n_pages]
    w_ref, pe_ref, o_ref, x_scr = refs[n_pages:]
    nrow = n_pages * SEG_PER_PAGE
    row = lax.broadcasted_iota(jnp.int32, (nrow, LANES), 0)
    for kv in range(2):
        for k, pg in enumerate(pages):
            x_scr[kv, k * PAGE_SIZE:(k + 1) * PAGE_SIZE, :] = pg[kv].T
    for kv in range(2):
        seg = jnp.concatenate([x_scr[kv, pl.ds(jj, nrow, stride=CMP_STRIDE), :].astype(BF16)
                               for jj in range(CMP_STRIDE)], axis=1)
        y = jnp.dot(jnp.concatenate([seg, pe_ref[kv]], axis=0), w_ref[kv], preferred_element_type=F32)
        second = pltpu.roll(y[:nrow, LANES:], nrow - 1, 0)
        second = jnp.where(row < nrow - 1, second, 0.0)
        out = y[:nrow, :LANES] + second + y[nrow:nrow + 1, :LANES] + y[nrow + 1:nrow + 2, LANES:]
        o_ref[kv] = out.astype(o_ref.dtype)


def past_compress(cache_t, page_table, w_all, pe_all):
    depth = cache_t.shape[1]
    db, n_pages = page_table.shape
    nrow = n_pages * SEG_PER_PAGE
    page_spec = lambda k: pl.BlockSpec(
        (None, None, 2, LANES, PAGE_SIZE), lambda l, b, pt, k=k: (pt[b, k], l, 0, 0, 0))
    grid_spec = pltpu.PrefetchScalarGridSpec(
        num_scalar_prefetch=1,
        grid=(depth, db),
        in_specs=[page_spec(k) for k in range(n_pages)]
                 + [pl.BlockSpec((None,) + w_all.shape[1:], lambda l, b, pt: (l, 0, 0, 0)),
                    pl.BlockSpec((None,) + pe_all.shape[1:], lambda l, b, pt: (l, 0, 0, 0))],
        out_specs=pl.BlockSpec((None, 2, None, nrow, LANES), lambda l, b, pt: (l, 0, b, 0, 0)),
        scratch_shapes=[pltpu.VMEM((2, n_pages * PAGE_SIZE, LANES), F32)],
    )
    return pl.pallas_call(
        _past_compress_kernel,
        grid_spec=grid_spec,
        out_shape=jax.ShapeDtypeStruct((depth, 2, db, nrow, LANES), BF16),
        compiler_params=_cparams("parallel", "parallel"),
    )(page_table, *([cache_t] * n_pages), w_all, pe_all)


def _sample_cmp_kernel(q_ref, kc_ref, vc_ref, ov_ref, ocmp_ref, idx_ref, *, past_len):
    ncp = kc_ref.shape[3]
    cend = lax.broadcasted_iota(jnp.int32, (1, ncp), 1) * CMP_STRIDE + (CMP_BLOCK - 1)
    cmask = cend <= past_len
    hrow = lax.broadcasted_iota(jnp.int32, (NSA_HEADS, 1), 0)
    sums = []
    for sq in range(SEQ_PER_STEP):
        q = q_ref[sq]
        s = lax.dot_general(q, kc_ref[0, 0, sq], NT_DIMS, preferred_element_type=F32)
        p = jnp.where(cmask, _softmax_rows(s, cmask), 0.0)
        ocmp_ref[sq] = jnp.dot(p.astype(BF16), vc_ref[0, 0, sq], preferred_element_type=F32)
        sums += [jnp.sum(jnp.where((hrow // NSA_REP) == g, p, 0.0), axis=0, keepdims=True)
                 for g in range(KV_GROUPS)]
    psum = jnp.concatenate(sums, axis=0)
    imp = None
    for part in _split3(psum):
        term = jnp.dot(part, ov_ref[...], preferred_element_type=F32)
        imp = term if imp is None else imp + term
    n_past = past_len // SEL_BLOCK
    blk = lax.broadcasted_iota(jnp.int32, (SUBLANES, SEL_PAD), 1)
    forced = jnp.where(blk == 0, 1.0, jnp.where(blk == n_past - 1, 1.0, 0.0))
    score = jnp.where(blk < n_past, imp + SEL_BONUS * forced, -jnp.inf)
    col = lax.broadcasted_iota(jnp.int32, (SUBLANES, LANES), 1)
    picked = jnp.zeros((SUBLANES, LANES), jnp.int32)
    for k in range(N_PICK):
        m = jnp.max(score, axis=-1, keepdims=True)
        idx = jnp.min(jnp.where(score == m, blk, SEL_PAD), axis=-1, keepdims=True)
        score = jnp.where(blk == idx, -jnp.inf, score)
        picked = jnp.where(col == k, idx, picked)
    idx_ref[0] = picked


def sample_cmp_select(q8, cmp_past, layer, ov, past_len):
    db = q8.shape[0]
    ncp = cmp_past.shape[3]
    n_past = past_len // SEL_BLOCK
    assert n_past <= SEL_PAD and n_past + 1 > SEL_TOPN and past_len % SEL_BLOCK == 0
    assert db % SEQ_PER_STEP == 0
    o_cmp, picked = pl.pallas_call(
        functools.partial(_sample_cmp_kernel, past_len=past_len),
        grid=(db // SEQ_PER_STEP,),
        in_specs=[pl.BlockSpec((SEQ_PER_STEP, NSA_HEADS, LANES), lambda b: (b, 0, 0)),
                  pl.BlockSpec((1, 1, SEQ_PER_STEP, ncp, LANES), lambda b: (layer, 0, b, 0, 0)),
                  pl.BlockSpec((1, 1, SEQ_PER_STEP, ncp, LANES), lambda b: (layer, 1, b, 0, 0)),
                  _const_spec(ov.shape)],
        out_specs=[pl.BlockSpec((SEQ_PER_STEP, NSA_HEADS, LANES), lambda b: (b, 0, 0)),
                   pl.BlockSpec((1, SUBLANES, LANES), lambda b: (b, 0, 0))],
        out_shape=[jax.ShapeDtypeStruct((db, NSA_HEADS, LANES), F32),
                   jax.ShapeDtypeStruct((db // SEQ_PER_STEP, SUBLANES, LANES), jnp.int32)],
        compiler_params=_cparams("parallel"),
    )(q8, cmp_past, cmp_past, ov)
    return o_cmp, picked.reshape(db, KV_GROUPS, LANES)[:, :, :SEL_TOPN]


def _sample_attn_kernel(idx_ref, pt_ref, *refs):
    del pt_ref
    kpage = refs[:N_PICK]
    vpage = refs[N_PICK:2 * N_PICK]
    q_ref, kv_ref, gd_ref, ocmp_ref, win_ref, o_ref = refs[2 * N_PICK:]
    b = pl.program_id(0)
    g = pl.program_id(1)
    q = q_ref[0]
    qf = q.astype(F32)
    kvn = kv_ref[0]
    lane = lax.broadcasted_iota(jnp.int32, (1, LANES), 1)
    chan = lax.broadcasted_iota(jnp.int32, (LANES, 1), 0)
    tok_half = lax.broadcasted_iota(jnp.int32, (1, PAGE_SIZE), 1) // SEL_BLOCK

    def branch(gg, keys_t, vals_t, tok_masks, j_new):
        own = (lane >= gg * HEAD_DIM) & (lane < (gg + 1) * HEAD_DIM)
        own_t = (chan >= gg * HEAD_DIM) & (chan < (gg + 1) * HEAD_DIM)
        ss = []
        for kt, tm in zip(keys_t, tok_masks):
            s = jnp.dot(q, kt.astype(BF16), preferred_element_type=F32)
            ss.append(s if tm is None else jnp.where(tm, s, NEG_INF))
        k_new = kvn[:, j_new * LANES:(j_new + 1) * LANES].astype(BF16).astype(F32)
        s_new = jnp.sum(qf * k_new, axis=-1, keepdims=True)
        m = s_new
        for s in ss:
            m = jnp.maximum(m, jnp.max(s, axis=-1, keepdims=True))
        v_new = jnp.where(own, kvn[:, (j_new + 1) * LANES:(j_new + 2) * LANES], 1.0).astype(BF16).astype(F32)
        acc = jnp.exp2(s_new - m).astype(BF16).astype(F32) * v_new
        for s, vt in zip(ss, vals_t):
            v_one_t = jnp.where(own_t, vt, 1.0).astype(BF16)
            acc = acc + lax.dot_general(jnp.exp2(s - m).astype(BF16), v_one_t, NT_DIMS, preferred_element_type=F32)
        return acc

    gd = gd_ref[0]
    ocmp = ocmp_ref[0]
    for gg in range(KV_GROUPS):
        @pl.when(g == gg)
        def _(gg=gg):
            den = (1 - gg) * HEAD_DIM
            masks = [tok_half == idx_ref[b, gg, j] % BLOCKS_PER_PAGE for j in range(N_PICK)]
            acc = branch(gg, [r[...] for r in kpage], [r[...] for r in vpage], masks, 2)
            o_slc = acc / acc[:, den:den + 1]
            acc = branch(gg, [win_ref[0]], [win_ref[1]], [None], 4)
            o_win = acc / acc[:, den:den + 1]
            for r in range(NSA_REP):
                hh = NSA_REP * gg + r
                mix = (gd[:, 3 * hh:3 * hh + 1] * ocmp[hh:hh + 1] + gd[:, 3 * hh + 1:3 * hh + 2] * o_slc[hh:hh + 1]
                       + gd[:, 3 * hh + 2:3 * hh + 3] * o_win[hh:hh + 1])
                o_ref[0, :, hh * HEAD_DIM:(hh + 1) * HEAD_DIM] = mix[:, gg * HEAD_DIM:(gg + 1) * HEAD_DIM]


def sample_attend(q8, kv, gd, ocmp, idx, page_table, cache_t, win_t, layer):
    db = q8.shape[0]
    page_spec = lambda which, j: pl.BlockSpec(
        (None, None, None, LANES, PAGE_SIZE),
        lambda b, g, ix, pt, j=j: (pt[b, ix[b, g, j] // BLOCKS_PER_PAGE], layer, which, 0, 0))
    per_b = lambda shape: pl.BlockSpec((1,) + shape, lambda b, g, ix, pt: (b,) + (0,) * len(shape))
    wlen = win_t.shape[4]
    grid_spec = pltpu.PrefetchScalarGridSpec(
        num_scalar_prefetch=2,
        grid=(db, KV_GROUPS),
        in_specs=[page_spec(2, j) for j in range(N_PICK)] + [page_spec(3, j) for j in range(N_PICK)]
                 + [per_b((NSA_HEADS, LANES)), per_b((1, 6 * LANES)), per_b((1, LANES)), per_b((NSA_HEADS, LANES)),
                    pl.BlockSpec((None, None, 2, LANES, wlen), lambda b, g, ix, pt: (b, layer, 0, 0, 0))],
        out_specs=per_b((1, NSA_HEADS * HEAD_DIM)),
    )
    return pl.pallas_call(
        _sample_attn_kernel,
        grid_spec=grid_spec,
        out_shape=jax.ShapeDtypeStruct((db, 1, NSA_HEADS * HEAD_DIM), F32),
        compiler_params=_cparams("parallel", "arbitrary"),
    )(idx, page_table, *([cache_t] * (2 * N_PICK)), q8, kv, gd, ocmp, win_t)


def _sample_state_kernel(u_ref, xbc_ref, z_ref, gd_ref, pool_ref, conv_ref, h_ref,
                         pw_ref, ps_ref, cw_ref, cb_ref, a_ref, dsk_ref, ng_ref,
                         ypool_ref, yssm_ref, npool_ref, nconv_ref, nh_ref):
    pd = SSM_HEADS * HEAD_DIM
    u = u_ref[0]
    ext = jnp.concatenate([pool_ref[...], u], axis=0)
    row = lax.broadcasted_iota(jnp.int32, ext.shape, 0)
    lane = lax.broadcasted_iota(jnp.int32, (1, ext.shape[1]), 1)
    d = jnp.zeros_like(u)
    for gi, win in enumerate(POOL_WINDOWS):
        s = jnp.sum(jnp.where(row >= POOL_BUF + 1 - win, ext, 0.0), axis=0, keepdims=True)
        d = jnp.where((lane >= gi * HEAD_DIM) & (lane < (gi + 1) * HEAD_DIM), s / float(win) - u, d)
    d8 = jnp.concatenate([d, jnp.zeros((SUBLANES - 1, d.shape[1]), F32)], axis=0).astype(BF16)
    ypool_ref[0] = jnp.dot(d8, pw_ref[...], preferred_element_type=F32)[0:1] * ps_ref[...]
    npool_ref[...] = ext[1:]

    xbc = xbc_ref[0]
    extc = jnp.concatenate([conv_ref[...], xbc], axis=0)
    act = _silu(jnp.sum(extc * cw_ref[...], axis=0, keepdims=True) + cb_ref[...])
    nconv_ref[...] = extc[1:]
    gd = gd_ref[0]
    dec_row = jnp.exp(gd * a_ref[...])
    eye = (lax.broadcasted_iota(jnp.int32, (HEAD_DIM, HEAD_DIM), 0)
           == lax.broadcasted_iota(jnp.int32, (HEAD_DIM, HEAD_DIM), 1))
    ys = []
    for hh in range(SSM_HEADS):
        g = hh // (SSM_HEADS // SSM_GROUPS)
        x_row = act[:, hh * HEAD_DIM:(hh + 1) * HEAD_DIM]
        dt = gd[:, DT_LANE0 + hh:DT_LANE0 + hh + 1]
        xdt_col = jnp.sum(jnp.where(eye, x_row * dt, 0.0), axis=1, keepdims=True)
        b_row = act[:, pd + g * SSM_STATE:pd + (g + 1) * SSM_STATE]
        c_row = act[:, pd + (SSM_GROUPS + g) * SSM_STATE:pd + (SSM_GROUPS + g + 1) * SSM_STATE]
        h_new = dec_row[:, DT_LANE0 + hh:DT_LANE0 + hh + 1] * h_ref[hh] + xdt_col * b_row
        nh_ref[hh] = h_new
        y_col = jnp.sum(h_new * c_row, axis=1, keepdims=True)
        y_row = jnp.sum(jnp.where(eye, y_col, 0.0), axis=0, keepdims=True)
        ys.append(y_row + dsk_ref[:, hh * HEAD_DIM:(hh + 1) * HEAD_DIM] * x_row)
    y = jnp.concatenate(ys, axis=1) * _silu(z_ref[0])
    outs = []
    for g in range(SSM_GROUPS):
        yg = y[:, g * SSM_STATE:(g + 1) * SSM_STATE]
        ms = jnp.mean(yg * yg, axis=-1, keepdims=True)
        outs.append(yg * lax.rsqrt(ms + EPS) * ng_ref[:, g * SSM_STATE:(g + 1) * SSM_STATE])
    yssm_ref[0] = jnp.concatenate(outs, axis=1)


def sample_state_mixers(u, xbc, z, gd, state_pool, state_conv, state_ssm, layer, lw):
    db = u.shape[0]
    per_b = lambda shape: pl.BlockSpec((1,) + shape, lambda b: (b,) + (0,) * len(shape))
    st = lambda shape: pl.BlockSpec((None, None) + shape, lambda b: (b, layer) + (0,) * len(shape))
    new = lambda shape: pl.BlockSpec((None,) + shape, lambda b: (b,) + (0,) * len(shape))
    ps, cs, hs = state_pool.shape[2:], state_conv.shape[2:], state_ssm.shape[2:]
    params = [lw['pool_w'], lw['pool_scale'], lw['conv_w'], lw['conv_b'], lw['a_row'], lw['d_skip'], lw['ssm_norm']]
    return pl.pallas_call(
        _sample_state_kernel,
        grid=(db,),
        in_specs=[per_b(u.shape[1:]), per_b(xbc.shape[1:]), per_b(z.shape[1:]), per_b(gd.shape[1:]),
                  st(ps), st(cs), st(hs)] + [_const_spec(p.shape) for p in params],
        out_specs=[per_b((1, ps[1])), per_b((1, z.shape[2])), new(ps), new(cs), new(hs)],
        out_shape=[jax.ShapeDtypeStruct((db, 1, ps[1]), F32), jax.ShapeDtypeStruct((db, 1, z.shape[2]), F32),
                   jax.ShapeDtypeStruct((db,) + ps, F32), jax.ShapeDtypeStruct((db,) + cs, F32),
                   jax.ShapeDtypeStruct((db,) + hs, F32)],
        compiler_params=_cparams("parallel"),
    )(u, xbc, z, gd, state_pool, state_conv, state_ssm, *params)


def _prep_layer(p, i):
    d = p['w_in'].shape[1]
    w_in = p['w_in'][i]
    o = np.cumsum([0, POOL_DIM, NSA_DIM, 6 * KV_GROUPS * HEAD_DIM, GATE_LANES, SSM_DIM, SSM_CONV_DIM, SSM_HEADS])
    w_q = w_in[:, o[1]:o[2]].reshape(d, NSA_HEADS, HEAD_DIM)
    slot = jnp.zeros((d, NSA_HEADS, KV_GROUPS, HEAD_DIM), F32)
    for hh in range(NSA_HEADS):
        slot = slot.at[:, hh, hh // NSA_REP].set(w_q[:, hh])
    w_gd = jnp.zeros((d, LANES), F32)
    w_gd = w_gd.at[:, :GATE_LANES].set(w_in[:, o[3]:o[4]])
    w_gd = w_gd.at[:, DT_LANE0:DT_LANE0 + SSM_HEADS].set(w_in[:, o[6]:o[7]])
    lane_pad = lambda v: jnp.zeros((1, LANES), F32).at[0, DT_LANE0:DT_LANE0 + SSM_HEADS].set(v)
    eye_g = jnp.eye(KV_GROUPS, dtype=F32)
    ratio = CMP_BLOCK // CMP_STRIDE
    cmp_w, cmp_pe = [], []
    for j in range(2):
        w4 = p['nsa_cmp_w'][i, j].reshape(ratio, CMP_STRIDE, HEAD_DIM, HEAD_DIM)
        cmp_w.append(jnp.einsum('rjde,gh->jgdrhe', w4, eye_g).reshape(CMP_STRIDE * 2 * HEAD_DIM, ratio * LANES))
        pe4 = p['nsa_cmp_pe'][i, j].reshape(ratio, CMP_STRIDE, 1, HEAD_DIM)
        pe_rows = jnp.broadcast_to(pe4, (ratio, CMP_STRIDE, KV_GROUPS, HEAD_DIM)).reshape(ratio, -1)
        cmp_pe.append(jnp.zeros((SUBLANES, pe_rows.shape[1]), F32).at[:ratio].set(pe_rows))
    pool_w = jnp.zeros((POOL_DIM, POOL_DIM), F32)
    for gi in range(len(POOL_WINDOWS)):
        sl = slice(gi * HEAD_DIM, (gi + 1) * HEAD_DIM)
        pool_w = pool_w.at[sl, sl].set(p['pool_w'][i, gi])
    w_out = p['w_out'][i]
    bf = lambda x: x.astype(BF16)
    row = lambda x: x.reshape(1, -1)
    return {
        'ffn1_norm': row(p['ffn1_norm'][i]), 'ffn1_w_gate': bf(p['ffn1_w_gate'][i]),
        'ffn1_w_up': bf(p['ffn1_w_up'][i]), 'ffn1_w_down': bf(p['ffn1_w_down'][i]),
        'ffn2_norm': row(p['ffn2_norm'][i]), 'ffn2_w_gate': bf(p['ffn2_w_gate'][i]),
        'ffn2_w_up': bf(p['ffn2_w_up'][i]), 'ffn2_w_down': bf(p['ffn2_w_down'][i]),
        'mix_norm': row(p['mix_norm'][i]),
        'w_u': bf(w_in[:, o[0]:o[1]]), 'w_q': bf(slot.reshape(d, NSA_HEADS * LANES)),
        'w_kv': bf(w_in[:, o[2]:o[3]]), 'w_gd': bf(w_gd), 'w_z': bf(w_in[:, o[4]:o[5]]),
        'w_xbc': bf(w_in[:, o[5]:o[6]]),
        'q_norm': row(jnp.tile(p['nsa_q_norm'][i], 2)), 'k_norm': jnp.tile(p['nsa_k_norm'][i], (1, 2)),
        'dt_bias': lane_pad(p['ssm_dt_bias'][i]),
        'w_out_pool': bf(w_out[:POOL_DIM]), 'w_out_nsa': bf(w_out[POOL_DIM:POOL_DIM + NSA_DIM]),
        'w_out_ssm': bf(w_out[POOL_DIM + NSA_DIM:]),
        'pool_w': bf(pool_w), 'pool_scale': row(p['pool_scale'][i]),
        'cmp_w': bf(jnp.stack(cmp_w)), 'cmp_pe': bf(jnp.stack(cmp_pe)),
        'conv_w': p['ssm_conv_w'][i], 'conv_b': row(p['ssm_conv_b'][i]),
        'a_row': lane_pad(-jnp.exp(p['ssm_a_log'][i])),
        'd_skip': row(jnp.repeat(p['ssm_d'][i], HEAD_DIM)), 'ssm_norm': row(p['ssm_norm'][i]),
        'ple_norm': row(p['ple_norm'][i]), 'ple_w_gate': bf(p['ple_w_gate'][i]),
        'ple_w_proj': bf(p['ple_w_proj'][i]),
    }


def _rope_tables(pos):
    half = HEAD_DIM // 2
    inv = ROPE_THETA ** (-jnp.arange(half, dtype=F32) / half)
    ang = pos.astype(F32)[:, None] * inv[None, :]
    cos = jnp.cos(ang)
    sin = jnp.sin(ang)
    return jnp.tile(cos, (1, 4)), jnp.tile(jnp.concatenate([-sin, sin], axis=1), (1, 2))


def _selection_constants(t):
    nseg = t // CMP_STRIDE
    nc = nseg - CMP_BLOCK // CMP_STRIDE + 1
    ns = t // SEL_BLOCK
    c_start = np.arange(nseg) * CMP_STRIDE
    s_start = np.arange(SEL_PAD) * SEL_BLOCK
    ovt = ((c_start[None, :] < s_start[:, None] + SEL_BLOCK) & (c_start[None, :] + CMP_BLOCK > s_start[:, None])
           & (np.arange(nseg)[None, :] < nc) & (np.arange(SEL_PAD)[:, None] < ns))
    onehot = (np.arange(t)[:, None] // SEL_BLOCK) == np.arange(SEL_PAD)[None, :]
    return jnp.asarray(onehot, BF16), jnp.asarray(ovt, BF16)


def _channel_major(cache):
    nd = cache.ndim
    t = jnp.transpose(cache, tuple(range(nd - 3)) + (nd - 2, nd - 1, nd - 3))
    return t.reshape(cache.shape[:nd - 3] + (cache.shape[-2] * cache.shape[-1], cache.shape[-3]))


def _token_layer_front(h, lw, cos, sin, seq_len=None):
    h = ffn_halfstep(h, lw['ffn1_norm'], lw['ffn1_w_gate'], lw['ffn1_w_up'], lw['ffn1_w_down'])
    return (h,) + tuple(in_projection(h, lw, cos, sin, seq_len))


def _token_layer_back(h, y_pool, y_nsa, y_ssm, pe, lw):
    h = out_projection(h, y_pool, y_nsa, y_ssm, lw)
    h = ffn_halfstep(h, lw['ffn2_norm'], lw['ffn2_w_gate'], lw['ffn2_w_up'], lw['ffn2_w_down'])
    return ple_step(h, pe, lw)


def kernel(x_prompt, x_sample, cache_nsa_kv, cache_win_kv, state_pool, state_conv, state_ssm, page_table,
           p_prompt, p_sample, ffn1_norm, ffn1_w_gate, ffn1_w_up, ffn1_w_down, mix_norm, w_in, w_out,
           pool_w, pool_scale, nsa_q_norm, nsa_k_norm, nsa_cmp_pe, nsa_cmp_w, ssm_conv_w, ssm_conv_b,
           ssm_dt_bias, ssm_a_log, ssm_d, ssm_norm, ffn2_norm, ffn2_w_gate, ffn2_w_up, ffn2_w_down,
           ple_norm, ple_w_gate, ple_w_proj):
    params = dict(ffn1_norm=ffn1_norm, ffn1_w_gate=ffn1_w_gate, ffn1_w_up=ffn1_w_up, ffn1_w_down=ffn1_w_down,
                  mix_norm=mix_norm, w_in=w_in, w_out=w_out, pool_w=pool_w, pool_scale=pool_scale,
                  nsa_q_norm=nsa_q_norm, nsa_k_norm=nsa_k_norm, nsa_cmp_pe=nsa_cmp_pe, nsa_cmp_w=nsa_cmp_w,
                  ssm_conv_w=ssm_conv_w, ssm_conv_b=ssm_conv_b, ssm_dt_bias=ssm_dt_bias, ssm_a_log=ssm_a_log,
                  ssm_d=ssm_d, ssm_norm=ssm_norm, ffn2_norm=ffn2_norm, ffn2_w_gate=ffn2_w_gate,
                  ffn2_w_up=ffn2_w_up, ffn2_w_down=ffn2_w_down, ple_norm=ple_norm, ple_w_gate=ple_w_gate,
                  ple_w_proj=ple_w_proj)
    depth = w_in.shape[0]
    b, t, d = x_prompt.shape
    db = x_sample.shape[0]
    past_len = page_table.shape[1] * PAGE_SIZE
    wkeep = min(WINDOW, t)

    cos_p, sin_p = _rope_tables(jnp.tile(jnp.arange(t, dtype=jnp.int32), b))
    cos_s, sin_s = _rope_tables(jnp.full((db,), past_len, jnp.int32))
    onehot, ovt = _selection_constants(t)

    lws = [_prep_layer(params, i) for i in range(depth)]
    cache_t, win_t = _channel_major(cache_nsa_kv), _channel_major(cache_win_kv)
    cmp_past = past_compress(cache_t, page_table, jnp.stack([lw['cmp_w'] for lw in lws]),
                             jnp.stack([lw['cmp_pe'] for lw in lws]))
    ov_past = _selection_constants(past_len)[1].T
    s3 = lambda x: x.reshape(db, 1, x.shape[-1])

    h_p = x_prompt.reshape(b * t, d)
    h_s = x_sample.reshape(db, d)
    st_p = [[] for _ in range(5)]
    st_s = [[] for _ in range(5)]
    for i in range(depth):
        lw = lws[i]
        h_p, u, qp, rows, wins, kvb, seg, gd, z, xbc = _token_layer_front(h_p, lw, cos_p, sin_p, t)
        r3 = lambda x: x.reshape(b, t, x.shape[-1])
        y_pool = pool_prompt(r3(u), lw)
        cmp = nsa_compress(seg.reshape(2, b, t // CMP_STRIDE, CMP_STRIDE * LANES), lw['cmp_w'], lw['cmp_pe'])
        y_nsa = nsa_prompt(r3(qp), r3(gd), cmp, r3(kvb), onehot, ovt)
        y_ssm, h_fin = ssd_prompt(r3(xbc), r3(z), r3(gd), lw)
        h_p = _token_layer_back(h_p, y_pool.reshape(b * t, -1), y_nsa.reshape(b * t, -1),
                                y_ssm.reshape(b * t, -1), p_prompt[i].reshape(b * t, -1), lw)
        st_p[0].append(rows.reshape(b, 4, t, KV_GROUPS, HEAD_DIM))
        st_p[1].append(wins[:, :, t - wkeep:].reshape(b, 2, wkeep, KV_GROUPS, HEAD_DIM))
        st_p[2].append(r3(u)[:, t - POOL_BUF:])
        st_p[3].append(r3(xbc)[:, t - (SSM_CONV - 1):])
        st_p[4].append(h_fin)
        h_s, u, qp, kv, gd, z, xbc = _token_layer_front(h_s, lw, cos_s, sin_s)
        q8 = qp.reshape(db, NSA_HEADS, LANES)
        o_cmp, picked = sample_cmp_select(q8, cmp_past, i, ov_past, past_len)
        y_nsa = sample_attend(q8, s3(kv), s3(gd), o_cmp, picked, page_table, cache_t, win_t, i)
        y_pool, y_ssm, new_pool, new_conv, new_h = sample_state_mixers(
            s3(u), s3(xbc), s3(z), s3(gd), state_pool, state_conv, state_ssm, i, lw)
        h_s = _token_layer_back(h_s, y_pool.reshape(db, -1), y_nsa.reshape(db, -1), y_ssm.reshape(db, -1),
                                p_sample[i].reshape(db, -1), lw)
        kv6 = kv.reshape(db, 6, 1, KV_GROUPS, HEAD_DIM)
        rows = kv6[:, 0:4]
        new_win = jnp.concatenate([cache_win_kv[:, i, :, 1:], kv6[:, 4:6]], axis=2)
        for j, v in enumerate((rows, new_win, new_pool, new_conv, new_h)):
            st_s[j].append(v)
    outs = [h_p.reshape(b, t, d), h_s.reshape(db, 1, d)]
    for j in range(5):
        outs.append(jnp.stack(st_p[j], axis=1))
        outs.append(jnp.stack(st_s[j], axis=1))
    return tuple(outs)
```

```python
import functools

import jax
import jax.numpy as jnp
import numpy as np
from jax import lax
from jax.experimental import pallas as pl
from jax.experimental.pallas import tpu as pltpu

F32 = jnp.float32
BF16 = jnp.bfloat16

POOL_WINDOWS = (2, 4, 8, 16)
POOL_BUF = 15
HEAD_DIM = 64
NSA_HEADS = 8
KV_GROUPS = 2
NSA_REP = NSA_HEADS // KV_GROUPS
CMP_BLOCK = 32
CMP_STRIDE = 16
SEL_BLOCK = 64
SEL_TOPN = 16
WINDOW = 512
SEL_BONUS = 1.0e4
NEG_INF = -1.0e30
SSM_HEADS = 4
SSM_GROUPS = 2
SSM_STATE = 128
SSM_CONV = 4
SSM_CHUNK = 128
ROPE_THETA = 10000.0
EPS = 1e-6
PAGE_SIZE = 128
POOL_DIM = len(POOL_WINDOWS) * HEAD_DIM
NSA_DIM = NSA_HEADS * HEAD_DIM
SSM_DIM = SSM_HEADS * HEAD_DIM
SSM_CONV_DIM = SSM_DIM + 2 * SSM_GROUPS * SSM_STATE

LANES = 128
SUBLANES = 8
VMEM_LIMIT_BYTES = 56 * 1024 * 1024

GATE_LANES = 3 * NSA_HEADS
DT_LANE0 = 32

Q_SCALE = HEAD_DIM ** -0.5 * float(np.log2(np.e))

NT_DIMS = (((1,), (1,)), ((), ()))
TN_DIMS = (((0,), (0,)), ((), ()))


def _cparams(*sem):
    return pltpu.CompilerParams(dimension_semantics=sem, vmem_limit_bytes=VMEM_LIMIT_BYTES)


def _const_spec(shape):
    nd = len(shape)
    return pl.BlockSpec(shape, lambda *_: (0,) * nd)


def _rmsnorm(x, g):
    ms = jnp.mean(x * x, axis=-1, keepdims=True)
    return x * lax.rsqrt(ms + EPS) * g


def _silu(x):
    return x * jax.nn.sigmoid(x)


def _split3(x):
    hi = x.astype(BF16)
    r = x - hi.astype(F32)
    mid = r.astype(BF16)
    lo = (r - mid.astype(F32)).astype(BF16)
    return hi, mid, lo


def _ffn_kernel(x_ref, g_ref, wg_ref, wu_ref, wd_ref, o_ref, *, n_chunks):
    x = x_ref[...]
    xn = _rmsnorm(x, g_ref[...]).astype(BF16)
    fc = wg_ref.shape[1] // n_chunks
    tot = None
    for c in range(n_chunks):
        sl = slice(c * fc, (c + 1) * fc)
        g = jnp.dot(xn, wg_ref[:, sl], preferred_element_type=F32)
        u = jnp.dot(xn, wu_ref[:, sl], preferred_element_type=F32)
        a = (_silu(g) * u).astype(BF16)
        d = jnp.dot(a, wd_ref[sl, :], preferred_element_type=F32)
        tot = d if tot is None else tot + d
    o_ref[...] = x + 0.5 * tot


def _row_tile(m, pref):
    return pref if m % pref == 0 else m


def _resident_spec(shape):
    nd = len(shape)
    return pl.BlockSpec(shape, lambda *_: (0,) * nd, pipeline_mode=pl.Buffered(1))


def ffn_halfstep(h, g, wg, wu, wd):
    m, d = h.shape
    f = wg.shape[1]
    tm = _row_tile(m, 1024)
    return pl.pallas_call(
        functools.partial(_ffn_kernel, n_chunks=2 if tm < 1024 else f // (2 * LANES)),
        grid=(m // tm,),
        in_specs=[pl.BlockSpec((tm, d), lambda i: (i, 0)),
                  _const_spec((1, d)), _resident_spec((d, f)), _resident_spec((d, f)), _resident_spec((f, d))],
        out_specs=pl.BlockSpec((tm, d), lambda i: (i, 0)),
        out_shape=jax.ShapeDtypeStruct((m, d), F32),
        compiler_params=_cparams("parallel"),
    )(h, g, wg, wu, wd)


def _rope128(x, cos, sin_signed, lane):
    rot = jnp.where((lane & 32) == 0, pltpu.roll(x, 96, 1), pltpu.roll(x, 32, 1))
    return x * cos + rot * sin_signed


def _inproj_kernel(x_ref, g_ref, wu_ref, wq_ref, wkv_ref, wgd_ref, wz_ref, wx_ref,
                   qn_ref, kn_ref, dtb_ref, cos_ref, sin_ref, *out_refs, prompt):
    if prompt:
        u_ref, q_ref, rows_ref, wins_ref, kvb_ref, seg_ref, gd_ref, z_ref, xbc_ref, seg_scr = out_refs
    else:
        u_ref, q_ref, kv_ref, gd_ref, z_ref, xbc_ref = out_refs
    x = x_ref[...]
    tm = x.shape[0]
    xn = _rmsnorm(x, g_ref[...]).astype(BF16)
    cos = cos_ref[...]
    sin = sin_ref[...]
    lane = lax.broadcasted_iota(jnp.int32, (tm, LANES), 1)
    inv_hd = 1.0 / HEAD_DIM

    q = jnp.dot(xn, wq_ref[...], preferred_element_type=F32)
    kv = jnp.dot(xn, wkv_ref[...], preferred_element_type=F32)

    y = jnp.dot(xn, wgd_ref[...], preferred_element_type=F32)
    yd = y + dtb_ref[...]
    softplus = jnp.maximum(yd, 0.0) + jnp.log1p(jnp.exp(-jnp.abs(yd)))
    gd_ref[...] = jnp.where(lane < DT_LANE0, jax.nn.sigmoid(y), softplus)

    qn = qn_ref[...]
    for hh in range(NSA_HEADS):
        s = q[:, hh * LANES:(hh + 1) * LANES]
        ms = jnp.sum(s * s, axis=-1, keepdims=True) * inv_hd
        s = s * lax.rsqrt(ms + EPS) * qn
        s = _rope128(s, cos, sin, lane) * Q_SCALE
        q_ref[:, hh * LANES:(hh + 1) * LANES] = s.astype(BF16)

    u_ref[...] = jnp.dot(xn, wu_ref[...], preferred_element_type=F32)
    z_ref[...] = jnp.dot(xn, wz_ref[...], preferred_element_type=F32)
    xbc_ref[...] = jnp.dot(xn, wx_ref[...], preferred_element_type=F32)

    low = lane < HEAD_DIM
    for j in range(6):
        s = kv[:, j * LANES:(j + 1) * LANES]
        if j % 2 == 0:
            sq = s * s
            s_all = jnp.sum(sq, axis=-1, keepdims=True)
            s_low = jnp.sum(jnp.where(low, sq, 0.0), axis=-1, keepdims=True)
            ms = jnp.where(low, s_low, s_all - s_low) * inv_hd
            s = s * lax.rsqrt(ms + EPS) * kn_ref[j // 2:j // 2 + 1, :]
            s = _rope128(s, cos, sin, lane)
        if not prompt:
            kv_ref[:, j * LANES:(j + 1) * LANES] = s
            continue
        if j < 4:
            rows_ref[0, j] = s
        else:
            wins_ref[0, j - 4] = s
        if j >= 2:
            kvb_ref[:, (j - 2) * LANES:(j - 1) * LANES] = s.astype(BF16)
        else:
            seg_scr[j] = s
            seg_ref[j] = jnp.concatenate(
                [seg_scr[j, pl.ds(jj, tm // CMP_STRIDE, stride=CMP_STRIDE), :].astype(BF16)
                 for jj in range(CMP_STRIDE)], axis=1)


def in_projection(h, lw, cos, sin, seq_len=None):
    m, d = h.shape
    tm = _row_tile(m, 512)
    row = lambda n: pl.BlockSpec((tm, n), lambda i: (i, 0))
    ws = [lw["w_u"], lw["w_q"], lw["w_kv"], lw["w_gd"], lw["w_z"], lw["w_xbc"]]
    tail = [(LANES, F32), (SSM_DIM, F32), (SSM_CONV_DIM, F32)]
    out_specs = [row(POOL_DIM), row(NSA_HEADS * LANES)]
    out_shape = [jax.ShapeDtypeStruct((m, POOL_DIM), F32), jax.ShapeDtypeStruct((m, NSA_HEADS * LANES), BF16)]
    scratch = []
    if seq_len is None:
        out_specs.append(row(6 * LANES))
        out_shape.append(jax.ShapeDtypeStruct((m, 6 * LANES), F32))
    else:
        assert seq_len % tm == 0 and tm % (2 * SUBLANES * CMP_STRIDE) == 0
        tps = seq_len // tm
        per_seq = lambda n: pl.BlockSpec((1, n, tm, LANES), lambda i: (i // tps, 0, i % tps, 0))
        out_specs += [per_seq(4), per_seq(2), row(4 * LANES),
                      pl.BlockSpec((2, tm // CMP_STRIDE, CMP_STRIDE * LANES), lambda i: (0, i, 0))]
        out_shape += [jax.ShapeDtypeStruct((m // seq_len, 4, seq_len, LANES), F32),
                      jax.ShapeDtypeStruct((m // seq_len, 2, seq_len, LANES), F32),
                      jax.ShapeDtypeStruct((m, 4 * LANES), BF16),
                      jax.ShapeDtypeStruct((2, m // CMP_STRIDE, CMP_STRIDE * LANES), BF16)]
        scratch = [pltpu.VMEM((2, tm, LANES), F32)]
    out_specs += [row(n) for n, _ in tail]
    out_shape += [jax.ShapeDtypeStruct((m, n), dt) for n, dt in tail]
    return pl.pallas_call(
        functools.partial(_inproj_kernel, prompt=seq_len is not None),
        grid=(m // tm,),
        in_specs=[row(d), _const_spec((1, d))] + [_resident_spec(w.shape) for w in ws]
                 + [_const_spec((1, LANES)), _const_spec((3, LANES)), _const_spec((1, LANES)),
                    row(LANES), row(LANES)],
        out_specs=out_specs,
        out_shape=out_shape,
        scratch_shapes=scratch,
        compiler_params=_cparams("parallel"),
    )(h, lw["mix_norm"], *ws, lw["q_norm"], lw["k_norm"], lw["dt_bias"], cos, sin)


def _outproj_kernel(h_ref, yp_ref, yn_ref, ys_ref, wp_ref, wn_ref, ws_ref, o_ref):
    acc = jnp.dot(yp_ref[...].astype(BF16), wp_ref[...], preferred_element_type=F32)
    acc = acc + jnp.dot(yn_ref[...].astype(BF16), wn_ref[...], preferred_element_type=F32)
    acc = acc + jnp.dot(ys_ref[...].astype(BF16), ws_ref[...], preferred_element_type=F32)
    o_ref[...] = h_ref[...] + acc


def out_projection(h, y_pool, y_nsa, y_ssm, lw):
    m, d = h.shape
    tm = _row_tile(m, 1024)
    row = lambda n: pl.BlockSpec((tm, n), lambda i: (i, 0))
    ws = [lw["w_out_pool"], lw["w_out_nsa"], lw["w_out_ssm"]]
    return pl.pallas_call(
        _outproj_kernel,
        grid=(m // tm,),
        in_specs=[row(d), row(y_pool.shape[1]), row(y_nsa.shape[1]), row(y_ssm.shape[1])]
                 + [_resident_spec(w.shape) for w in ws],
        out_specs=row(d),
        out_shape=jax.ShapeDtypeStruct((m, d), F32),
        compiler_params=_cparams("parallel"),
    )(h, y_pool, y_nsa, y_ssm, *ws)


def _ple_kernel(h_ref, pe_ref, g_ref, wg_ref, wp_ref, o_ref):
    proj = jnp.dot(pe_ref[...].astype(BF16), wp_ref[...], preferred_element_type=F32)
    h = h_ref[...]
    xn = _rmsnorm(h, g_ref[...]).astype(BF16)
    gate = jax.nn.sigmoid(jnp.dot(xn, wg_ref[...], preferred_element_type=F32))
    o_ref[...] = h + gate * proj


def ple_step(h, pe, lw):
    m, d = h.shape
    tm = _row_tile(m, 1024)
    row = lambda n: pl.BlockSpec((tm, n), lambda i: (i, 0))
    return pl.pallas_call(
        _ple_kernel,
        grid=(m // tm,),
        in_specs=[row(d), row(pe.shape[1]), _const_spec((1, d)),
                  _resident_spec(lw["ple_w_gate"].shape), _resident_spec(lw["ple_w_proj"].shape)],
        out_specs=row(d),
        out_shape=jax.ShapeDtypeStruct((m, d), F32),
        compiler_params=_cparams("parallel"),
    )(h, pe, lw["ple_norm"], lw["ple_w_gate"], lw["ple_w_proj"])


POOL_HALO = 2 * SUBLANES


def _pool_kernel(u_ref, w_ref, sc_ref, y_ref, ext_ref, *, tm):
    c = pl.program_id(1)

    @pl.when(c == 0)
    def _():
        ext_ref[0:POOL_HALO, :] = jnp.zeros((POOL_HALO, ext_ref.shape[1]), F32)

    ext_ref[POOL_HALO:POOL_HALO + tm, :] = u_ref[0]
    lane = lax.broadcasted_iota(jnp.int32, (tm, LANES), 1)
    low = lane < HEAD_DIM
    pos1 = c * tm + lax.broadcasted_iota(jnp.int32, (tm, LANES), 0) + 1
    ds = []
    for slab, (w_lo, w_hi) in enumerate(((POOL_WINDOWS[0], POOL_WINDOWS[1]), (POOL_WINDOWS[2], POOL_WINDOWS[3]))):
        cols = slice(slab * LANES, (slab + 1) * LANES)
        x = ext_ref[POOL_HALO:POOL_HALO + tm, cols]
        run = x
        s_lo = None
        for k in range(1, w_hi):
            run = run + ext_ref[POOL_HALO - k:POOL_HALO - k + tm, cols]
            if k == w_lo - 1:
                s_lo = run
        cnt = jnp.where(low, jnp.minimum(pos1, w_lo), jnp.minimum(pos1, w_hi)).astype(F32)
        ds.append(jnp.where(low, s_lo, run) / cnt - x)
    d = jnp.concatenate(ds, axis=1).astype(BF16)
    y_ref[0] = (jnp.dot(d, w_ref[...], preferred_element_type=F32) * sc_ref[...]).astype(y_ref.dtype)
    ext_ref[0:POOL_HALO, :] = ext_ref[tm:tm + POOL_HALO, :]


def pool_prompt(u, lw):
    b, t, ch = u.shape
    tm = _row_tile(t, 512)
    return pl.pallas_call(
        functools.partial(_pool_kernel, tm=tm),
        grid=(b, t // tm),
        in_specs=[pl.BlockSpec((1, tm, ch), lambda i, j: (i, j, 0)),
                  _const_spec((ch, ch)), _const_spec((1, ch))],
        out_specs=pl.BlockSpec((1, tm, ch), lambda i, j: (i, j, 0)),
        out_shape=jax.ShapeDtypeStruct((b, t, ch), BF16),
        scratch_shapes=[pltpu.VMEM((POOL_HALO + tm, ch), F32)],
        compiler_params=_cparams("parallel", "arbitrary"),
    )(u, lw["pool_w"], lw["pool_scale"])


def _compress_kernel(seg_ref, w_ref, pe_ref, o_ref):
    y = jnp.dot(seg_ref[0], w_ref[0], preferred_element_type=F32)
    pe = jnp.dot(pe_ref[0], w_ref[0], preferred_element_type=F32)
    nseg = y.shape[0]
    second = pltpu.roll(y[:, LANES:], nseg - 1, 0)
    row = lax.broadcasted_iota(jnp.int32, (nseg, LANES), 0)
    second = jnp.where(row < nseg - 1, second, 0.0)
    out = y[:, :LANES] + second + pe[0:1, :LANES] + pe[1:2, LANES:]
    o_ref[0, 0] = out.astype(o_ref.dtype)


def nsa_compress(seg, w, pe):
    _, b, nseg, width = seg.shape
    return pl.pallas_call(
        _compress_kernel,
        grid=(2, b),
        in_specs=[pl.BlockSpec((None, 1, nseg, width), lambda j, i: (j, i, 0, 0)),
                  pl.BlockSpec((1, width, 2 * LANES), lambda j, i: (j, 0, 0)),
                  pl.BlockSpec((1, SUBLANES, width), lambda j, i: (j, 0, 0))],
        out_specs=pl.BlockSpec((1, 1, nseg, LANES), lambda j, i: (j, i, 0, 0)),
        out_shape=jax.ShapeDtypeStruct((2, b, nseg, LANES), BF16),
        compiler_params=_cparams("parallel", "parallel"),
    )(seg, w, pe)


Q_TILE = 256
KEY_CHUNK = 512
SEL_PAD = 128


def _topk_knockout_t(score, n_sel):
    blk = lax.broadcasted_iota(jnp.int32, score.shape, 0)

    def body(_, sc):
        m = jnp.max(sc, axis=0, keepdims=True)
        idx = jnp.min(jnp.where(sc == m, blk, score.shape[0]), axis=0, keepdims=True)
        return jnp.where(blk == idx, -jnp.inf, sc)

    return lax.fori_loop(0, n_sel, body, score)


def _softmax_rows(s, mask):
    sm = jnp.where(mask, s, NEG_INF)
    m = jnp.max(sm, axis=-1, keepdims=True)
    e = jnp.exp2(sm - m)
    return e / jnp.sum(e, axis=-1, keepdims=True)


def _nsa_kernel(q_ref, gd_ref, kc_ref, vc_ref, ks_ref, vs_ref, kw_ref, vw_ref, e_ref, ovt_ref, o_ref,
                sa_scr, sb_scr, m_scr, acc_scr, *, n_blocks, n_sel, win_len, last_chunk):
    i = pl.program_id(1)
    t0 = i * Q_TILE
    ncp = kc_ref.shape[2]
    qpos = t0 + lax.broadcasted_iota(jnp.int32, (Q_TILE, 1), 0)
    gd = gd_ref[0]

    blk = lax.broadcasted_iota(jnp.int32, (SEL_PAD, Q_TILE), 0)
    qpos_t = t0 + lax.broadcasted_iota(jnp.int32, (SEL_PAD, Q_TILE), 1)
    cur = qpos_t // SEL_BLOCK
    forced = jnp.where(blk == 0, 1.0, jnp.where(blk == cur, 1.0, jnp.where(blk == cur - 1, 1.0, 0.0)))
    valid = blk * SEL_BLOCK <= qpos_t
    real = blk < n_blocks

    n_full = t0 // KEY_CHUNK
    rows = NSA_REP * Q_TILE
    groups = range(KV_GROUPS)
    lane_row = lax.broadcasted_iota(jnp.int32, (1, LANES), 1)
    cend = lax.broadcasted_iota(jnp.int32, (1, ncp), 1) * CMP_STRIDE + (CMP_BLOCK - 1)
    qpos4 = jnp.concatenate([qpos] * NSA_REP, axis=0)
    own = [(lane_row >= g * HEAD_DIM) & (lane_row < (g + 1) * HEAD_DIM) for g in groups]
    den_lane = [(1 - g) * HEAD_DIM for g in groups]
    q_all = [jnp.concatenate([q_ref[0, :, (NSA_REP * g + r) * LANES:(NSA_REP * g + r + 1) * LANES]
                              for r in range(NSA_REP)], axis=0) for g in groups]

    def split_heads(acc):
        return [acc[r * Q_TILE:(r + 1) * Q_TILE] for r in range(NSA_REP)]

    o_cmp, score = [], []
    for g in groups:
        s = lax.dot_general(q_all[g], kc_ref[0, 0], NT_DIMS, preferred_element_type=F32)
        s = jnp.where(cend <= qpos4, s, NEG_INF)
        e = jnp.exp2(s - jnp.max(s, axis=-1, keepdims=True))
        has_block = jnp.where(qpos4 >= CMP_BLOCK - 1, 1.0, 0.0)
        p = e * (has_block / jnp.sum(e, axis=-1, keepdims=True))
        o_cmp.append(split_heads(jnp.dot(p.astype(BF16), vc_ref[0, 0], preferred_element_type=F32)))
        psum = p[0:Q_TILE]
        for r in range(1, NSA_REP):
            psum = psum + p[r * Q_TILE:(r + 1) * Q_TILE]
        imp_t = None
        for part in _split3(psum):
            term = lax.dot_general(ovt_ref[...], part, NT_DIMS, preferred_element_type=F32)
            imp_t = term if imp_t is None else imp_t + term
        sc = jnp.where(valid, imp_t + SEL_BONUS * forced, -1.0)
        score.append(jnp.where(real, sc, -jnp.inf))

    knocked = _topk_knockout_t(jnp.concatenate(score, axis=1), n_sel)
    q_aug = []
    for g in groups:
        picked = (knocked[:, g * Q_TILE:(g + 1) * Q_TILE] == -jnp.inf) & real
        bias = jnp.where(picked, 0.0, NEG_INF).T.astype(BF16)
        q_aug.append(jnp.concatenate([jnp.concatenate([bias] * NSA_REP, axis=0), q_all[g]], axis=1))

    def scores(c, g, s_ref):
        k0 = pl.multiple_of(jnp.minimum(c, last_chunk) * KEY_CHUNK, KEY_CHUNK)
        k_aug = jnp.concatenate([e_ref[pl.ds(k0, KEY_CHUNK), :], ks_ref[0, pl.ds(k0, KEY_CHUNK), :]], axis=1)
        s_ref[g] = lax.dot_general(q_aug[g], k_aug, NT_DIMS, preferred_element_type=F32)

    def reduce_chunk(c, g, s_ref, causal):
        k0 = pl.multiple_of(jnp.minimum(c, last_chunk) * KEY_CHUNK, KEY_CHUNK)
        v_one = jnp.where(own[g], vs_ref[0, pl.ds(k0, KEY_CHUNK), :], 1.0).astype(BF16)
        s = s_ref[g]
        if causal:
            kpos = c * KEY_CHUNK + lax.broadcasted_iota(jnp.int32, (1, KEY_CHUNK), 1)
            s = jnp.where(kpos <= qpos4, s, NEG_INF)
        m = m_scr[g]
        m_new = jnp.maximum(m, jnp.max(s, axis=-1, keepdims=True))
        pc = jnp.exp2(s - m_new).astype(BF16)
        acc_scr[g] = jnp.exp2(m - m_new) * acc_scr[g] + jnp.dot(pc, v_one, preferred_element_type=F32)
        m_scr[g] = m_new

    def pair(p, carry):
        for g in groups:
            scores(2 * p + 1, g, sb_scr)
        for g in groups:
            reduce_chunk(2 * p, g, sa_scr, False)
        for g in groups:
            scores(2 * p + 2, g, sa_scr)
        for g in groups:
            reduce_chunk(2 * p + 1, g, sb_scr, False)
        return carry

    for g in groups:
        m_scr[g] = jnp.full((rows, 1), NEG_INF, F32)
        acc_scr[g] = jnp.zeros((rows, LANES), F32)
        scores(0, g, sa_scr)
    n_pairs = n_full // 2
    lax.fori_loop(0, n_pairs, pair, 0)
    one_more = n_full > 2 * n_pairs

    @pl.when(one_more)
    def _():
        for g in groups:
            scores(2 * n_pairs + 1, g, sb_scr)

    for g in groups:
        reduce_chunk(2 * n_pairs, g, sa_scr, True)

    @pl.when(one_more)
    def _():
        for g in groups:
            reduce_chunk(2 * n_pairs + 1, g, sb_scr, True)

    w0 = pl.multiple_of(jnp.maximum(t0 + Q_TILE - win_len, 0), Q_TILE)
    kw = kw_ref[0, pl.ds(w0, win_len), :]
    dpos = (w0 + lax.broadcasted_iota(jnp.int32, (1, win_len), 1)) - qpos4
    wmask = (dpos + WINDOW).astype(jnp.uint32) <= WINDOW
    o_slc, o_win = [], []
    for g in groups:
        acc = acc_scr[g]
        o_slc.append(split_heads(acc / acc[:, den_lane[g]:den_lane[g] + 1]))
        vw_one = jnp.where(own[g], vw_ref[0, pl.ds(w0, win_len), :], 1.0).astype(BF16)
        s = jnp.where(wmask, lax.dot_general(q_all[g], kw, NT_DIMS, preferred_element_type=F32), NEG_INF)
        pw = jnp.exp2(s - jnp.max(s, axis=-1, keepdims=True)).astype(BF16)
        acc = jnp.dot(pw, vw_one, preferred_element_type=F32)
        o_win.append(split_heads(acc / acc[:, den_lane[g]:den_lane[g] + 1]))

    for g in groups:
        for r in range(NSA_REP):
            hh = NSA_REP * g + r
            mix = (gd[:, 3 * hh:3 * hh + 1] * o_cmp[g][r] + gd[:, 3 * hh + 1:3 * hh + 2] * o_slc[g][r]
                   + gd[:, 3 * hh + 2:3 * hh + 3] * o_win[g][r])
            o_ref[0, :, hh * HEAD_DIM:(hh + 1) * HEAD_DIM] = mix[:, g * HEAD_DIM:(g + 1) * HEAD_DIM].astype(o_ref.dtype)


def nsa_prompt(q, gd, cmp, kvb, onehot, ovt):
    b, t, _ = q.shape
    ncp = cmp.shape[2]
    n_blocks = t // SEL_BLOCK
    assert n_blocks <= SEL_PAD and t % KEY_CHUNK == 0
    win_len = min(WINDOW + Q_TILE, t)
    kern = functools.partial(_nsa_kernel, n_blocks=n_blocks, n_sel=min(SEL_TOPN, n_blocks), win_len=win_len,
                             last_chunk=t // KEY_CHUNK - 1)
    rows = NSA_REP * Q_TILE
    slab = lambda j: pl.BlockSpec((1, t, LANES), lambda bi, i, j=j: (bi, 0, j))
    return pl.pallas_call(
        kern,
        grid=(b, t // Q_TILE),
        in_specs=[pl.BlockSpec((1, Q_TILE, NSA_HEADS * LANES), lambda bi, i: (bi, i, 0)),
                  pl.BlockSpec((1, Q_TILE, LANES), lambda bi, i: (bi, i, 0)),
                  pl.BlockSpec((1, 1, ncp, LANES), lambda bi, i: (0, bi, 0, 0)),
                  pl.BlockSpec((1, 1, ncp, LANES), lambda bi, i: (1, bi, 0, 0)),
                  slab(0), slab(1), slab(2), slab(3),
                  _const_spec(onehot.shape), _const_spec(ovt.shape)],
        out_specs=pl.BlockSpec((1, Q_TILE, NSA_HEADS * HEAD_DIM), lambda bi, i: (bi, i, 0)),
        out_shape=jax.ShapeDtypeStruct((b, t, NSA_HEADS * HEAD_DIM), BF16),
        scratch_shapes=[pltpu.VMEM((KV_GROUPS, rows, KEY_CHUNK), F32), pltpu.VMEM((KV_GROUPS, rows, KEY_CHUNK), F32),
                        pltpu.VMEM((KV_GROUPS, rows, 1), F32), pltpu.VMEM((KV_GROUPS, rows, LANES), F32)],
        compiler_params=_cparams("parallel", "arbitrary"),
    )(q, gd, cmp, cmp, kvb, kvb, kvb, kvb, onehot, ovt)


def _ssd_kernel(xbc_ref, z_ref, gd_ref, cw_ref, cb_ref, a_ref, dsk_ref, ng_ref,
                y_ref, hfin_ref, ext_ref, h_ref, y_scr):
    c = pl.program_id(1)
    L = SSM_CHUNK
    pd = SSM_HEADS * HEAD_DIM
    gw = SSM_STATE

    @pl.when(c == 0)
    def _():
        ext_ref[0:SUBLANES, :] = jnp.zeros((SUBLANES, ext_ref.shape[1]), F32)
        h_ref[...] = jnp.zeros(h_ref.shape, F32)

    ext_ref[SUBLANES:SUBLANES + L, :] = xbc_ref[0]
    conv = cb_ref[...]
    for k in range(SSM_CONV):
        off = SUBLANES - (SSM_CONV - 1) + k
        conv = conv + ext_ref[off:off + L, :] * cw_ref[k:k + 1, :]
    ext_ref[0:SUBLANES, :] = ext_ref[L:L + SUBLANES, :]
    act = _silu(conv)
    xs = act[:, :pd]
    gd = gd_ref[0]

    ii = lax.broadcasted_iota(jnp.int32, (L, L), 0)
    jj = lax.broadcasted_iota(jnp.int32, (L, L), 1)
    causal = ii >= jj
    tri = jnp.where(causal, 1.0, 0.0).astype(BF16)
    acum = None
    for part in _split3(gd * a_ref[...]):
        term = jnp.dot(tri, part, preferred_element_type=F32)
        acum = term if acum is None else acum + term
    acum_t = acum.T

    hpg = SSM_HEADS // SSM_GROUPS
    bm = [act[:, pd + g * gw:pd + (g + 1) * gw].astype(BF16) for g in range(SSM_GROUPS)]
    cm = [act[:, pd + (SSM_GROUPS + g) * gw:pd + (SSM_GROUPS + g + 1) * gw].astype(BF16) for g in range(SSM_GROUPS)]
    cb = [lax.dot_general(cm[g], bm[g], NT_DIMS, preferred_element_type=F32) for g in range(SSM_GROUPS)]
    h_prev = [h_ref[hh] for hh in range(SSM_HEADS)]
    c_h = [lax.dot_general(cm[hh // hpg], h_prev[hh].astype(BF16), NT_DIMS, preferred_element_type=F32)
           for hh in range(SSM_HEADS)]
    for g in range(SSM_GROUPS):
        for hl in range(hpg):
            hh = g * hpg + hl
            col = acum[:, DT_LANE0 + hh:DT_LANE0 + hh + 1]
            row = acum_t[DT_LANE0 + hh:DT_LANE0 + hh + 1, :]
            last = acum[L - 1:L, DT_LANE0 + hh:DT_LANE0 + hh + 1]
            lmat = jnp.exp(jnp.where(causal, col - row, NEG_INF))
            x_h = xs[:, hh * HEAD_DIM:(hh + 1) * HEAD_DIM]
            xdt = x_h * gd[:, DT_LANE0 + hh:DT_LANE0 + hh + 1]
            y_diag = jnp.dot((cb[g] * lmat).astype(BF16), xdt.astype(BF16), preferred_element_type=F32)
            y_off = c_h[hh] * jnp.exp(col)
            st = lax.dot_general((xdt * jnp.exp(last - col)).astype(BF16), bm[g], TN_DIMS,
                                 preferred_element_type=F32)
            h_ref[hh] = jnp.exp(last) * h_prev[hh] + st
            y_scr[:, hh * HEAD_DIM:(hh + 1) * HEAD_DIM] = (
                y_diag + y_off + dsk_ref[:, hh * HEAD_DIM:(hh + 1) * HEAD_DIM] * x_h)

    y = y_scr[...] * _silu(z_ref[0])
    for g in range(SSM_GROUPS):
        cols = slice(g * gw, (g + 1) * gw)
        yg = y[:, cols]
        ms = jnp.mean(yg * yg, axis=-1, keepdims=True)
        y_ref[0, :, cols] = (yg * lax.rsqrt(ms + EPS) * ng_ref[:, cols]).astype(y_ref.dtype)

    @pl.when(c == pl.num_programs(1) - 1)
    def _():
        hfin_ref[0] = h_ref[...]


def ssd_prompt(xbc, z, gd, lw):
    b, t, cd = xbc.shape
    L = SSM_CHUNK
    pd = SSM_HEADS * HEAD_DIM
    tile = lambda n: pl.BlockSpec((1, L, n), lambda i, j: (i, j, 0))
    return pl.pallas_call(
        _ssd_kernel,
        grid=(b, t // L),
        in_specs=[tile(cd), tile(pd), tile(LANES),
                  _const_spec((SSM_CONV, cd)), _const_spec((1, cd)), _const_spec((1, LANES)),
                  _const_spec((1, pd)), _const_spec((1, pd))],
        out_specs=[tile(pd), pl.BlockSpec((1, SSM_HEADS, HEAD_DIM, SSM_STATE), lambda i, j: (i, 0, 0, 0))],
        out_shape=[jax.ShapeDtypeStruct((b, t, pd), BF16),
                   jax.ShapeDtypeStruct((b, SSM_HEADS, HEAD_DIM, SSM_STATE), F32)],
        scratch_shapes=[pltpu.VMEM((SUBLANES + L, cd), F32),
                        pltpu.VMEM((SSM_HEADS, HEAD_DIM, SSM_STATE), F32),
                        pltpu.VMEM((L, pd), F32)],
        compiler_params=_cparams("parallel", "arbitrary"),
    )(xbc, z, gd, lw["conv_w"], lw["conv_b"], lw["a_row"], lw["d_skip"], lw["ssm_norm"])


SEG_PER_PAGE = PAGE_SIZE // CMP_STRIDE
BLOCKS_PER_PAGE = PAGE_SIZE // SEL_BLOCK
N_PICK = SEL_TOPN - 1
SEQ_PER_STEP = SUBLANES // KV_GROUPS


def _past_compress_kernel(pt_ref, *refs):
    del pt_ref
    n_pages = len(refs) - 4
    pages = refs[:n_pages]
    w_ref, pe_ref, o_ref, x_scr = refs[n_pages:]
    nrow = n_pages * SEG_PER_PAGE
    row = lax.broadcasted_iota(jnp.int32, (nrow, LANES), 0)
    for kv in range(2):
        for k, pg in enumerate(pages):
            x_scr[kv, k * PAGE_SIZE:(k + 1) * PAGE_SIZE, :] = pg[kv].T
    for kv in range(2):
        seg = jnp.concatenate([x_scr[kv, pl.ds(jj, nrow, stride=CMP_STRIDE), :].astype(BF16)
                               for jj in range(CMP_STRIDE)], axis=1)
        y = jnp.dot(jnp.concatenate([seg, pe_ref[kv]], axis=0), w_ref[kv], preferred_element_type=F32)
        second = pltpu.roll(y[:nrow, LANES:], nrow - 1, 0)
        second = jnp.where(row < nrow - 1, second, 0.0)
        out = y[:nrow, :LANES] + second + y[nrow:nrow + 1, :LANES] + y[nrow + 1:nrow + 2, LANES:]
        o_ref[kv] = out.astype(o_ref.dtype)


def past_compress(cache_t, page_table, w_all, pe_all):
    depth = cache_t.shape[1]
    db, n_pages = page_table.shape
    nrow = n_pages * SEG_PER_PAGE
    page_spec = lambda k: pl.BlockSpec(
        (None, None, 2, LANES, PAGE_SIZE), lambda l, b, pt, k=k: (pt[b, k], l, 0, 0, 0))
    grid_spec = pltpu.PrefetchScalarGridSpec(
        num_scalar_prefetch=1,
        grid=(depth, db),
        in_specs=[page_spec(k) for k in range(n_pages)]
                 + [pl.BlockSpec((None,) + w_all.shape[1:], lambda l, b, pt: (l, 0, 0, 0)),
                    pl.BlockSpec((None,) + pe_all.shape[1:], lambda l, b, pt: (l, 0, 0, 0))],
        out_specs=pl.BlockSpec((None, 2, None, nrow, LANES), lambda l, b, pt: (l, 0, b, 0, 0)),
        scratch_shapes=[pltpu.VMEM((2, n_pages * PAGE_SIZE, LANES), F32)],
    )
    return pl.pallas_call(
        _past_compress_kernel,
        grid_spec=grid_spec,
        out_shape=jax.ShapeDtypeStruct((depth, 2, db, nrow, LANES), BF16),
        compiler_params=_cparams("parallel", "parallel"),
    )(page_table, *([cache_t] * n_pages), w_all, pe_all)


def _sample_cmp_kernel(q_ref, kc_ref, vc_ref, ov_ref, ocmp_ref, idx_ref, *, past_len):
    ncp = kc_ref.shape[3]
    cend = lax.broadcasted_iota(jnp.int32, (1, ncp), 1) * CMP_STRIDE + (CMP_BLOCK - 1)
    cmask = cend <= past_len
    hrow = lax.broadcasted_iota(jnp.int32, (NSA_HEADS, 1), 0)
    sums = []
    for sq in range(SEQ_PER_STEP):
        q = q_ref[sq]
        s = lax.dot_general(q, kc_ref[0, 0, sq], NT_DIMS, preferred_element_type=F32)
        p = jnp.where(cmask, _softmax_rows(s, cmask), 0.0)
        ocmp_ref[sq] = jnp.dot(p.astype(BF16), vc_ref[0, 0, sq], preferred_element_type=F32)
        sums += [jnp.sum(jnp.where((hrow // NSA_REP) == g, p, 0.0), axis=0, keepdims=True)
                 for g in range(KV_GROUPS)]
    psum = jnp.concatenate(sums, axis=0)
    imp = None
    for part in _split3(psum):
        term = jnp.dot(part, ov_ref[...], preferred_element_type=F32)
        imp = term if imp is None else imp + term
    n_past = past_len // SEL_BLOCK
    blk = lax.broadcasted_iota(jnp.int32, (SUBLANES, SEL_PAD), 1)
    forced = jnp.where(blk == 0, 1.0, jnp.where(blk == n_past - 1, 1.0, 0.0))
    score = jnp.where(blk < n_past, imp + SEL_BONUS * forced, -jnp.inf)
    col = lax.broadcasted_iota(jnp.int32, (SUBLANES, LANES), 1)
    picked = jnp.zeros((SUBLANES, LANES), jnp.int32)
    for k in range(N_PICK):
        m = jnp.max(score, axis=-1, keepdims=True)
        idx = jnp.min(jnp.where(score == m, blk, SEL_PAD), axis=-1, keepdims=True)
        score = jnp.where(blk == idx, -jnp.inf, score)
        picked = jnp.where(col == k, idx, picked)
    idx_ref[0] = picked


def sample_cmp_select(q8, cmp_past, layer, ov, past_len):
    db = q8.shape[0]
    ncp = cmp_past.shape[3]
    n_past = past_len // SEL_BLOCK
    assert n_past <= SEL_PAD and n_past + 1 > SEL_TOPN and past_len % SEL_BLOCK == 0
    assert db % SEQ_PER_STEP == 0
    o_cmp, picked = pl.pallas_call(
        functools.partial(_sample_cmp_kernel, past_len=past_len),
        grid=(db // SEQ_PER_STEP,),
        in_specs=[pl.BlockSpec((SEQ_PER_STEP, NSA_HEADS, LANES), lambda b: (b, 0, 0)),
                  pl.BlockSpec((1, 1, SEQ_PER_STEP, ncp, LANES), lambda b: (layer, 0, b, 0, 0)),
                  pl.BlockSpec((1, 1, SEQ_PER_STEP, ncp, LANES), lambda b: (layer, 1, b, 0, 0)),
                  _const_spec(ov.shape)],
        out_specs=[pl.BlockSpec((SEQ_PER_STEP, NSA_HEADS, LANES), lambda b: (b, 0, 0)),
                   pl.BlockSpec((1, SUBLANES, LANES), lambda b: (b, 0, 0))],
        out_shape=[jax.ShapeDtypeStruct((db, NSA_HEADS, LANES), F32),
                   jax.ShapeDtypeStruct((db // SEQ_PER_STEP, SUBLANES, LANES), jnp.int32)],
        compiler_params=_cparams("parallel"),
    )(q8, cmp_past, cmp_past, ov)
    return o_cmp, picked.reshape(db, KV_GROUPS, LANES)[:, :, :SEL_TOPN]


def _sample_attn_kernel(idx_ref, pt_ref, *refs):
    del pt_ref
    kpage = refs[:N_PICK]
    vpage = refs[N_PICK:2 * N_PICK]
    q_ref, kv_ref, gd_ref, ocmp_ref, win_ref, o_ref = refs[2 * N_PICK:]
    b = pl.program_id(0)
    g = pl.program_id(1)
    q = q_ref[0]
    qf = q.astype(F32)
    kvn = kv_ref[0]
    lane = lax.broadcasted_iota(jnp.int32, (1, LANES), 1)
    chan = lax.broadcasted_iota(jnp.int32, (LANES, 1), 0)
    tok_half = lax.broadcasted_iota(jnp.int32, (1, PAGE_SIZE), 1) // SEL_BLOCK

    def branch(gg, keys_t, vals_t, tok_masks, j_new):
        own = (lane >= gg * HEAD_DIM) & (lane < (gg + 1) * HEAD_DIM)
        own_t = (chan >= gg * HEAD_DIM) & (chan < (gg + 1) * HEAD_DIM)
        ss = []
        for kt, tm in zip(keys_t, tok_masks):
            s = jnp.dot(q, kt.astype(BF16), preferred_element_type=F32)
            ss.append(s if tm is None else jnp.where(tm, s, NEG_INF))
        k_new = kvn[:, j_new * LANES:(j_new + 1) * LANES].astype(BF16).astype(F32)
        s_new = jnp.sum(qf * k_new, axis=-1, keepdims=True)
        m = s_new
        for s in ss:
            m = jnp.maximum(m, jnp.max(s, axis=-1, keepdims=True))
        v_new = jnp.where(own, kvn[:, (j_new + 1) * LANES:(j_new + 2) * LANES], 1.0).astype(BF16).astype(F32)
        acc = jnp.exp2(s_new - m).astype(BF16).astype(F32) * v_new
        for s, vt in zip(ss, vals_t):
            v_one_t = jnp.where(own_t, vt, 1.0).astype(BF16)
            acc = acc + lax.dot_general(jnp.exp2(s - m).astype(BF16), v_one_t, NT_DIMS, preferred_element_type=F32)
        return acc

    gd = gd_ref[0]
    ocmp = ocmp_ref[0]
    for gg in range(KV_GROUPS):
        @pl.when(g == gg)
        def _(gg=gg):
            den = (1 - gg) * HEAD_DIM
            masks = [tok_half == idx_ref[b, gg, j] % BLOCKS_PER_PAGE for j in range(N_PICK)]
            acc = branch(gg, [r[...] for r in kpage], [r[...] for r in vpage], masks, 2)
            o_slc = acc / acc[:, den:den + 1]
            acc = branch(gg, [win_ref[0]], [win_ref[1]], [None], 4)
            o_win = acc / acc[:, den:den + 1]
            for r in range(NSA_REP):
                hh = NSA_REP * gg + r
                mix = (gd[:, 3 * hh:3 * hh + 1] * ocmp[hh:hh + 1] + gd[:, 3 * hh + 1:3 * hh + 2] * o_slc[hh:hh + 1]
                       + gd[:, 3 * hh + 2:3 * hh + 3] * o_win[hh:hh + 1])
                o_ref[0, :, hh * HEAD_DIM:(hh + 1) * HEAD_DIM] = mix[:, gg * HEAD_DIM:(gg + 1) * HEAD_DIM]


def sample_attend(q8, kv, gd, ocmp, idx, page_table, cache_t, win_t, layer):
    db = q8.shape[0]
    page_spec = lambda which, j: pl.BlockSpec(
        (None, None, None, LANES, PAGE_SIZE),
        lambda b, g, ix, pt, j=j: (pt[b, ix[b, g, j] // BLOCKS_PER_PAGE], layer, which, 0, 0))
    per_b = lambda shape: pl.BlockSpec((1,) + shape, lambda b, g, ix, pt: (b,) + (0,) * len(shape))
    wlen = win_t.shape[4]
    grid_spec = pltpu.PrefetchScalarGridSpec(
        num_scalar_prefetch=2,
        grid=(db, KV_GROUPS),
        in_specs=[page_spec(2, j) for j in range(N_PICK)] + [page_spec(3, j) for j in range(N_PICK)]
                 + [per_b((NSA_HEADS, LANES)), per_b((1, 6 * LANES)), per_b((1, LANES)), per_b((NSA_HEADS, LANES)),
                    pl.BlockSpec((None, None, 2, LANES, wlen), lambda b, g, ix, pt: (b, layer, 0, 0, 0))],
        out_specs=per_b((1, NSA_HEADS * HEAD_DIM)),
    )
    return pl.pallas_call(
        _sample_attn_kernel,
        grid_spec=grid_spec,
        out_shape=jax.ShapeDtypeStruct((db, 1, NSA_HEADS * HEAD_DIM), F32),
        compiler_params=_cparams("parallel", "arbitrary"),
    )(idx, page_table, *([cache_t] * (2 * N_PICK)), q8, kv, gd, ocmp, win_t)


def _sample_state_kernel(u_ref, xbc_ref, z_ref, gd_ref, pool_ref, conv_ref, h_ref,
                         pw_ref, ps_ref, cw_ref, cb_ref, a_ref, dsk_ref, ng_ref,
                         ypool_ref, yssm_ref, npool_ref, nconv_ref, nh_ref):
    pd = SSM_HEADS * HEAD_DIM
    u = u_ref[0]
    ext = jnp.concatenate([pool_ref[...], u], axis=0)
    row = lax.broadcasted_iota(jnp.int32, ext.shape, 0)
    lane = lax.broadcasted_iota(jnp.int32, (1, ext.shape[1]), 1)
    d = jnp.zeros_like(u)
    for gi, win in enumerate(POOL_WINDOWS):
        s = jnp.sum(jnp.where(row >= POOL_BUF + 1 - win, ext, 0.0), axis=0, keepdims=True)
        d = jnp.where((lane >= gi * HEAD_DIM) & (lane < (gi + 1) * HEAD_DIM), s / float(win) - u, d)
    d8 = jnp.concatenate([d, jnp.zeros((SUBLANES - 1, d.shape[1]), F32)], axis=0).astype(BF16)
    ypool_ref[0] = jnp.dot(d8, pw_ref[...], preferred_element_type=F32)[0:1] * ps_ref[...]
    npool_ref[...] = ext[1:]

    xbc = xbc_ref[0]
    extc = jnp.concatenate([conv_ref[...], xbc], axis=0)
    act = _silu(jnp.sum(extc * cw_ref[...], axis=0, keepdims=True) + cb_ref[...])
    nconv_ref[...] = extc[1:]
    gd = gd_ref[0]
    dec_row = jnp.exp(gd * a_ref[...])
    eye = (lax.broadcasted_iota(jnp.int32, (HEAD_DIM, HEAD_DIM), 0)
           == lax.broadcasted_iota(jnp.int32, (HEAD_DIM, HEAD_DIM), 1))
    ys = []
    for hh in range(SSM_HEADS):
        g = hh // (SSM_HEADS // SSM_GROUPS)
        x_row = act[:, hh * HEAD_DIM:(hh + 1) * HEAD_DIM]
        dt = gd[:, DT_LANE0 + hh:DT_LANE0 + hh + 1]
        xdt_col = jnp.sum(jnp.where(eye, x_row * dt, 0.0), axis=1, keepdims=True)
        b_row = act[:, pd + g * SSM_STATE:pd + (g + 1) * SSM_STATE]
        c_row = act[:, pd + (SSM_GROUPS + g) * SSM_STATE:pd + (SSM_GROUPS + g + 1) * SSM_STATE]
        h_new = dec_row[:, DT_LANE0 + hh:DT_LANE0 + hh + 1] * h_ref[hh] + xdt_col * b_row
        nh_ref[hh] = h_new
        y_col = jnp.sum(h_new * c_row, axis=1, keepdims=True)
        y_row = jnp.sum(jnp.where(eye, y_col, 0.0), axis=0, keepdims=True)
        ys.append(y_row + dsk_ref[:, hh * HEAD_DIM:(hh + 1) * HEAD_DIM] * x_row)
    y = jnp.concatenate(ys, axis=1) * _silu(z_ref[0])
    outs = []
    for g in range(SSM_GROUPS):
        yg = y[:, g * SSM_STATE:(g + 1) * SSM_STATE]
        ms = jnp.mean(yg * yg, axis=-1, keepdims=True)
        outs.append(yg * lax.rsqrt(ms + EPS) * ng_ref[:, g * SSM_STATE:(g + 1) * SSM_STATE])
    yssm_ref[0] = jnp.concatenate(outs, axis=1)


def sample_state_mixers(u, xbc, z, gd, state_pool, state_conv, state_ssm, layer, lw):
    db = u.shape[0]
    per_b = lambda shape: pl.BlockSpec((1,) + shape, lambda b: (b,) + (0,) * len(shape))
    st = lambda shape: pl.BlockSpec((None, None) + shape, lambda b: (b, layer) + (0,) * len(shape))
    new = lambda shape: pl.BlockSpec((None,) + shape, lambda b: (b,) + (0,) * len(shape))
    ps, cs, hs = state_pool.shape[2:], state_conv.shape[2:], state_ssm.shape[2:]
    params = [lw['pool_w'], lw['pool_scale'], lw['conv_w'], lw['conv_b'], lw['a_row'], lw['d_skip'], lw['ssm_norm']]
    return pl.pallas_call(
        _sample_state_kernel,
        grid=(db,),
        in_specs=[per_b(u.shape[1:]), per_b(xbc.shape[1:]), per_b(z.shape[1:]), per_b(gd.shape[1:]),
                  st(ps), st(cs), st(hs)] + [_const_spec(p.shape) for p in params],
        out_specs=[per_b((1, ps[1])), per_b((1, z.shape[2])), new(ps), new(cs), new(hs)],
        out_shape=[jax.ShapeDtypeStruct((db, 1, ps[1]), F32), jax.ShapeDtypeStruct((db, 1, z.shape[2]), F32),
                   jax.ShapeDtypeStruct((db,) + ps, F32), jax.ShapeDtypeStruct((db,) + cs, F32),
                   jax.ShapeDtypeStruct((db,) + hs, F32)],
        compiler_params=_cparams("parallel"),
    )(u, xbc, z, gd, state_pool, state_conv, state_ssm, *params)


def _prep_layer(p, i):
    d = p['w_in'].shape[1]
    w_in = p['w_in'][i]
    o = np.cumsum([0, POOL_DIM, NSA_DIM, 6 * KV_GROUPS * HEAD_DIM, GATE_LANES, SSM_DIM, SSM_CONV_DIM, SSM_HEADS])
    w_q = w_in[:, o[1]:o[2]].reshape(d, NSA_HEADS, HEAD_DIM)
    slot = jnp.zeros((d, NSA_HEADS, KV_GROUPS, HEAD_DIM), F32)
    for hh in range(NSA_HEADS):
        slot = slot.at[:, hh, hh // NSA_REP].set(w_q[:, hh])
    w_gd = jnp.zeros((d, LANES), F32)
    w_gd = w_gd.at[:, :GATE_LANES].set(w_in[:, o[3]:o[4]])
    w_gd = w_gd.at[:, DT_LANE0:DT_LANE0 + SSM_HEADS].set(w_in[:, o[6]:o[7]])
    lane_pad = lambda v: jnp.zeros((1, LANES), F32).at[0, DT_LANE0:DT_LANE0 + SSM_HEADS].set(v)
    eye_g = jnp.eye(KV_GROUPS, dtype=F32)
    ratio = CMP_BLOCK // CMP_STRIDE
    cmp_w, cmp_pe = [], []
    for j in range(2):
        w4 = p['nsa_cmp_w'][i, j].reshape(ratio, CMP_STRIDE, HEAD_DIM, HEAD_DIM)
        cmp_w.append(jnp.einsum('rjde,gh->jgdrhe', w4, eye_g).reshape(CMP_STRIDE * 2 * HEAD_DIM, ratio * LANES))
        pe4 = p['nsa_cmp_pe'][i, j].reshape(ratio, CMP_STRIDE, 1, HEAD_DIM)
        pe_rows = jnp.broadcast_to(pe4, (ratio, CMP_STRIDE, KV_GROUPS, HEAD_DIM)).reshape(ratio, -1)
        cmp_pe.append(jnp.zeros((SUBLANES, pe_rows.shape[1]), F32).at[:ratio].set(pe_rows))
    pool_w = jnp.zeros((POOL_DIM, POOL_DIM), F32)
    for gi in range(len(POOL_WINDOWS)):
        sl = slice(gi * HEAD_DIM, (gi + 1) * HEAD_DIM)
        pool_w = pool_w.at[sl, sl].set(p['pool_w'][i, gi])
    w_out = p['w_out'][i]
    bf = lambda x: x.astype(BF16)
    row = lambda x: x.reshape(1, -1)
    return {
        'ffn1_norm': row(p['ffn1_norm'][i]), 'ffn1_w_gate': bf(p['ffn1_w_gate'][i]),
        'ffn1_w_up': bf(p['ffn1_w_up'][i]), 'ffn1_w_down': bf(p['ffn1_w_down'][i]),
        'ffn2_norm': row(p['ffn2_norm'][i]), 'ffn2_w_gate': bf(p['ffn2_w_gate'][i]),
        'ffn2_w_up': bf(p['ffn2_w_up'][i]), 'ffn2_w_down': bf(p['ffn2_w_down'][i]),
        'mix_norm': row(p['mix_norm'][i]),
        'w_u': bf(w_in[:, o[0]:o[1]]), 'w_q': bf(slot.reshape(d, NSA_HEADS * LANES)),
        'w_kv': bf(w_in[:, o[2]:o[3]]), 'w_gd': bf(w_gd), 'w_z': bf(w_in[:, o[4]:o[5]]),
        'w_xbc': bf(w_in[:, o[5]:o[6]]),
        'q_norm': row(jnp.tile(p['nsa_q_norm'][i], 2)), 'k_norm': jnp.tile(p['nsa_k_norm'][i], (1, 2)),
        'dt_bias': lane_pad(p['ssm_dt_bias'][i]),
        'w_out_pool': bf(w_out[:POOL_DIM]), 'w_out_nsa': bf(w_out[POOL_DIM:POOL_DIM + NSA_DIM]),
        'w_out_ssm': bf(w_out[POOL_DIM + NSA_DIM:]),
        'pool_w': bf(pool_w), 'pool_scale': row(p['pool_scale'][i]),
        'cmp_w': bf(jnp.stack(cmp_w)), 'cmp_pe': bf(jnp.stack(cmp_pe)),
        'conv_w': p['ssm_conv_w'][i], 'conv_b': row(p['ssm_conv_b'][i]),
        'a_row': lane_pad(-jnp.exp(p['ssm_a_log'][i])),
        'd_skip': row(jnp.repeat(p['ssm_d'][i], HEAD_DIM)), 'ssm_norm': row(p['ssm_norm'][i]),
        'ple_norm': row(p['ple_norm'][i]), 'ple_w_gate': bf(p['ple_w_gate'][i]),
        'ple_w_proj': bf(p['ple_w_proj'][i]),
    }


def _rope_tables(pos):
    half = HEAD_DIM // 2
    inv = ROPE_THETA ** (-jnp.arange(half, dtype=F32) / half)
    ang = pos.astype(F32)[:, None] * inv[None, :]
    cos = jnp.cos(ang)
    sin = jnp.sin(ang)
    return jnp.tile(cos, (1, 4)), jnp.tile(jnp.concatenate([-sin, sin], axis=1), (1, 2))


def _selection_constants(t):
    nseg = t // CMP_STRIDE
    nc = nseg - CMP_BLOCK // CMP_STRIDE + 1
    ns = t // SEL_BLOCK
    c_start = np.arange(nseg) * CMP_STRIDE
    s_start = np.arange(SEL_PAD) * SEL_BLOCK
    ovt = ((c_start[None, :] < s_start[:, None] + SEL_BLOCK) & (c_start[None, :] + CMP_BLOCK > s_start[:, None])
           & (np.arange(nseg)[None, :] < nc) & (np.arange(SEL_PAD)[:, None] < ns))
    onehot = (np.arange(t)[:, None] // SEL_BLOCK) == np.arange(SEL_PAD)[None, :]
    return jnp.asarray(onehot, BF16), jnp.asarray(ovt, BF16)


def _channel_major(cache):
    nd = cache.ndim
    t = jnp.transpose(cache, tuple(range(nd - 3)) + (nd - 2, nd - 1, nd - 3))
    return t.reshape(cache.shape[:nd - 3] + (cache.shape[-2] * cache.shape[-1], cache.shape[-3]))


def _token_layer_front(h, lw, cos, sin, seq_len=None):
    h = ffn_halfstep(h, lw['ffn1_norm'], lw['ffn1_w_gate'], lw['ffn1_w_up'], lw['ffn1_w_down'])
    return (h,) + tuple(in_projection(h, lw, cos, sin, seq_len))


def _token_layer_back(h, y_pool, y_nsa, y_ssm, pe, lw):
    h = out_projection(h, y_pool, y_nsa, y_ssm, lw)
    h = ffn_halfstep(h, lw['ffn2_norm'], lw['ffn2_w_gate'], lw['ffn2_w_up'], lw['ffn2_w_down'])
    return ple_step(h, pe, lw)


def kernel(x_prompt, x_sample, cache_nsa_kv, cache_win_kv, state_pool, state_conv, state_ssm, page_table,
           p_prompt, p_sample, ffn1_norm, ffn1_w_gate, ffn1_w_up, ffn1_w_down, mix_norm, w_in, w_out,
           pool_w, pool_scale, nsa_q_norm, nsa_k_norm, nsa_cmp_pe, nsa_cmp_w, ssm_conv_w, ssm_conv_b,
           ssm_dt_bias, ssm_a_log, ssm_d, ssm_norm, ffn2_norm, ffn2_w_gate, ffn2_w_up, ffn2_w_down,
           ple_norm, ple_w_gate, ple_w_proj):
    params = dict(ffn1_norm=ffn1_norm, ffn1_w_gate=ffn1_w_gate, ffn1_w_up=ffn1_w_up, ffn1_w_down=ffn1_w_down,
                  mix_norm=mix_norm, w_in=w_in, w_out=w_out, pool_w=pool_w, pool_scale=pool_scale,
                  nsa_q_norm=nsa_q_norm, nsa_k_norm=nsa_k_norm, nsa_cmp_pe=nsa_cmp_pe, nsa_cmp_w=nsa_cmp_w,
                  ssm_conv_w=ssm_conv_w, ssm_conv_b=ssm_conv_b, ssm_dt_bias=ssm_dt_bias, ssm_a_log=ssm_a_log,
                  ssm_d=ssm_d, ssm_norm=ssm_norm, ffn2_norm=ffn2_norm, ffn2_w_gate=ffn2_w_gate,
                  ffn2_w_up=ffn2_w_up, ffn2_w_down=ffn2_w_down, ple_norm=ple_norm, ple_w_gate=ple_w_gate,
                  ple_w_proj=ple_w_proj)
    depth = w_in.shape[0]
    b, t, d = x_prompt.shape
    db = x_sample.shape[0]
    past_len = page_table.shape[1] * PAGE_SIZE
    wkeep = min(WINDOW, t)

    cos_p, sin_p = _rope_tables(jnp.tile(jnp.arange(t, dtype=jnp.int32), b))
    cos_s, sin_s = _rope_tables(jnp.full((db,), past_len, jnp.int32))
    onehot, ovt = _selection_constants(t)

    lws = [_prep_layer(params, i) for i in range(depth)]
    cache_t, win_t = _channel_major(cache_nsa_kv), _channel_major(cache_win_kv)
    cmp_past = past_compress(cache_t, page_table, jnp.stack([lw['cmp_w'] for lw in lws]),
                             jnp.stack([lw['cmp_pe'] for lw in lws]))
    ov_past = _selection_constants(past_len)[1].T
    s3 = lambda x: x.reshape(db, 1, x.shape[-1])

    h_p = x_prompt.reshape(b * t, d)
    h_s = x_sample.reshape(db, d)
    st_p = [[] for _ in range(5)]
    st_s = [[] for _ in range(5)]
    for i in range(depth):
        lw = lws[i]
        h_p, u, qp, rows, wins, kvb, seg, gd, z, xbc = _token_layer_front(h_p, lw, cos_p, sin_p, t)
        r3 = lambda x: x.reshape(b, t, x.shape[-1])
        y_pool = pool_prompt(r3(u), lw)
        cmp = nsa_compress(seg.reshape(2, b, t // CMP_STRIDE, CMP_STRIDE * LANES), lw['cmp_w'], lw['cmp_pe'])
        y_nsa = nsa_prompt(r3(qp), r3(gd), cmp, r3(kvb), onehot, ovt)
        y_ssm, h_fin = ssd_prompt(r3(xbc), r3(z), r3(gd), lw)
        h_p = _token_layer_back(h_p, y_pool.reshape(b * t, -1), y_nsa.reshape(b * t, -1),
                                y_ssm.reshape(b * t, -1), p_prompt[i].reshape(b * t, -1), lw)
        st_p[0].append(rows.reshape(b, 4, t, KV_GROUPS, HEAD_DIM))
        st_p[1].append(wins[:, :, t - wkeep:].reshape(b, 2, wkeep, KV_GROUPS, HEAD_DIM))
        st_p[2].append(r3(u)[:, t - POOL_BUF:])
        st_p[3].append(r3(xbc)[:, t - (SSM_CONV - 1):])
        st_p[4].append(h_fin)
        h_s, u, qp, kv, gd, z, xbc = _token_layer_front(h_s, lw, cos_s, sin_s)
        q8 = qp.reshape(db, NSA_HEADS, LANES)
        o_cmp, picked = sample_cmp_select(q8, cmp_past, i, ov_past, past_len)
        y_nsa = sample_attend(q8, s3(kv), s3(gd), o_cmp, picked, page_table, cache_t, win_t, i)
        y_pool, y_ssm, new_pool, new_conv, new_h = sample_state_mixers(
            s3(u), s3(xbc), s3(z), s3(gd), state_pool, state_conv, state_ssm, i, lw)
        h_s = _token_layer_back(h_s, y_pool.reshape(db, -1), y_nsa.reshape(db, -1), y_ssm.reshape(db, -1),
                                p_sample[i].reshape(db, -1), lw)
        kv6 = kv.reshape(db, 6, 1, KV_GROUPS, HEAD_DIM)
        rows = kv6[:, 0:4]
        new_win = jnp.concatenate([cache_win_kv[:, i, :, 1:], kv6[:, 4:6]], axis=2)
        for j, v in enumerate((rows, new_win, new_pool, new_conv, new_h)):
            st_s[j].append(v)
    outs = [h_p.reshape(b, t, d), h_s.reshape(db, 1, d)]
    for j in range(5):
        outs.append(jnp.stack(st_p[j], axis=1))
        outs.append(jnp.stack(st_s[j], axis=1))
    return tuple(outs)
```
